```python
import math
import jax, jax.numpy as jnp
from jax import lax
import numpy as np

D_MODEL = 1024
BATCH = 16
SEQ = 4096
DEPTH = 2
DEC_BATCH = 8
DEC_SEQ = 16
PAST_LEN = 1024

CHUNK = 64
N_HEADS = 8
HEAD_DIM = 64
D_ATTN = N_HEADS * HEAD_DIM
N_IDX_HEADS = 8
IDX_DIM = 64
IDX_W_SCALE = (N_IDX_HEADS * IDX_DIM) ** -0.5
TOPK_MAX = 256
SSM_GROUP = 16
D_SSM = D_MODEL // 2
N_SSM_GROUPS = D_SSM // SSM_GROUP
SSM_STATE = 64
DT_MIN = 1e-3
DT_MAX = 1e-1
D_FF = 2816
N_EXPERTS = 8
TOP_K_EXPERTS = 2
D_FF_EXPERT = 3584
RMS_EPS = 1e-6
NEG_INF = -1e30
IN_SIZES = (D_ATTN, D_ATTN, D_ATTN, N_IDX_HEADS * IDX_DIM, IDX_DIM, N_IDX_HEADS, D_SSM, D_MODEL, D_MODEL)
N_IN = 3 * D_ATTN + N_IDX_HEADS * IDX_DIM + IDX_DIM + N_IDX_HEADS + D_SSM + 2 * D_MODEL

kernel_name = "chunk_causal_dsa_s5_hybrid_step"


def rmsnorm(x, g):
    xf = x.astype(jnp.float32)
    y = xf * lax.rsqrt(jnp.mean(xf * xf, axis=-1, keepdims=True) + RMS_EPS)
    return (y * g.astype(jnp.float32)).astype(x.dtype)


def project_in(hn, w):
    offs = []
    acc = 0
    for s in IN_SIZES[:-1]:
        acc += s
        offs.append(acc)
    q, k, v, iq, ik, iw, u, ga, gb = jnp.split(hn @ w, offs, axis=-1)
    lead = hn.shape[:-1]
    q = q.reshape(*lead, N_HEADS, HEAD_DIM)
    k = k.reshape(*lead, N_HEADS, HEAD_DIM)
    v = v.reshape(*lead, N_HEADS, HEAD_DIM)
    iq = iq.reshape(*lead, N_IDX_HEADS, IDX_DIM)
    return q, k, v, iq, ik, iw * IDX_W_SCALE, u, ga, gb


def indexer_scores(iq, iw, ik):
    s = jax.nn.relu(jnp.einsum('bthd,bsd->bths', iq, ik).astype(jnp.float32))
    return jnp.einsum('bth,bths->bts', iw.astype(jnp.float32), s)


def gather_rows(rows, idx):
    return jax.vmap(lambda r, i: r[i])(rows, idx)


def sparse_attend(q, k, v, idx, valid):
    ks = gather_rows(k, idx)
    vs = gather_rows(v, idx)
    s = jnp.einsum('bthd,btkhd->bthk', q, ks).astype(jnp.float32) * (HEAD_DIM ** -0.5)
    s = jnp.where(valid[:, :, None, :], s, NEG_INF)
    p = jax.nn.softmax(s, axis=-1).astype(v.dtype)
    return jnp.einsum('bthk,btkhd->bthd', p, vs)


def dsa_prompt(q, k, v, iq, iw, ik):
    b, t = q.shape[:2]
    n_keep = min(TOPK_MAX, t // 4)
    key_pos = jnp.arange(t)

    def one_chunk(c):
        start = c * CHUNK
        limit = start + CHUNK
        qb = lax.dynamic_slice_in_dim(q, start, CHUNK, axis=1)
        iqb = lax.dynamic_slice_in_dim(iq, start, CHUNK, axis=1)
        iwb = lax.dynamic_slice_in_dim(iw, start, CHUNK, axis=1)
        sc = indexer_scores(iqb, iwb, ik)
        sc = jnp.where(key_pos < limit, sc, NEG_INF)
        _, idx = lax.top_k(sc, n_keep)
        return sparse_attend(qb, k, v, idx, idx < limit)

    out = lax.map(one_chunk, jnp.arange(t // CHUNK))
    return jnp.moveaxis(out, 0, 1).reshape(b, t, N_HEADS, HEAD_DIM)


def dsa_sample(q, k_all, v_all, iq, iw, ik_all):
    n_keep = min(TOPK_MAX, k_all.shape[1] // 4)
    sc = indexer_scores(iq, iw, ik_all)
    _, idx = lax.top_k(sc, n_keep)
    return sparse_attend(q, k_all, v_all, idx, jnp.ones(idx.shape, dtype=bool))


def complex_affine_combine(e1, e2):
    a1r, a1i, b1r, b1i = e1
    a2r, a2i, b2r, b2i = e2
    return (a2r * a1r - a2i * a1i, a2r * a1i + a2i * a1r,
            a2r * b1r - a2i * b1i + b2r, a2r * b1i + a2i * b1r + b2i)


def s5_sequence(u, h0_re, h0_im, ab_re, ab_im, bc_re, bc_im, b_re, b_im, c_re, c_im, d):
    t = u.shape[0]
    ug = u.reshape(t, N_SSM_GROUPS, SSM_GROUP)
    bu_r = jnp.einsum('tgc,gnc->tgn', ug, b_re)
    bu_i = jnp.einsum('tgc,gnc->tgn', ug, b_im)
    x_r = bc_re * bu_r - bc_im * bu_i
    x_i = bc_re * bu_i + bc_im * bu_r
    x_r = x_r.at[0].add(ab_re * h0_re - ab_im * h0_im)
    x_i = x_i.at[0].add(ab_re * h0_im + ab_im * h0_re)
    a_r = jnp.broadcast_to(ab_re, x_r.shape)
    a_i = jnp.broadcast_to(ab_im, x_i.shape)
    _, _, h_r, h_i = lax.associative_scan(complex_affine_combine, (a_r, a_i, x_r, x_i), axis=0)
    y = jnp.einsum('tgn,gcn->tgc', h_r, c_re) - jnp.einsum('tgn,gcn->tgc', h_i, c_im)
    y = y.reshape(t, D_SSM) + d * u
    return y, h_r[-1], h_i[-1]


def s5_mixer(u, h0_re, h0_im, a_re, a_im, log_dt, b_re, b_im, c_re, c_im, d, glu_w, glu_b):
    f32 = jnp.float32
    a_re = a_re.astype(f32)
    a_im = a_im.astype(f32)
    dt = jnp.exp(log_dt.astype(f32))[:, None]
    mag = jnp.exp(dt * a_re)
    ab_re = mag * jnp.cos(dt * a_im)
    ab_im = mag * jnp.sin(dt * a_im)
    den = a_re * a_re + a_im * a_im
    nr = ab_re - 1.0
    ni = ab_im
    bc_re = (nr * a_re + ni * a_im) / den
    bc_im = (ni * a_re - nr * a_im) / den
    params = (ab_re, ab_im, bc_re, bc_im, b_re.astype(f32), b_im.astype(f32),
              c_re.astype(f32), c_im.astype(f32), d.astype(f32))
    y, hr, hi = lax.map(lambda a: s5_sequence(a[0], a[1], a[2], *params),
                        (u.astype(f32), h0_re.astype(f32), h0_im.astype(f32)))
    y = jax.nn.gelu(y)
    y = y * jax.nn.sigmoid(y @ glu_w.astype(f32) + glu_b.astype(f32))
    return y.astype(u.dtype), hr, hi


def merge_branches(att, ssm, ga, gb, w_pa, w_pb, w_o):
    lead = att.shape[:-2]
    a = att.reshape(*lead, D_ATTN) @ w_pa
    s = ssm @ w_pb
    return (jax.nn.sigmoid(ga) * a + jax.nn.sigmoid(gb) * s) @ w_o


def swiglu(x, wg, wu, wd):
    return (jax.nn.silu(x @ wg) * (x @ wu)) @ wd


def moe_swiglu(x, router, wg, wu, wd):
    logits = (x @ router).astype(jnp.float32)
    top_v, top_i = lax.top_k(logits, TOP_K_EXPERTS)
    w = jax.nn.softmax(top_v, axis=-1)
    gates = jnp.sum(jax.nn.one_hot(top_i, N_EXPERTS, dtype=jnp.float32) * w[..., None], axis=-2)
    out = jnp.zeros_like(x)
    for e in range(N_EXPERTS):
        out = out + gates[..., e:e + 1].astype(x.dtype) * swiglu(x, wg[e], wu[e], wd[e])
    return out


def channel_mixer(xn, l, ffn_w_gate, ffn_w_up, ffn_w_down, moe_router, moe_w_gate, moe_w_up, moe_w_down):
    i = l // 2
    if l % 2 == 0:
        return swiglu(xn, ffn_w_gate[i], ffn_w_up[i], ffn_w_down[i])
    return moe_swiglu(xn, moe_router[i], moe_w_gate[i], moe_w_up[i], moe_w_down[i])


def setup_inputs(seed: int = 0) -> dict:
    key = jax.random.key(seed)
    ks = jax.random.split(key, 32)
    f32 = jnp.float32
    n_dense = (DEPTH + 1) // 2
    n_moe = DEPTH // 2
    G, P, GC = N_SSM_GROUPS, SSM_STATE, SSM_GROUP

    def nrm(k, shape, scale):
        return scale * jax.random.normal(k, shape, f32)

    n_idx = jnp.arange(P, dtype=f32)
    return {
        "x_prompt": nrm(ks[0], (BATCH, SEQ, D_MODEL), 1.0),
        "x_sample": nrm(ks[1], (DEC_BATCH, DEC_SEQ, D_MODEL), 1.0),
        "cache_k": nrm(ks[2], (DEPTH, DEC_BATCH, PAST_LEN, N_HEADS, HEAD_DIM), 1.0),
        "cache_v": nrm(ks[3], (DEPTH, DEC_BATCH, PAST_LEN, N_HEADS, HEAD_DIM), 1.0),
        "cache_idx_k": nrm(ks[4], (DEPTH, DEC_BATCH, PAST_LEN, IDX_DIM), 1.0),
        "state_ssm_re": nrm(ks[5], (DEPTH, DEC_BATCH, G, P), 0.5),
        "state_ssm_im": nrm(ks[6], (DEPTH, DEC_BATCH, G, P), 0.5),
        "norm_mix_g": 1.0 + nrm(ks[7], (DEPTH, D_MODEL), 0.01),
        "w_in": nrm(ks[8], (DEPTH, D_MODEL, N_IN), D_MODEL ** -0.5),
        "ssm_a_re": -0.5 + nrm(ks[9], (DEPTH, G, P), 0.01),
        "ssm_a_im": math.pi * n_idx + nrm(ks[10], (DEPTH, G, P), 0.01),
        "ssm_log_dt": jax.random.uniform(ks[11], (DEPTH, G), f32, math.log(DT_MIN), math.log(DT_MAX)),
        "ssm_b_re": nrm(ks[12], (DEPTH, G, P, GC), (2 * GC) ** -0.5),
        "ssm_b_im": nrm(ks[13], (DEPTH, G, P, GC), (2 * GC) ** -0.5),
        "ssm_c_re": nrm(ks[14], (DEPTH, G, GC, P), (2 * P) ** -0.5),
        "ssm_c_im": nrm(ks[15], (DEPTH, G, GC, P), (2 * P) ** -0.5),
        "ssm_d": nrm(ks[16], (DEPTH, D_SSM), 1.0),
        "glu_w": nrm(ks[17], (DEPTH, D_SSM, D_SSM), D_SSM ** -0.5),
        "glu_b": nrm(ks[18], (DEPTH, D_SSM), 0.01),
        "w_branch_attn": nrm(ks[19], (DEPTH, D_ATTN, D_MODEL), D_ATTN ** -0.5),
        "w_branch_ssm": nrm(ks[20], (DEPTH, D_SSM, D_MODEL), D_SSM ** -0.5),
        "w_out": nrm(ks[21], (DEPTH, D_MODEL, D_MODEL), D_MODEL ** -0.5),
        "norm_ffn_g": 1.0 + nrm(ks[22], (DEPTH, D_MODEL), 0.01),
        "ffn_w_gate": nrm(ks[23], (n_dense, D_MODEL, D_FF), D_MODEL ** -0.5),
        "ffn_w_up": nrm(ks[24], (n_dense, D_MODEL, D_FF), D_MODEL ** -0.5),
        "ffn_w_down": nrm(ks[25], (n_dense, D_FF, D_MODEL), D_FF ** -0.5),
        "moe_router": nrm(ks[26], (n_moe, D_MODEL, N_EXPERTS), D_MODEL ** -0.5),
        "moe_w_gate": nrm(ks[27], (n_moe, N_EXPERTS, D_MODEL, D_FF_EXPERT), D_MODEL ** -0.5),
        "moe_w_up": nrm(ks[28], (n_moe, N_EXPERTS, D_MODEL, D_FF_EXPERT), D_MODEL ** -0.5),
        "moe_w_down": nrm(ks[29], (n_moe, N_EXPERTS, D_FF_EXPERT, D_MODEL), D_FF_EXPERT ** -0.5),
        "final_norm_g": 1.0 + nrm(ks[30], (D_MODEL,), 0.01),
    }


def reference(x_prompt, x_sample, cache_k, cache_v, cache_idx_k, state_ssm_re, state_ssm_im,
              norm_mix_g, w_in, ssm_a_re, ssm_a_im, ssm_log_dt, ssm_b_re, ssm_b_im, ssm_c_re, ssm_c_im,
              ssm_d, glu_w, glu_b, w_branch_attn, w_branch_ssm, w_out, norm_ffn_g,
              ffn_w_gate, ffn_w_up, ffn_w_down, moe_router, moe_w_gate, moe_w_up, moe_w_down,
              final_norm_g):
    xp = x_prompt
    xs = x_sample
    kp, vp, ikp, hrp, hip = [], [], [], [], []
    kd, vd, ikd, hrd, hid = [], [], [], [], []
    ffn_args = (ffn_w_gate, ffn_w_up, ffn_w_down, moe_router, moe_w_gate, moe_w_up, moe_w_down)
    for l in range(DEPTH):
        ssm_p = (ssm_a_re[l], ssm_a_im[l], ssm_log_dt[l], ssm_b_re[l], ssm_b_im[l],
                 ssm_c_re[l], ssm_c_im[l], ssm_d[l], glu_w[l], glu_b[l])
        out_p = (w_branch_attn[l], w_branch_ssm[l], w_out[l])

        hn = rmsnorm(xp, norm_mix_g[l])
        q, k, v, iq, ik, iw, u, ga, gb = project_in(hn, w_in[l])
        att = dsa_prompt(q, k, v, iq, iw, ik)
        h0 = jnp.zeros((xp.shape[0], N_SSM_GROUPS, SSM_STATE), jnp.float32)
        ssm, hr, hi = s5_mixer(u, h0, h0, *ssm_p)
        xp = xp + merge_branches(att, ssm, ga, gb, *out_p)
        xp = xp + channel_mixer(rmsnorm(xp, norm_ffn_g[l]), l, *ffn_args)
        kp.append(k)
        vp.append(v)
        ikp.append(ik)
        hrp.append(hr)
        hip.append(hi)

        hn = rmsnorm(xs, norm_mix_g[l])
        q, k, v, iq, ik, iw, u, ga, gb = project_in(hn, w_in[l])
        k_all = jnp.concatenate([cache_k[l], k], axis=1)
        v_all = jnp.concatenate([cache_v[l], v], axis=1)
        ik_all = jnp.concatenate([cache_idx_k[l], ik], axis=1)
        att = dsa_sample(q, k_all, v_all, iq, iw, ik_all)
        ssm, hr, hi = s5_mixer(u, state_ssm_re[l], state_ssm_im[l], *ssm_p)
        xs = xs + merge_branches(att, ssm, ga, gb, *out_p)
        xs = xs + channel_mixer(rmsnorm(xs, norm_ffn_g[l]), l, *ffn_args)
        kd.append(k)
        vd.append(v)
        ikd.append(ik)
        hrd.append(hr)
        hid.append(hi)

    y_prompt = rmsnorm(xp, final_norm_g)
    y_sample = rmsnorm(xs, final_norm_g)
    new_k_prompt = jnp.stack(kp, 0)
    new_v_prompt = jnp.stack(vp, 0)
    new_idx_k_prompt = jnp.stack(ikp, 0)
    new_ssm_re_prompt = jnp.stack(hrp, 0)
    new_ssm_im_prompt = jnp.stack(hip, 0)
    new_k_sample = jnp.stack(kd, 0)
    new_v_sample = jnp.stack(vd, 0)
    new_idx_k_sample = jnp.stack(ikd, 0)
    new_ssm_re_sample = jnp.stack(hrd, 0)
    new_ssm_im_sample = jnp.stack(hid, 0)
    return (y_prompt, y_sample, new_k_prompt, new_v_prompt, new_idx_k_prompt, new_ssm_re_prompt,
            new_ssm_im_prompt, new_k_sample, new_v_sample, new_idx_k_sample, new_ssm_re_sample,
            new_ssm_im_sample)
```

```python
import functools
import math

import jax
import jax.numpy as jnp
from jax import lax
from jax.experimental import pallas as pl
from jax.experimental.pallas import tpu as pltpu

F32 = jnp.float32
BF16 = jnp.bfloat16
I32 = jnp.int32

CHUNK = 64
N_HEADS = 8
HEAD_DIM = 64
N_IDX_HEADS = 8
IDX_DIM = 64
TOPK_MAX = 256
SSM_GROUP = 16
SSM_STATE = 64
TOP_K_EXPERTS = 2
RMS_EPS = 1e-6
NEG_INF = -1e30
INT32_MIN = -(2 ** 31)

D_ATTN = N_HEADS * HEAD_DIM
D_IDX = N_IDX_HEADS * IDX_DIM
IDX_W_SCALE = float(D_IDX) ** -0.5
ATTN_SCALE = float(HEAD_DIM) ** -0.5

V7X_LANES = 128
V7X_VMEM_LIMIT_BYTES = 56 * 1024 * 1024


def _params(*semantics):
    return pltpu.CompilerParams(dimension_semantics=semantics, vmem_limit_bytes=V7X_VMEM_LIMIT_BYTES)


def _sigmoid(x):
    return 1.0 / (1.0 + jnp.exp(-x))


def _rms(x, g):
    return (x * lax.rsqrt(jnp.mean(x * x, axis=-1, keepdims=True) + RMS_EPS)) * g


def _const_spec(shape):
    return pl.BlockSpec(shape, lambda *_: (0,) * len(shape))


_SEG_Q, _SEG_K, _SEG_V, _SEG_IQ = 0, D_ATTN, 2 * D_ATTN, 3 * D_ATTN
_SEG_SMALL = 3 * D_ATTN + D_IDX


def _pack_w_in(w_in, d_ssm, d_model):
    offs = [0]
    for s in (D_ATTN, D_ATTN, D_ATTN, D_IDX, IDX_DIM, N_IDX_HEADS, d_ssm, d_model, d_model):
        offs.append(offs[-1] + s)
    pad = V7X_LANES - IDX_DIM - N_IDX_HEADS
    small = jnp.concatenate([w_in[:, offs[4]:offs[6]], jnp.zeros((w_in.shape[0], pad), w_in.dtype)], axis=1)
    return jnp.concatenate([w_in[:, :offs[4]], small, w_in[:, offs[6]:]], axis=1).astype(BF16)


def _inproj_body(x_ref, g_ref, w_ref, q_ref, k_ref, v_ref, iq_ref, ik_ref, iw_ref, u_ref, ga_ref, gb_ref, *, d_ssm, d_model):
    hb = _rms(x_ref[...], g_ref[...]).astype(BF16)

    def proj(lo, n):
        return jnp.dot(hb, w_ref[:, lo:lo + n], preferred_element_type=F32)

    q_ref[...] = proj(_SEG_Q, D_ATTN).astype(BF16)
    k_ref[...] = proj(_SEG_K, D_ATTN)
    v_ref[...] = proj(_SEG_V, D_ATTN)
    iq_ref[...] = proj(_SEG_IQ, D_IDX).astype(BF16)
    small = proj(_SEG_SMALL, V7X_LANES)
    ik_ref[...] = small[:, :IDX_DIM]
    iw_ref[...] = small[:, IDX_DIM:IDX_DIM + N_IDX_HEADS] * IDX_W_SCALE
    seg_u = _SEG_SMALL + V7X_LANES
    u_ref[...] = proj(seg_u, d_ssm)
    ga_ref[...] = proj(seg_u + d_ssm, d_model)
    gb_ref[...] = proj(seg_u + d_ssm + d_model, d_model)


def _inproj(x, g, w_packed, d_ssm):
    b, t, d = x.shape
    tm = min(256, t)
    n_w = w_packed.shape[1]

    def tok(n, dtype):
        return jax.ShapeDtypeStruct((b, t, n), dtype), pl.BlockSpec((None, tm, n), lambda bi, i: (bi, i, 0))

    shapes, specs = zip(
        tok(D_ATTN, BF16), tok(D_ATTN, F32), tok(D_ATTN, F32), tok(D_IDX, BF16), tok(IDX_DIM, F32),
        tok(N_IDX_HEADS, F32), tok(d_ssm, F32), tok(d, F32), tok(d, F32))
    return pl.pallas_call(
        functools.partial(_inproj_body, d_ssm=d_ssm, d_model=d),
        out_shape=shapes,
        grid=(b, t // tm),
        in_specs=[pl.BlockSpec((None, tm, d), lambda bi, i: (bi, i, 0)), _const_spec((1, d)), _const_spec((d, n_w))],
        out_specs=specs,
        compiler_params=_params("parallel", "parallel"),
        name="inproj",
    )(x, g.reshape(1, d), w_packed)


def _dsa_body(q_ref, iq_ref, iw_ref, kt_ref, v_ref, ikt_ref, o_ref, key_ref, acc_ref, m_ref, l_ref, *,
              tq, kb, n_keep, causal, n_valid, n_kb_total, pos_bits):
    row0 = pl.program_id(1) * tq
    if causal:
        n_kb = (row0 + tq + kb - 1) // kb
        rows = row0 + lax.broadcasted_iota(I32, (tq, 1), 0)
        shift = CHUNK.bit_length() - 1
        row_limit = ((rows >> shift) + 1) << shift
    else:
        n_kb = n_kb_total
        row_limit = jnp.full((tq, 1), n_valid, I32)

    def key_pos(j):
        return j * kb + lax.broadcasted_iota(I32, (tq, kb), 1)

    iq = iq_ref[...]
    iq_hm = jnp.concatenate([iq[:, h * IDX_DIM:(h + 1) * IDX_DIM] for h in range(N_IDX_HEADS)], axis=0)
    iw = iw_ref[...]
    w_cols = [iw[:, h:h + 1] for h in range(N_IDX_HEADS)]

    def idx_block(j, carry):
        d = jnp.dot(iq_hm, ikt_ref[j], preferred_element_type=F32)
        s = w_cols[0] * jnp.maximum(d[0:tq], 0.0)
        for h in range(1, N_IDX_HEADS):
            s = s + w_cols[h] * jnp.maximum(d[h * tq:(h + 1) * tq], 0.0)
        s = jnp.where(s == 0.0, 0.0, s)
        s = jnp.where(key_pos(j) < row_limit, s, NEG_INF)
        bits = lax.bitcast_convert_type(s, I32)
        key_ref[j] = jnp.where(bits < 0, bits ^ jnp.int32(0x7FFFFFFF), bits)
        return carry

    lax.fori_loop(0, n_kb, idx_block, 0)

    def count(pred):
        def body(j, acc):
            c = jnp.where(pred(key_ref[j], j), 1.0, 0.0)
            part = c[:, 0:V7X_LANES]
            for i in range(1, kb // V7X_LANES):
                part = part + c[:, i * V7X_LANES:(i + 1) * V7X_LANES]
            return acc + part
        acc = lax.fori_loop(0, n_kb, body, jnp.zeros((tq, min(kb, V7X_LANES)), F32))
        return jnp.sum(acc, axis=1, keepdims=True)

    def bit_step(i, thr):
        cand = thr + lax.shift_left(jnp.int32(1), 31 - i)
        cnt = count(lambda key, j: key >= cand)
        return jnp.where(cnt >= n_keep, cand, thr)

    thr = lax.fori_loop(0, 32, bit_step, jnp.full((tq, 1), INT32_MIN, I32))

    need = n_keep - count(lambda key, j: key > thr)
    n_eq = count(lambda key, j: key == thr)
    pos_all = jnp.int32(2 ** pos_bits - 1)

    def tie_search():
        def step(i, end):
            cand = end + lax.shift_left(jnp.int32(1), pos_bits - 1 - i)
            cnt = count(lambda key, j: (key == thr) & (key_pos(j) < cand))
            return jnp.where(cnt <= need, cand, end)
        return lax.fori_loop(0, pos_bits, step, jnp.zeros((tq, 1), I32))

    tie_end = lax.cond(jnp.max(n_eq - need) > 0.0, tie_search, lambda: jnp.full((tq, 1), pos_all, I32))

    q = q_ref[...] * jnp.asarray(ATTN_SCALE, BF16)
    q_heads = [q[:, h * HEAD_DIM:(h + 1) * HEAD_DIM] for h in range(N_HEADS)]
    m_ref[...] = jnp.full(m_ref.shape, NEG_INF, F32)
    l_ref[...] = jnp.zeros(l_ref.shape, F32)
    acc_ref[...] = jnp.zeros(acc_ref.shape, F32)

    def att_block(j, carry):
        key = key_ref[j]
        pos = key_pos(j)
        keep = (pos < row_limit) & ((key > thr) | ((key == thr) & (pos < tie_end)))
        bias = jnp.where(keep, 0.0, NEG_INF)
        kt = kt_ref[j]
        for h in range(N_HEADS):
            s = jnp.dot(q_heads[h], kt[h * HEAD_DIM:(h + 1) * HEAD_DIM], preferred_element_type=F32) + bias
            m_old = m_ref[h]
            m_new = jnp.maximum(m_old, jnp.max(s, axis=1, keepdims=True))
            alpha = jnp.exp(m_old - m_new)
            p = jnp.exp(s - m_new)
            l_ref[h] = alpha * l_ref[h] + jnp.sum(p, axis=1, keepdims=True)
            v_blk = v_ref[h, pl.ds(pl.multiple_of(j * kb, kb), kb), :]
            acc_ref[h] = alpha * acc_ref[h] + jnp.dot(p.astype(BF16), v_blk, preferred_element_type=F32)
            m_ref[h] = m_new
        return carry

    lax.fori_loop(0, n_kb, att_block, 0)
    o_ref[...] = jnp.concatenate([acc_ref[h] / l_ref[h] for h in range(N_HEADS)], axis=1).astype(o_ref.dtype)


def _dsa(q, iq, iw, kt_blk, v_hm, ikt_blk, *, n_keep, causal, n_valid):
    b, t, _ = q.shape
    n_kb_total, kb = kt_blk.shape[1], kt_blk.shape[3]
    s_pad = n_kb_total * kb
    tq = min(256, t)
    assert t % tq == 0 and tq % CHUNK == 0 or not causal
    body = functools.partial(_dsa_body, tq=tq, kb=kb, n_keep=float(n_keep), causal=causal, n_valid=n_valid,
                             n_kb_total=n_kb_total, pos_bits=s_pad.bit_length())
    tok = lambda n: pl.BlockSpec((None, tq, n), lambda bi, i: (bi, i, 0))
    return pl.pallas_call(
        body,
        out_shape=jax.ShapeDtypeStruct((b, t, D_ATTN), BF16),
        grid=(b, t // tq),
        in_specs=[tok(D_ATTN), tok(D_IDX), tok(N_IDX_HEADS),
                  pl.BlockSpec((None, n_kb_total, D_ATTN, kb), lambda bi, i: (bi, 0, 0, 0)),
                  pl.BlockSpec((None, N_HEADS, s_pad, HEAD_DIM), lambda bi, i: (bi, 0, 0, 0)),
                  pl.BlockSpec((None, n_kb_total, IDX_DIM, kb), lambda bi, i: (bi, 0, 0, 0))],
        out_specs=tok(D_ATTN),
        scratch_shapes=[pltpu.VMEM((n_kb_total, tq, kb), I32),
                        pltpu.VMEM((N_HEADS, tq, HEAD_DIM), F32),
                        pltpu.VMEM((N_HEADS, tq, 1), F32),
                        pltpu.VMEM((N_HEADS, tq, 1), F32)],
        compiler_params=_params("parallel", "arbitrary"),
        name="dsa",
    )(q, iq, iw, kt_blk, v_hm, ikt_blk)


def _key_layouts(k, v, ik, kb):
    b, s, _ = k.shape
    n = s // kb
    kt_blk = k.astype(BF16).reshape(b, n, kb, D_ATTN).transpose(0, 1, 3, 2)
    ikt_blk = ik.astype(BF16).reshape(b, n, kb, IDX_DIM).transpose(0, 1, 3, 2)
    v_hm = v.astype(BF16).reshape(b, s, N_HEADS, HEAD_DIM).transpose(0, 2, 1, 3)
    return kt_blk, v_hm, ikt_blk


def _ssm_discretize_body(a_re_ref, a_im_ref, log_dt_ref, ab_re_ref, ab_im_ref, bc_re_ref, bc_im_ref):
    a_re, a_im = a_re_ref[...], a_im_ref[...]
    dt = jnp.exp(log_dt_ref[...])
    mag = jnp.exp(dt * a_re)
    ab_re = mag * jnp.cos(dt * a_im)
    ab_im = mag * jnp.sin(dt * a_im)
    den = a_re * a_re + a_im * a_im
    nr = ab_re - 1.0
    ni = ab_im
    ab_re_ref[...] = ab_re
    ab_im_ref[...] = ab_im
    bc_re_ref[...] = (nr * a_re + ni * a_im) / den
    bc_im_ref[...] = (ni * a_re - nr * a_im) / den


def _ssm_discretize(a_re, a_im, log_dt):
    g, p = a_re.shape
    out = jax.ShapeDtypeStruct((g, p), F32)
    return pl.pallas_call(_ssm_discretize_body, out_shape=(out,) * 4, name="ssm_discretize")(
        a_re, a_im, log_dt.reshape(g, 1))


def _block_diag_halves(w, dtype):
    g, r, c = w.shape
    gh = g // 2
    eye = jnp.eye(gh, dtype=w.dtype)
    halves = [jnp.einsum("grc,gh->grhc", w[i * gh:(i + 1) * gh], eye).reshape(gh * r, gh * c) for i in range(2)]
    return jnp.stack(halves, 0).astype(dtype)


def _ssm_body(u_ref, h0r_ref, h0i_ref, abr_ref, abi_ref, bcr_ref, bci_ref, br_ref, bi_ref, cr_ref, ci_ref,
              d_ref, gw_ref, gb_ref, y_ref, hr_ref, hi_ref, xr_ref, xi_ref, *, tb, nb, precise):
    d_ssm = u_ref.shape[-1]
    n_state = hr_ref.shape[-1]
    kh, sh = d_ssm // 2, n_state // 2

    def mm(a, w):
        if precise:
            return jnp.dot(a, w, preferred_element_type=F32, precision=lax.Precision.HIGHEST)
        return jnp.dot(a.astype(BF16), w, preferred_element_type=F32)

    @pl.when(pl.program_id(0) == 0)
    def _():
        hr_ref[...] = h0r_ref[...]
        hi_ref[...] = h0i_ref[...]

    u = u_ref[...].reshape(tb * nb, d_ssm)

    for half in range(2):
        cols = slice(half * sh, (half + 1) * sh)
        uh = u[:, half * kh:(half + 1) * kh]
        bu_r = mm(uh, br_ref[half])
        bu_i = mm(uh, bi_ref[half])
        bc_r, bc_i = bcr_ref[:, cols], bci_ref[:, cols]
        xr_ref[:, cols] = bc_r * bu_r - bc_i * bu_i
        xi_ref[:, cols] = bc_r * bu_i + bc_i * bu_r

    slab = min(512, n_state)
    for s0 in range(0, n_state, slab):
        cols = slice(s0, s0 + slab)
        a_r = jnp.broadcast_to(abr_ref[:, cols], (nb, slab))
        a_i = jnp.broadcast_to(abi_ref[:, cols], (nb, slab))

        def step(t, h):
            h_r, h_i = h
            rows = pl.ds(pl.multiple_of(t * nb, nb), nb)
            n_r = a_r * h_r - a_i * h_i + xr_ref[rows, cols]
            n_i = a_r * h_i + a_i * h_r + xi_ref[rows, cols]
            xr_ref[rows, cols] = n_r
            xi_ref[rows, cols] = n_i
            return n_r, n_i

        h_r, h_i = lax.fori_loop(0, tb, step, (hr_ref[:, cols], hi_ref[:, cols]), unroll=min(8, tb))
        hr_ref[:, cols] = h_r
        hi_ref[:, cols] = h_i

    ys = []
    for half in range(2):
        cols = slice(half * sh, (half + 1) * sh)
        ys.append(mm(xr_ref[:, cols], cr_ref[half]) - mm(xi_ref[:, cols], ci_ref[half]))
    y = jnp.concatenate(ys, axis=1) + d_ref[...] * u
    y = y * (0.5 * (1.0 + jnp.tanh(math.sqrt(2.0 / math.pi) * (y + 0.044715 * (y * y * y)))))
    z = mm(y, gw_ref[...]) + gb_ref[...]
    y_ref[...] = (y * _sigmoid(z)).reshape(tb, nb, d_ssm)


def _ssm(u_tm, h0_re, h0_im, disc, b_re, b_im, c_re, c_im, d, glu_w, glu_b, *, precise):
    t, nb, d_ssm = u_tm.shape
    n_state = h0_re.shape[1]
    assert nb % 8 == 0
    tb = min(32, t)
    wdt = F32 if precise else BF16
    br = _block_diag_halves(jnp.swapaxes(b_re, 1, 2), wdt)
    bi = _block_diag_halves(jnp.swapaxes(b_im, 1, 2), wdt)
    cr = _block_diag_halves(jnp.swapaxes(c_re, 1, 2), wdt)
    ci = _block_diag_halves(jnp.swapaxes(c_im, 1, 2), wdt)
    row = lambda a: a.reshape(1, -1).astype(F32)
    state_spec = _const_spec((nb, n_state))
    vec_state = _const_spec((1, n_state))
    vec_ch = _const_spec((1, d_ssm))
    return pl.pallas_call(
        functools.partial(_ssm_body, tb=tb, nb=nb, precise=precise),
        out_shape=(jax.ShapeDtypeStruct((t, nb, d_ssm), F32),
                   jax.ShapeDtypeStruct((nb, n_state), F32), jax.ShapeDtypeStruct((nb, n_state), F32)),
        grid=(t // tb,),
        in_specs=[pl.BlockSpec((tb, nb, d_ssm), lambda i: (i, 0, 0)), state_spec, state_spec,
                  vec_state, vec_state, vec_state, vec_state,
                  _const_spec(br.shape), _const_spec(bi.shape), _const_spec(cr.shape), _const_spec(ci.shape),
                  vec_ch, _const_spec((d_ssm, d_ssm)), vec_ch],
        out_specs=(pl.BlockSpec((tb, nb, d_ssm), lambda i: (i, 0, 0)), state_spec, state_spec),
        scratch_shapes=[pltpu.VMEM((tb * nb, n_state), F32), pltpu.VMEM((tb * nb, n_state), F32)],
        compiler_params=_params("arbitrary"),
        name="ssm",
    )(u_tm, h0_re, h0_im, row(disc[0]), row(disc[1]), row(disc[2]), row(disc[3]), br, bi, cr, ci,
      row(d), glu_w.astype(wdt), row(glu_b))


def _merge_body(att_ref, ssm_ref, ga_ref, gb_ref, x_ref, wpa_ref, wpb_ref, wo_ref, o_ref):
    a = jnp.dot(att_ref[...], wpa_ref[...], preferred_element_type=F32)
    s = jnp.dot(ssm_ref[...].astype(BF16), wpb_ref[...], preferred_element_type=F32)
    mix = _sigmoid(ga_ref[...]) * a + _sigmoid(gb_ref[...]) * s
    o_ref[...] = x_ref[...] + jnp.dot(mix.astype(BF16), wo_ref[...], preferred_element_type=F32)


def _merge(att, ssm, ga, gb, x, w_pa, w_pb, w_o):
    b, t, d = x.shape
    tm = min(512, t)
    tok = lambda n: pl.BlockSpec((None, tm, n), lambda bi, i: (bi, i, 0))
    d_ssm = ssm.shape[-1]
    return pl.pallas_call(
        _merge_body,
        out_shape=jax.ShapeDtypeStruct((b, t, d), F32),
        grid=(b, t // tm),
        in_specs=[tok(D_ATTN), tok(d_ssm), tok(d), tok(d), tok(d),
                  _const_spec(w_pa.shape), _const_spec(w_pb.shape), _const_spec(w_o.shape)],
        out_specs=tok(d),
        compiler_params=_params("parallel", "parallel"),
        name="merge",
    )(att, ssm, ga, gb, x, w_pa.astype(BF16), w_pb.astype(BF16), w_o.astype(BF16))


def _ffn_body(x_ref, g_ref, wg_ref, wu_ref, wd_ref, o_ref, xn_ref, *, n_f):
    f = pl.program_id(1)

    @pl.when(f == 0)
    def _():
        xn_ref[...] = _rms(x_ref[...], g_ref[...]).astype(BF16)
        o_ref[...] = x_ref[...]

    xn = xn_ref[...]
    hg = jnp.dot(xn, wg_ref[...], preferred_element_type=F32)
    hu = jnp.dot(xn, wu_ref[...], preferred_element_type=F32)
    h = (hg * _sigmoid(hg)) * hu
    o_ref[...] += jnp.dot(h.astype(BF16), wd_ref[...], preferred_element_type=F32)


def _ffn(x2d, g, wg, wu, wd):
    n, d = x2d.shape
    d_ff = wg.shape[1]
    tm = min(512, n)
    fc = d_ff
    for cand in (1408, 1024, 896, 768, 512):
        if d_ff % cand == 0:
            fc = cand
            break
    n_f = d_ff // fc
    return pl.pallas_call(
        functools.partial(_ffn_body, n_f=n_f),
        out_shape=jax.ShapeDtypeStruct((n, d), F32),
        grid=(n // tm, n_f),
        in_specs=[pl.BlockSpec((tm, d), lambda i, f: (i, 0)), _const_spec((1, d)),
                  pl.BlockSpec((d, fc), lambda i, f: (0, f)), pl.BlockSpec((d, fc), lambda i, f: (0, f)),
                  pl.BlockSpec((fc, d), lambda i, f: (f, 0))],
        out_specs=pl.BlockSpec((tm, d), lambda i, f: (i, 0)),
        scratch_shapes=[pltpu.VMEM((tm, d), BF16)],
        compiler_params=_params("parallel", "arbitrary"),
        name="ffn",
    )(x2d, g.reshape(1, d), wg.astype(BF16), wu.astype(BF16), wd.astype(BF16))


def _moe_body(x_ref, g_ref, r_ref, wg_ref, wu_ref, wd_ref, o_ref, xn_ref, gate_ref, *, n_exp):
    e, f = pl.program_id(1), pl.program_id(2)

    @pl.when((e == 0) & (f == 0))
    def _():
        xn = _rms(x_ref[...], g_ref[...])
        xn_ref[...] = xn.astype(BF16)
        o_ref[...] = x_ref[...]
        logits = jnp.dot(xn, r_ref[...], preferred_element_type=F32, precision=lax.Precision.HIGHEST)
        lane = lax.broadcasted_iota(I32, logits.shape, 1).astype(F32)
        logits = jnp.where(lane < n_exp, logits, NEG_INF)
        big = float(logits.shape[1])
        m1 = jnp.max(logits, axis=1, keepdims=True)
        i1 = jnp.min(jnp.where(logits == m1, lane, big), axis=1, keepdims=True)
        rest = jnp.where(lane == i1, NEG_INF, logits)
        m2 = jnp.max(rest, axis=1, keepdims=True)
        i2 = jnp.min(jnp.where(rest == m2, lane, big), axis=1, keepdims=True)
        e2 = jnp.exp(m2 - m1)
        w1 = 1.0 / (1.0 + e2)
        w2 = e2 / (1.0 + e2)
        gate_ref[...] = jnp.where(lane == i1, w1, 0.0) + jnp.where(lane == i2, w2, 0.0)

    gates = gate_ref[...]
    lane = lax.broadcasted_iota(I32, gates.shape, 1)
    gate = jnp.sum(jnp.where(lane == e, gates, 0.0), axis=1, keepdims=True)
    xn = xn_ref[...]
    hg = jnp.dot(xn, wg_ref[...], preferred_element_type=F32)
    hu = jnp.dot(xn, wu_ref[...], preferred_element_type=F32)
    h = (hg * _sigmoid(hg)) * hu
    o_ref[...] += gate * jnp.dot(h.astype(BF16), wd_ref[...], preferred_element_type=F32)


def _moe(x2d, g, router, wg, wu, wd):
    n, d = x2d.shape
    n_exp, _, d_ff = wg.shape
    assert TOP_K_EXPERTS == 2 and n_exp <= V7X_LANES
    tm = min(512, n)
    fc = d_ff
    for cand in (896, 1024, 768, 512):
        if d_ff % cand == 0:
            fc = cand
            break
    n_f = d_ff // fc
    r_pad = jnp.concatenate([router, jnp.zeros((d, V7X_LANES - n_exp), router.dtype)], axis=1)
    return pl.pallas_call(
        functools.partial(_moe_body, n_exp=n_exp),
        out_shape=jax.ShapeDtypeStruct((n, d), F32),
        grid=(n // tm, n_exp, n_f),
        in_specs=[pl.BlockSpec((tm, d), lambda i, e, f: (i, 0)), _const_spec((1, d)), _const_spec((d, V7X_LANES)),
                  pl.BlockSpec((None, d, fc), lambda i, e, f: (e, 0, f)),
                  pl.BlockSpec((None, d, fc), lambda i, e, f: (e, 0, f)),
                  pl.BlockSpec((None, fc, d), lambda i, e, f: (e, f, 0))],
        out_specs=pl.BlockSpec((tm, d), lambda i, e, f: (i, 0)),
        scratch_shapes=[pltpu.VMEM((tm, d), BF16), pltpu.VMEM((tm, V7X_LANES), F32)],
        compiler_params=_params("parallel", "arbitrary", "arbitrary"),
        name="moe",
    )(x2d, g.reshape(1, d), r_pad, wg.astype(BF16), wu.astype(BF16), wd.astype(BF16))


def _final_norm_body(x_ref, g_ref, o_ref):
    o_ref[...] = _rms(x_ref[...], g_ref[...])


def _final_norm(x2d, g):
    n, d = x2d.shape
    tm = min(1024, n)
    return pl.pallas_call(
        _final_norm_body,
        out_shape=jax.ShapeDtypeStruct((n, d), F32),
        grid=(n // tm,),
        in_specs=[pl.BlockSpec((tm, d), lambda i: (i, 0)), _const_spec((1, d))],
        out_specs=pl.BlockSpec((tm, d), lambda i: (i, 0)),
        compiler_params=_params("parallel"),
        name="final_norm",
    )(x2d, g.reshape(1, d))


def _layer(x, past, h0, lw, *, layer_idx, precise_ssm):
    b, t, d = x.shape
    d_ssm = lw["ssm_d"].shape[0]
    q, k, v, iq, ik, iw, u, ga, gb = _inproj(x, lw["norm_mix_g"], lw["w_in_packed"], d_ssm)

    if past is None:
        kb = min(512, t)
        k_all, v_all, ik_all, s_valid = k, v, ik, t
    else:
        ck, cv, cik = past
        k_all = jnp.concatenate([ck.reshape(b, -1, D_ATTN), k], axis=1)
        v_all = jnp.concatenate([cv.reshape(b, -1, D_ATTN), v], axis=1)
        ik_all = jnp.concatenate([cik, ik], axis=1)
        s_valid = k_all.shape[1]
        kb = 512
        pad = (-s_valid) % kb
        padf = lambda a: jnp.pad(a, ((0, 0), (0, pad), (0, 0)))
        k_all, v_all, ik_all = padf(k_all), padf(v_all), padf(ik_all)
    kt_blk, v_hm, ikt_blk = _key_layouts(k_all, v_all, ik_all, kb)
    att = _dsa(q, iq, iw, kt_blk, v_hm, ikt_blk, n_keep=min(TOPK_MAX, s_valid // 4), causal=past is None,
               n_valid=s_valid)

    ssm_tm, h_re, h_im = _ssm(u.transpose(1, 0, 2), h0[0], h0[1], lw["disc"], lw["ssm_b_re"], lw["ssm_b_im"], lw["ssm_c_re"],
                              lw["ssm_c_im"], lw["ssm_d"], lw["glu_w"], lw["glu_b"], precise=precise_ssm)
    x = _merge(att, ssm_tm.transpose(1, 0, 2), ga, gb, x, lw["w_branch_attn"], lw["w_branch_ssm"], lw["w_out"])
    x2d = x.reshape(b * t, d)
    if layer_idx % 2 == 0:
        x2d = _ffn(x2d, lw["norm_ffn_g"], lw["ffn_w_gate"], lw["ffn_w_up"], lw["ffn_w_down"])
    else:
        x2d = _moe(x2d, lw["norm_ffn_g"], lw["moe_router"], lw["moe_w_gate"], lw["moe_w_up"], lw["moe_w_down"])
    g_shape = (b, -1, SSM_STATE)
    return (x2d.reshape(b, t, d), k.reshape(b, t, N_HEADS, HEAD_DIM), v.reshape(b, t, N_HEADS, HEAD_DIM), ik,
            h_re.reshape(g_shape), h_im.reshape(g_shape))


def kernel(x_prompt, x_sample, cache_k, cache_v, cache_idx_k, state_ssm_re, state_ssm_im, norm_mix_g, w_in,
           ssm_a_re, ssm_a_im, ssm_log_dt, ssm_b_re, ssm_b_im, ssm_c_re, ssm_c_im, ssm_d, glu_w, glu_b,
           w_branch_attn, w_branch_ssm, w_out, norm_ffn_g, ffn_w_gate, ffn_w_up, ffn_w_down, moe_router,
           moe_w_gate, moe_w_up, moe_w_down, final_norm_g):
    depth = w_in.shape[0]
    d_model = x_prompt.shape[-1]
    d_ssm = ssm_d.shape[1]
    xp, xs = x_prompt, x_sample
    outs_p, outs_s = [], []
    for l in range(depth):
        lw = dict(
            norm_mix_g=norm_mix_g[l], w_in_packed=_pack_w_in(w_in[l], d_ssm, d_model),
            disc=_ssm_discretize(ssm_a_re[l], ssm_a_im[l], ssm_log_dt[l]),
            ssm_b_re=ssm_b_re[l], ssm_b_im=ssm_b_im[l], ssm_c_re=ssm_c_re[l], ssm_c_im=ssm_c_im[l],
            ssm_d=ssm_d[l], glu_w=glu_w[l], glu_b=glu_b[l],
            w_branch_attn=w_branch_attn[l], w_branch_ssm=w_branch_ssm[l], w_out=w_out[l], norm_ffn_g=norm_ffn_g[l])
        i = l // 2
        if l % 2 == 0:
            lw.update(ffn_w_gate=ffn_w_gate[i], ffn_w_up=ffn_w_up[i], ffn_w_down=ffn_w_down[i])
        else:
            lw.update(moe_router=moe_router[i], moe_w_gate=moe_w_gate[i], moe_w_up=moe_w_up[i],
                      moe_w_down=moe_w_down[i])
        n_state = ssm_a_re.shape[1] * ssm_a_re.shape[2]
        zeros = jnp.zeros((xp.shape[0], n_state), F32)
        xp, *rest_p = _layer(xp, None, (zeros, zeros), lw, layer_idx=l, precise_ssm=False)
        outs_p.append(rest_p)
        h0 = (state_ssm_re[l].reshape(xs.shape[0], n_state), state_ssm_im[l].reshape(xs.shape[0], n_state))
        xs, *rest_s = _layer(xs, (cache_k[l], cache_v[l], cache_idx_k[l]), h0, lw, layer_idx=l, precise_ssm=True)
        outs_s.append(rest_s)

    y_prompt = _final_norm(xp.reshape(-1, d_model), final_norm_g).reshape(xp.shape)
    y_sample = _final_norm(xs.reshape(-1, d_model), final_norm_g).reshape(xs.shape)
    stack = lambda outs, j: jnp.stack([o[j] for o in outs], 0)
    return (y_prompt, y_sample,
            stack(outs_p, 0), stack(outs_p, 1), stack(outs_p, 2), stack(outs_p, 3), stack(outs_p, 4),
            stack(outs_s, 0), stack(outs_s, 1), stack(outs_s, 2), stack(outs_s, 3), stack(outs_s, 4))
```

```python
import functools
import math

import jax
import jax.numpy as jnp
from jax import lax
from jax.experimental import pallas as pl
from jax.experimental.pallas import tpu as pltpu

F32 = jnp.float32
BF16 = jnp.bfloat16
I32 = jnp.int32

CHUNK = 64
N_HEADS = 8
HEAD_DIM = 64
N_IDX_HEADS = 8
IDX_DIM = 64
TOPK_MAX = 256
SSM_GROUP = 16
SSM_STATE = 64
TOP_K_EXPERTS = 2
RMS_EPS = 1e-6
NEG_INF = -1e30
INT32_MIN = -(2 ** 31)

D_ATTN = N_HEADS * HEAD_DIM
D_IDX = N_IDX_HEADS * IDX_DIM
IDX_W_SCALE = float(D_IDX) ** -0.5
ATTN_SCALE = float(HEAD_DIM) ** -0.5

V7X_LANES = 128
V7X_VMEM_LIMIT_BYTES = 56 * 1024 * 1024


def _params(*semantics):
    return pltpu.CompilerParams(dimension_semantics=semantics, vmem_limit_bytes=V7X_VMEM_LIMIT_BYTES)


def _sigmoid(x):
    return 1.0 / (1.0 + jnp.exp(-x))


def _rms(x, g):
    return (x * lax.rsqrt(jnp.mean(x * x, axis=-1, keepdims=True) + RMS_EPS)) * g


def _const_spec(shape):
    return pl.BlockSpec(shape, lambda *_: (0,) * len(shape))


_SEG_Q, _SEG_K, _SEG_V, _SEG_IQ = 0, D_ATTN, 2 * D_ATTN, 3 * D_ATTN
_SEG_SMALL = 3 * D_ATTN + D_IDX


def _pack_w_in(w_in, d_ssm, d_model):
    offs = [0]
    for s in (D_ATTN, D_ATTN, D_ATTN, D_IDX, IDX_DIM, N_IDX_HEADS, d_ssm, d_model, d_model):
        offs.append(offs[-1] + s)
    pad = V7X_LANES - IDX_DIM - N_IDX_HEADS
    small = jnp.concatenate([w_in[:, offs[4]:offs[6]], jnp.zeros((w_in.shape[0], pad), w_in.dtype)], axis=1)
    return jnp.concatenate([w_in[:, :offs[4]], small, w_in[:, offs[6]:]], axis=1).astype(BF16)


def _inproj_body(x_ref, g_ref, w_ref, q_ref, k_ref, v_ref, iq_ref, ik_ref, iw_ref, u_ref, ga_ref, gb_ref, *, d_ssm, d_model):
    hb = _rms(x_ref[...], g_ref[...]).astype(BF16)

    def proj(lo, n):
        return jnp.dot(hb, w_ref[:, lo:lo + n], preferred_element_type=F32)

    q_ref[...] = proj(_SEG_Q, D_ATTN).astype(BF16)
    k_ref[...] = proj(_SEG_K, D_ATTN)
    v_ref[...] = proj(_SEG_V, D_ATTN)
    iq_ref[...] = proj(_SEG_IQ, D_IDX).astype(BF16)
    small = proj(_SEG_SMALL, V7X_LANES)
    ik_ref[...] = small[:, :IDX_DIM]
    iw_ref[...] = small[:, IDX_DIM:IDX_DIM + N_IDX_HEADS] * IDX_W_SCALE
    seg_u = _SEG_SMALL + V7X_LANES
    u_ref[...] = proj(seg_u, d_ssm)
    ga_ref[...] = proj(seg_u + d_ssm, d_model)
    gb_ref[...] = proj(seg_u + d_ssm + d_model, d_model)


def _inproj(x, g, w_packed, d_ssm):
    b, t, d = x.shape
    tm = min(256, t)
    n_w = w_packed.shape[1]

    def tok(n, dtype):
        return jax.ShapeDtypeStruct((b, t, n), dtype), pl.BlockSpec((None, tm, n), lambda bi, i: (bi, i, 0))

    shapes, specs = zip(
        tok(D_ATTN, BF16), tok(D_ATTN, F32), tok(D_ATTN, F32), tok(D_IDX, BF16), tok(IDX_DIM, F32),
        tok(N_IDX_HEADS, F32), tok(d_ssm, F32), tok(d, F32), tok(d, F32))
    return pl.pallas_call(
        functools.partial(_inproj_body, d_ssm=d_ssm, d_model=d),
        out_shape=shapes,
        grid=(b, t // tm),
        in_specs=[pl.BlockSpec((None, tm, d), lambda bi, i: (bi, i, 0)), _const_spec((1, d)), _const_spec((d, n_w))],
        out_specs=specs,
        compiler_params=_params("parallel", "parallel"),
        name="inproj",
    )(x, g.reshape(1, d), w_packed)


def _dsa_body(qt_ref, iqt_ref, iwt_ref, k_ref, vt_ref, ik_ref, o_ref, key_ref, qs_ref, acc_ref, m_ref, l_ref, s_ref, s2_ref, *,
              tq, kb, n_keep, causal, n_valid, n_kb_total, pos_bits):
    col0 = pl.program_id(1) * tq
    if causal:
        n_kb = (col0 + tq + kb - 1) // kb
        q_pos = col0 + lax.broadcasted_iota(I32, (1, tq), 1)
        shift = CHUNK.bit_length() - 1
        q_limit = ((q_pos >> shift) + 1) << shift
    else:
        n_kb = n_kb_total
        q_limit = jnp.full((1, tq), n_valid, I32)

    def key_pos(j):
        return j * kb + lax.broadcasted_iota(I32, (kb, tq), 0)

    def key_rows(j):
        return pl.ds(pl.multiple_of(j * kb, kb), kb)

    fold_rows = min(64, kb)

    def fold(x, op):
        return op(x.reshape(kb // fold_rows, fold_rows, tq), axis=0)

    iwt = iwt_ref[...]

    def idx_block(j, carry):
        ik_blk = ik_ref[key_rows(j), :]
        s = None
        for h in range(N_IDX_HEADS):
            d = jnp.dot(ik_blk, iqt_ref[h * IDX_DIM:(h + 1) * IDX_DIM, :], preferred_element_type=F32)
            term = iwt[h:h + 1, :] * jnp.maximum(d, 0.0)
            s = term if s is None else s + term
        s = jnp.where(s == 0.0, 0.0, s)
        s = jnp.where(key_pos(j) < q_limit, s, NEG_INF)
        bits = lax.bitcast_convert_type(s, I32)
        key_ref[j] = jnp.where(bits < 0, bits ^ jnp.int32(0x7FFFFFFF), bits)
        return carry

    lax.fori_loop(0, n_kb, idx_block, 0)

    def count(pred):
        def body(j, acc):
            return acc + fold(jnp.where(pred(key_ref[j], j), 1.0, 0.0), jnp.sum)
        acc = lax.fori_loop(0, n_kb, body, jnp.zeros((fold_rows, tq), F32))
        return jnp.sum(acc, axis=0, keepdims=True)

    def bit_step(i, thr):
        cand = thr + lax.shift_left(jnp.int32(1), 31 - i)
        cnt = count(lambda key, j: key >= cand)
        return jnp.where(cnt >= n_keep, cand, thr)

    thr = lax.fori_loop(0, 32, bit_step, jnp.full((1, tq), INT32_MIN, I32))

    need = n_keep - count(lambda key, j: key > thr)
    n_eq = count(lambda key, j: key == thr)
    pos_all = jnp.int32(2 ** pos_bits - 1)

    def tie_search():
        def step(i, end):
            cand = end + lax.shift_left(jnp.int32(1), pos_bits - 1 - i)
            cnt = count(lambda key, j: (key == thr) & (key_pos(j) < cand))
            return jnp.where(cnt <= need, cand, end)
        return lax.fori_loop(0, pos_bits, step, jnp.zeros((1, tq), I32))

    tie_end = lax.cond(jnp.max(n_eq - need) > 0.0, tie_search, lambda: jnp.full((1, tq), pos_all, I32))

    qs_ref[...] = qt_ref[...] * jnp.asarray(ATTN_SCALE, BF16)
    m_ref[...] = jnp.full(m_ref.shape, NEG_INF, F32)
    l_ref[...] = jnp.zeros(l_ref.shape, F32)
    acc_ref[...] = jnp.zeros(acc_ref.shape, F32)

    def bias_block(j, carry):
        key = key_ref[j]
        pos = key_pos(j)
        keep = (pos < q_limit) & ((key > thr) | ((key == thr) & (pos < tie_end)))
        key_ref[j] = lax.bitcast_convert_type(jnp.where(keep, 0.0, NEG_INF), I32)
        return carry

    lax.fori_loop(0, n_kb, bias_block, 0)

    ks = min(V7X_LANES, kb)

    def issue_scores(j, r0, buf):
        k_rows = pl.ds(pl.multiple_of(j * kb + r0, ks), ks)
        for h in range(N_HEADS):
            buf[h] = jnp.dot(k_ref[h, k_rows, :], qs_ref[h * HEAD_DIM:(h + 1) * HEAD_DIM, :],
                             preferred_element_type=F32)

    def att_block(j, carry):
        tiles = list(range(0, kb, ks))
        bufs = [s_ref, s2_ref]
        issue_scores(j, tiles[0], bufs[0])
        for i, r0 in enumerate(tiles):
            if i + 1 < len(tiles):
                issue_scores(j, tiles[i + 1], bufs[(i + 1) % 2])
            bias = lax.bitcast_convert_type(key_ref[j, r0:r0 + ks, :], F32)
            for h in range(N_HEADS):
                rows = slice(h * HEAD_DIM, (h + 1) * HEAD_DIM)
                s = bufs[i % 2][h] + bias
                m_old = m_ref[h]
                m_new = jnp.maximum(m_old, jnp.max(s, axis=0, keepdims=True))
                alpha = jnp.exp(m_old - m_new)
                p = jnp.exp(s.reshape(ks // 8, 8, tq) - m_new[None])
                l_ref[h] = alpha * l_ref[h] + jnp.sum(p, axis=0)
                pv = jnp.dot(vt_ref[j, rows, r0:r0 + ks], p.reshape(ks, tq).astype(BF16),
                             preferred_element_type=F32)
                acc = acc_ref[rows, :].reshape(HEAD_DIM // 8, 8, tq) * alpha[None]
                acc_ref[rows, :] = acc.reshape(HEAD_DIM, tq) + pv
                m_ref[h] = m_new
        return carry

    lax.fori_loop(0, n_kb, att_block, 0)
    for h in range(N_HEADS):
        rows = slice(h * HEAD_DIM, (h + 1) * HEAD_DIM)
        l_tot = jnp.sum(l_ref[h], axis=0, keepdims=True)
        o_ref[rows, :] = (acc_ref[rows, :] / l_tot).astype(o_ref.dtype)


def _dsa(q, iq, iw, k, v, ik, *, n_keep, causal, n_valid, kb):
    b, t, _ = q.shape
    s_pad = k.shape[1]
    n_kb_total = s_pad // kb
    t_pad = -(-t // V7X_LANES) * V7X_LANES
    tq = min(256, t_pad)
    assert t_pad % tq == 0 and (not causal or (t_pad == t and tq % CHUNK == 0))
    to_lanes = lambda a: jnp.pad(a.transpose(0, 2, 1), ((0, 0), (0, 0), (0, t_pad - t)))
    k_hm = k.astype(BF16).reshape(b, s_pad, N_HEADS, HEAD_DIM).transpose(0, 2, 1, 3)
    vt_blk = v.astype(BF16).reshape(b, n_kb_total, kb, D_ATTN).transpose(0, 1, 3, 2)
    body = functools.partial(_dsa_body, tq=tq, kb=kb, n_keep=float(n_keep), causal=causal, n_valid=n_valid,
                             n_kb_total=n_kb_total, pos_bits=s_pad.bit_length())
    qry = lambda n: pl.BlockSpec((None, n, tq), lambda bi, i: (bi, 0, i))
    out_t = pl.pallas_call(
        body,
        out_shape=jax.ShapeDtypeStruct((b, D_ATTN, t_pad), BF16),
        grid=(b, t_pad // tq),
        in_specs=[qry(D_ATTN), qry(D_IDX), qry(N_IDX_HEADS),
                  pl.BlockSpec((None, N_HEADS, s_pad, HEAD_DIM), lambda bi, i: (bi, 0, 0, 0)),
                  pl.BlockSpec((None, n_kb_total, D_ATTN, kb), lambda bi, i: (bi, 0, 0, 0)),
                  pl.BlockSpec((None, s_pad, IDX_DIM), lambda bi, i: (bi, 0, 0))],
        out_specs=qry(D_ATTN),
        scratch_shapes=[pltpu.VMEM((n_kb_total, kb, tq), I32),
                        pltpu.VMEM((D_ATTN, tq), BF16),
                        pltpu.VMEM((D_ATTN, tq), F32),
                        pltpu.VMEM((N_HEADS, 8, tq), F32),
                        pltpu.VMEM((N_HEADS, 8, tq), F32),
                        pltpu.VMEM((N_HEADS, min(V7X_LANES, kb), tq), F32),
                        pltpu.VMEM((N_HEADS, min(V7X_LANES, kb), tq), F32)],
        compiler_params=_params("parallel", "arbitrary"),
        name="dsa",
    )(to_lanes(q), to_lanes(iq), to_lanes(iw), k_hm, vt_blk, ik.astype(BF16))
    return out_t[:, :, :t].transpose(0, 2, 1)


def _ssm_discretize_body(a_re_ref, a_im_ref, log_dt_ref, ab_re_ref, ab_im_ref, bc_re_ref, bc_im_ref):
    a_re, a_im = a_re_ref[...], a_im_ref[...]
    dt = jnp.exp(log_dt_ref[...])
    mag = jnp.exp(dt * a_re)
    ab_re = mag * jnp.cos(dt * a_im)
    ab_im = mag * jnp.sin(dt * a_im)
    den = a_re * a_re + a_im * a_im
    nr = ab_re - 1.0
    ni = ab_im
    ab_re_ref[...] = ab_re
    ab_im_ref[...] = ab_im
    bc_re_ref[...] = (nr * a_re + ni * a_im) / den
    bc_im_ref[...] = (ni * a_re - nr * a_im) / den


def _ssm_discretize(a_re, a_im, log_dt):
    g, p = a_re.shape
    out = jax.ShapeDtypeStruct((g, p), F32)
    return pl.pallas_call(_ssm_discretize_body, out_shape=(out,) * 4, name="ssm_discretize")(
        a_re, a_im, log_dt.reshape(g, 1))


def _block_diag_halves(w, dtype):
    g, r, c = w.shape
    gh = g // 2
    eye = jnp.eye(gh, dtype=w.dtype)
    halves = [jnp.einsum("grc,gh->grhc", w[i * gh:(i + 1) * gh], eye).reshape(gh * r, gh * c) for i in range(2)]
    return jnp.stack(halves, 0).astype(dtype)


def _ssm_body(u_ref, h0r_ref, h0i_ref, abr_ref, abi_ref, bcr_ref, bci_ref, br_ref, bi_ref, cr_ref, ci_ref,
              d_ref, gw_ref, gb_ref, y_ref, hr_ref, hi_ref, xr_ref, xi_ref, *, tb, nb, precise):
    d_ssm = u_ref.shape[-1]
    n_state = hr_ref.shape[-1]
    kh, sh = d_ssm // 2, n_state // 2

    def mm(a, w):
        if precise:
            return jnp.dot(a, w, preferred_element_type=F32, precision=lax.Precision.HIGHEST)
        return jnp.dot(a.astype(BF16), w, preferred_element_type=F32)

    @pl.when(pl.program_id(0) == 0)
    def _():
        hr_ref[...] = h0r_ref[...]
        hi_ref[...] = h0i_ref[...]

    u = u_ref[...].reshape(tb * nb, d_ssm)

    for half in range(2):
        cols = slice(half * sh, (half + 1) * sh)
        uh = u[:, half * kh:(half + 1) * kh]
        bu_r = mm(uh, br_ref[half])
        bu_i = mm(uh, bi_ref[half])
        bc_r, bc_i = bcr_ref[:, cols], bci_ref[:, cols]
        xr_ref[:, cols] = bc_r * bu_r - bc_i * bu_i
        xi_ref[:, cols] = bc_r * bu_i + bc_i * bu_r

    slab = min(512, n_state)
    for s0 in range(0, n_state, slab):
        cols = slice(s0, s0 + slab)
        a_r = jnp.broadcast_to(abr_ref[:, cols], (nb, slab))
        a_i = jnp.broadcast_to(abi_ref[:, cols], (nb, slab))

        def step(t, h):
            h_r, h_i = h
            rows = pl.ds(pl.multiple_of(t * nb, nb), nb)
            n_r = a_r * h_r - a_i * h_i + xr_ref[rows, cols]
            n_i = a_r * h_i + a_i * h_r + xi_ref[rows, cols]
            xr_ref[rows, cols] = n_r
            xi_ref[rows, cols] = n_i
            return n_r, n_i

        h_r, h_i = lax.fori_loop(0, tb, step, (hr_ref[:, cols], hi_ref[:, cols]), unroll=min(8, tb))
        hr_ref[:, cols] = h_r
        hi_ref[:, cols] = h_i

    ys = []
    for half in range(2):
        cols = slice(half * sh, (half + 1) * sh)
        ys.append(mm(xr_ref[:, cols], cr_ref[half]) - mm(xi_ref[:, cols], ci_ref[half]))
    y = jnp.concatenate(ys, axis=1) + d_ref[...] * u
    y = y * (0.5 * (1.0 + jnp.tanh(math.sqrt(2.0 / math.pi) * (y + 0.044715 * (y * y * y)))))
    z = mm(y, gw_ref[...]) + gb_ref[...]
    y_ref[...] = (y * _sigmoid(z)).reshape(tb, nb, d_ssm)


def _ssm(u_tm, h0_re, h0_im, disc, b_re, b_im, c_re, c_im, d, glu_w, glu_b, *, precise):
    t, nb, d_ssm = u_tm.shape
    n_state = h0_re.shape[1]
    assert nb % 8 == 0
    tb = min(32, t)
    wdt = F32 if precise else BF16
    br = _block_diag_halves(jnp.swapaxes(b_re, 1, 2), wdt)
    bi = _block_diag_halves(jnp.swapaxes(b_im, 1, 2), wdt)
    cr = _block_diag_halves(jnp.swapaxes(c_re, 1, 2), wdt)
    ci = _block_diag_halves(jnp.swapaxes(c_im, 1, 2), wdt)
    row = lambda a: a.reshape(1, -1).astype(F32)
    state_spec = _const_spec((nb, n_state))
    vec_state = _const_spec((1, n_state))
    vec_ch = _const_spec((1, d_ssm))
    return pl.pallas_call(
        functools.partial(_ssm_body, tb=tb, nb=nb, precise=precise),
        out_shape=(jax.ShapeDtypeStruct((t, nb, d_ssm), F32),
                   jax.ShapeDtypeStruct((nb, n_state), F32), jax.ShapeDtypeStruct((nb, n_state), F32)),
        grid=(t // tb,),
        in_specs=[pl.BlockSpec((tb, nb, d_ssm), lambda i: (i, 0, 0)), state_spec, state_spec,
                  vec_state, vec_state, vec_state, vec_state,
                  _const_spec(br.shape), _const_spec(bi.shape), _const_spec(cr.shape), _const_spec(ci.shape),
                  vec_ch, _const_spec((d_ssm, d_ssm)), vec_ch],
        out_specs=(pl.BlockSpec((tb, nb, d_ssm), lambda i: (i, 0, 0)), state_spec, state_spec),
        scratch_shapes=[pltpu.VMEM((tb * nb, n_state), F32), pltpu.VMEM((tb * nb, n_state), F32)],
        compiler_params=_params("arbitrary"),
        name="ssm",
    )(u_tm, h0_re, h0_im, row(disc[0]), row(disc[1]), row(disc[2]), row(disc[3]), br, bi, cr, ci,
      row(d), glu_w.astype(wdt), row(glu_b))


def _merge_body(att_ref, ssm_ref, ga_ref, gb_ref, x_ref, wpa_ref, wpb_ref, wo_ref, o_ref):
    a = jnp.dot(att_ref[...], wpa_ref[...], preferred_element_type=F32)
    s = jnp.dot(ssm_ref[...].astype(BF16), wpb_ref[...], preferred_element_type=F32)
    mix = _sigmoid(ga_ref[...]) * a + _sigmoid(gb_ref[...]) * s
    o_ref[...] = x_ref[...] + jnp.dot(mix.astype(BF16), wo_ref[...], preferred_element_type=F32)


def _merge(att, ssm, ga, gb, x, w_pa, w_pb, w_o):
    b, t, d = x.shape
    tm = min(512, t)
    tok = lambda n: pl.BlockSpec((None, tm, n), lambda bi, i: (bi, i, 0))
    d_ssm = ssm.shape[-1]
    return pl.pallas_call(
        _merge_body,
        out_shape=jax.ShapeDtypeStruct((b, t, d), F32),
        grid=(b, t // tm),
        in_specs=[tok(D_ATTN), tok(d_ssm), tok(d), tok(d), tok(d),
                  _const_spec(w_pa.shape), _const_spec(w_pb.shape), _const_spec(w_o.shape)],
        out_specs=tok(d),
        compiler_params=_params("parallel", "parallel"),
        name="merge",
    )(att, ssm, ga, gb, x, w_pa.astype(BF16), w_pb.astype(BF16), w_o.astype(BF16))


def _ffn_body(x_ref, g_ref, wg_ref, wu_ref, wd_ref, o_ref, xn_ref, *, n_f):
    f = pl.program_id(1)

    @pl.when(f == 0)
    def _():
        xn_ref[...] = _rms(x_ref[...], g_ref[...]).astype(BF16)
        o_ref[...] = x_ref[...]

    xn = xn_ref[...]
    hg = jnp.dot(xn, wg_ref[...], preferred_element_type=F32)
    hu = jnp.dot(xn, wu_ref[...], preferred_element_type=F32)
    h = (hg * _sigmoid(hg)) * hu
    o_ref[...] += jnp.dot(h.astype(BF16), wd_ref[...], preferred_element_type=F32)


def _ffn(x2d, g, wg, wu, wd):
    n, d = x2d.shape
    d_ff = wg.shape[1]
    tm = min(512, n)
    fc = d_ff
    for cand in (1408, 1024, 896, 768, 512):
        if d_ff % cand == 0:
            fc = cand
            break
    n_f = d_ff // fc
    return pl.pallas_call(
        functools.partial(_ffn_body, n_f=n_f),
        out_shape=jax.ShapeDtypeStruct((n, d), F32),
        grid=(n // tm, n_f),
        in_specs=[pl.BlockSpec((tm, d), lambda i, f: (i, 0)), _const_spec((1, d)),
                  pl.BlockSpec((d, fc), lambda i, f: (0, f)), pl.BlockSpec((d, fc), lambda i, f: (0, f)),
                  pl.BlockSpec((fc, d), lambda i, f: (f, 0))],
        out_specs=pl.BlockSpec((tm, d), lambda i, f: (i, 0)),
        scratch_shapes=[pltpu.VMEM((tm, d), BF16)],
        compiler_params=_params("parallel", "arbitrary"),
        name="ffn",
    )(x2d, g.reshape(1, d), wg.astype(BF16), wu.astype(BF16), wd.astype(BF16))


def _moe_body(x_ref, g_ref, r_ref, wg_ref, wu_ref, wd_ref, o_ref, xn_ref, gate_ref, *, n_exp):
    e, f = pl.program_id(1), pl.program_id(2)

    @pl.when((e == 0) & (f == 0))
    def _():
        xn = _rms(x_ref[...], g_ref[...])
        xn_ref[...] = xn.astype(BF16)
        o_ref[...] = x_ref[...]
        logits = jnp.dot(xn, r_ref[...], preferred_element_type=F32, precision=lax.Precision.HIGHEST)
        lane = lax.broadcasted_iota(I32, logits.shape, 1).astype(F32)
        logits = jnp.where(lane < n_exp, logits, NEG_INF)
        big = float(logits.shape[1])
        m1 = jnp.max(logits, axis=1, keepdims=True)
        i1 = jnp.min(jnp.where(logits == m1, lane, big), axis=1, keepdims=True)
        rest = jnp.where(lane == i1, NEG_INF, logits)
        m2 = jnp.max(rest, axis=1, keepdims=True)
        i2 = jnp.min(jnp.where(rest == m2, lane, big), axis=1, keepdims=True)
        e2 = jnp.exp(m2 - m1)
        w1 = 1.0 / (1.0 + e2)
        w2 = e2 / (1.0 + e2)
        gate_ref[...] = jnp.where(lane == i1, w1, 0.0) + jnp.where(lane == i2, w2, 0.0)

    gates = gate_ref[...]
    lane = lax.broadcasted_iota(I32, gates.shape, 1)
    gate = jnp.sum(jnp.where(lane == e, gates, 0.0), axis=1, keepdims=True)
    xn = xn_ref[...]
    hg = jnp.dot(xn, wg_ref[...], preferred_element_type=F32)
    hu = jnp.dot(xn, wu_ref[...], preferred_element_type=F32)
    h = (hg * _sigmoid(hg)) * hu
    o_ref[...] += gate * jnp.dot(h.astype(BF16), wd_ref[...], preferred_element_type=F32)


def _moe(x2d, g, router, wg, wu, wd):
    n, d = x2d.shape
    n_exp, _, d_ff = wg.shape
    assert TOP_K_EXPERTS == 2 and n_exp <= V7X_LANES
    tm = min(512, n)
    fc = d_ff
    for cand in (896, 1024, 768, 512):
        if d_ff % cand == 0:
            fc = cand
            break
    n_f = d_ff // fc
    r_pad = jnp.concatenate([router, jnp.zeros((d, V7X_LANES - n_exp), router.dtype)], axis=1)
    return pl.pallas_call(
        functools.partial(_moe_body, n_exp=n_exp),
        out_shape=jax.ShapeDtypeStruct((n, d), F32),
        grid=(n // tm, n_exp, n_f),
        in_specs=[pl.BlockSpec((tm, d), lambda i, e, f: (i, 0)), _const_spec((1, d)), _const_spec((d, V7X_LANES)),
                  pl.BlockSpec((None, d, fc), lambda i, e, f: (e, 0, f)),
                  pl.BlockSpec((None, d, fc), lambda i, e, f: (e, 0, f)),
                  pl.BlockSpec((None, fc, d), lambda i, e, f: (e, f, 0))],
        out_specs=pl.BlockSpec((tm, d), lambda i, e, f: (i, 0)),
        scratch_shapes=[pltpu.VMEM((tm, d), BF16), pltpu.VMEM((tm, V7X_LANES), F32)],
        compiler_params=_params("parallel", "arbitrary", "arbitrary"),
        name="moe",
    )(x2d, g.reshape(1, d), r_pad, wg.astype(BF16), wu.astype(BF16), wd.astype(BF16))


def _final_norm_body(x_ref, g_ref, o_ref):
    o_ref[...] = _rms(x_ref[...], g_ref[...])


def _final_norm(x2d, g):
    n, d = x2d.shape
    tm = min(1024, n)
    return pl.pallas_call(
        _final_norm_body,
        out_shape=jax.ShapeDtypeStruct((n, d), F32),
        grid=(n // tm,),
        in_specs=[pl.BlockSpec((tm, d), lambda i: (i, 0)), _const_spec((1, d))],
        out_specs=pl.BlockSpec((tm, d), lambda i: (i, 0)),
        compiler_params=_params("parallel"),
        name="final_norm",
    )(x2d, g.reshape(1, d))


def _layer(x, past, h0, lw, *, layer_idx, precise_ssm):
    b, t, d = x.shape
    d_ssm = lw["ssm_d"].shape[0]
    q, k, v, iq, ik, iw, u, ga, gb = _inproj(x, lw["norm_mix_g"], lw["w_in_packed"], d_ssm)

    if past is None:
        kb = min(512, t)
        k_all, v_all, ik_all, s_valid = k, v, ik, t
    else:
        ck, cv, cik = past
        k_all = jnp.concatenate([ck.reshape(b, -1, D_ATTN), k], axis=1)
        v_all = jnp.concatenate([cv.reshape(b, -1, D_ATTN), v], axis=1)
        ik_all = jnp.concatenate([cik, ik], axis=1)
        s_valid = k_all.shape[1]
        kb = 512
        pad = (-s_valid) % kb
        padf = lambda a: jnp.pad(a, ((0, 0), (0, pad), (0, 0)))
        k_all, v_all, ik_all = padf(k_all), padf(v_all), padf(ik_all)
    att = _dsa(q, iq, iw, k_all, v_all, ik_all, n_keep=min(TOPK_MAX, s_valid // 4), causal=past is None,
               n_valid=s_valid, kb=kb)

    ssm_tm, h_re, h_im = _ssm(u.transpose(1, 0, 2), h0[0], h0[1], lw["disc"], lw["ssm_b_re"], lw["ssm_b_im"], lw["ssm_c_re"],
                              lw["ssm_c_im"], lw["ssm_d"], lw["glu_w"], lw["glu_b"], precise=precise_ssm)
    x = _merge(att, ssm_tm.transpose(1, 0, 2), ga, gb, x, lw["w_branch_attn"], lw["w_branch_ssm"], lw["w_out"])
    x2d = x.reshape(b * t, d)
    if layer_idx % 2 == 0:
        x2d = _ffn(x2d, lw["norm_ffn_g"], lw["ffn_w_gate"], lw["ffn_w_up"], lw["ffn_w_down"])
    else:
        x2d = _moe(x2d, lw["norm_ffn_g"], lw["moe_router"], lw["moe_w_gate"], lw["moe_w_up"], lw["moe_w_down"])
    g_shape = (b, -1, SSM_STATE)
    return (x2d.reshape(b, t, d), k.reshape(b, t, N_HEADS, HEAD_DIM), v.reshape(b, t, N_HEADS, HEAD_DIM), ik,
            h_re.reshape(g_shape), h_im.reshape(g_shape))


def kernel(x_prompt, x_sample, cache_k, cache_v, cache_idx_k, state_ssm_re, state_ssm_im, norm_mix_g, w_in,
           ssm_a_re, ssm_a_im, ssm_log_dt, ssm_b_re, ssm_b_im, ssm_c_re, ssm_c_im, ssm_d, glu_w, glu_b,
           w_branch_attn, w_branch_ssm, w_out, norm_ffn_g, ffn_w_gate, ffn_w_up, ffn_w_down, moe_router,
           moe_w_gate, moe_w_up, moe_w_down, final_norm_g):
    depth = w_in.shape[0]
    d_model = x_prompt.shape[-1]
    d_ssm = ssm_d.shape[1]
    xp, xs = x_prompt, x_sample
    outs_p, outs_s = [], []
    for l in range(depth):
        lw = dict(
            norm_mix_g=norm_mix_g[l], w_in_packed=_pack_w_in(w_in[l], d_ssm, d_model),
            disc=_ssm_discretize(ssm_a_re[l], ssm_a_im[l], ssm_log_dt[l]),
            ssm_b_re=ssm_b_re[l], ssm_b_im=ssm_b_im[l], ssm_c_re=ssm_c_re[l], ssm_c_im=ssm_c_im[l],
            ssm_d=ssm_d[l], glu_w=glu_w[l], glu_b=glu_b[l],
            w_branch_attn=w_branch_attn[l], w_branch_ssm=w_branch_ssm[l], w_out=w_out[l], norm_ffn_g=norm_ffn_g[l])
        i = l // 2
        if l % 2 == 0:
            lw.update(ffn_w_gate=ffn_w_gate[i], ffn_w_up=ffn_w_up[i], ffn_w_down=ffn_w_down[i])
        else:
            lw.update(moe_router=moe_router[i], moe_w_gate=moe_w_gate[i], moe_w_up=moe_w_up[i],
                      moe_w_down=moe_w_down[i])
        n_state = ssm_a_re.shape[1] * ssm_a_re.shape[2]
        zeros = jnp.zeros((xp.shape[0], n_state), F32)
        xp, *rest_p = _layer(xp, None, (zeros, zeros), lw, layer_idx=l, precise_ssm=False)
        outs_p.append(rest_p)
        h0 = (state_ssm_re[l].reshape(xs.shape[0], n_state), state_ssm_im[l].reshape(xs.shape[0], n_state))
        xs, *rest_s = _layer(xs, (cache_k[l], cache_v[l], cache_idx_k[l]), h0, lw, layer_idx=l, precise_ssm=True)
        outs_s.append(rest_s)

    y_prompt = _final_norm(xp.reshape(-1, d_model), final_norm_g).reshape(xp.shape)
    y_sample = _final_norm(xs.reshape(-1, d_model), final_norm_g).reshape(xs.shape)
    stack = lambda outs, j: jnp.stack([o[j] for o in outs], 0)
    return (y_prompt, y_sample,
            stack(outs_p, 0), stack(outs_p, 1), stack(outs_p, 2), stack(outs_p, 3), stack(outs_p, 4),
            stack(outs_s, 0), stack(outs_s, 1), stack(outs_s, 2), stack(outs_s, 3), stack(outs_s, 4))
```

```python
import functools
import math

import jax
import jax.numpy as jnp
from jax import lax
from jax.experimental import pallas as pl
from jax.experimental.pallas import tpu as pltpu

F32 = jnp.float32
BF16 = jnp.bfloat16
I32 = jnp.int32

CHUNK = 64
N_HEADS = 8
HEAD_DIM = 64
N_IDX_HEADS = 8
IDX_DIM = 64
TOPK_MAX = 256
SSM_GROUP = 16
SSM_STATE = 64
TOP_K_EXPERTS = 2
RMS_EPS = 1e-6
NEG_INF = -1e30
INT32_MIN = -(2 ** 31)

D_ATTN = N_HEADS * HEAD_DIM
D_IDX = N_IDX_HEADS * IDX_DIM
IDX_W_SCALE = float(D_IDX) ** -0.5
ATTN_SCALE = float(HEAD_DIM) ** -0.5

V7X_LANES = 128
V7X_VMEM_LIMIT_BYTES = 56 * 1024 * 1024


def _params(*semantics):
    return pltpu.CompilerParams(dimension_semantics=semantics, vmem_limit_bytes=V7X_VMEM_LIMIT_BYTES)


def _sigmoid(x):
    return 1.0 / (1.0 + jnp.exp(-x))


def _rms(x, g):
    return (x * lax.rsqrt(jnp.mean(x * x, axis=-1, keepdims=True) + RMS_EPS)) * g


def _const_spec(shape):
    return pl.BlockSpec(shape, lambda *_: (0,) * len(shape))


_SEG_Q, _SEG_K, _SEG_V, _SEG_IQ = 0, D_ATTN, 2 * D_ATTN, 3 * D_ATTN
_SEG_SMALL = 3 * D_ATTN + D_IDX


def _pack_w_in(w_in, d_ssm, d_model):
    offs = [0]
    for s in (D_ATTN, D_ATTN, D_ATTN, D_IDX, IDX_DIM, N_IDX_HEADS, d_ssm, d_model, d_model):
        offs.append(offs[-1] + s)
    pad = V7X_LANES - IDX_DIM - N_IDX_HEADS
    small = jnp.concatenate([w_in[:, offs[4]:offs[6]], jnp.zeros((w_in.shape[0], pad), w_in.dtype)], axis=1)
    return jnp.concatenate([w_in[:, :offs[4]], small, w_in[:, offs[6]:]], axis=1).astype(BF16)


def _inproj_body(x_ref, g_ref, w_ref, q_ref, k_ref, v_ref, iq_ref, ik_ref, iw_ref, u_ref, ga_ref, gb_ref, *, d_ssm, d_model):
    hb = _rms(x_ref[...], g_ref[...]).astype(BF16)

    def proj(lo, n):
        return jnp.dot(hb, w_ref[:, lo:lo + n], preferred_element_type=F32)

    q_ref[...] = proj(_SEG_Q, D_ATTN).astype(BF16)
    k_ref[...] = proj(_SEG_K, D_ATTN)
    v_ref[...] = proj(_SEG_V, D_ATTN)
    iq_ref[...] = proj(_SEG_IQ, D_IDX).astype(BF16)
    small = proj(_SEG_SMALL, V7X_LANES)
    ik_ref[...] = small[:, :IDX_DIM]
    iw_ref[...] = small[:, IDX_DIM:IDX_DIM + N_IDX_HEADS] * IDX_W_SCALE
    seg_u = _SEG_SMALL + V7X_LANES
    u_ref[...] = proj(seg_u, d_ssm)
    ga_ref[...] = proj(seg_u + d_ssm, d_model)
    gb_ref[...] = proj(seg_u + d_ssm + d_model, d_model)


def _inproj(x, g, w_packed, d_ssm):
    b, t, d = x.shape
    tm = min(256, t)
    n_w = w_packed.shape[1]

    def tok(n, dtype):
        return jax.ShapeDtypeStruct((b, t, n), dtype), pl.BlockSpec((None, tm, n), lambda bi, i: (bi, i, 0))

    shapes, specs = zip(
        tok(D_ATTN, BF16), tok(D_ATTN, F32), tok(D_ATTN, F32), tok(D_IDX, BF16), tok(IDX_DIM, F32),
        tok(N_IDX_HEADS, F32), tok(d_ssm, F32), tok(d, F32), tok(d, F32))
    return pl.pallas_call(
        functools.partial(_inproj_body, d_ssm=d_ssm, d_model=d),
        out_shape=shapes,
        grid=(b, t // tm),
        in_specs=[pl.BlockSpec((None, tm, d), lambda bi, i: (bi, i, 0)), _const_spec((1, d)), _const_spec((d, n_w))],
        out_specs=specs,
        compiler_params=_params("parallel", "parallel"),
        name="inproj",
    )(x, g.reshape(1, d), w_packed)


def _dsa_body(qt_ref, iqt_ref, iwt_ref, k_ref, vt_ref, ik_ref, o_ref, key_ref, qs_ref, acc_ref, m_ref, l_ref, s_ref, s2_ref, *,
              tq, kb, n_keep, causal, n_valid, n_kb_total, pos_bits):
    col0 = pl.program_id(1) * tq
    if causal:
        n_kb = (col0 + tq + kb - 1) // kb
        q_pos = col0 + lax.broadcasted_iota(I32, (1, tq), 1)
        shift = CHUNK.bit_length() - 1
        q_limit = ((q_pos >> shift) + 1) << shift
    else:
        n_kb = n_kb_total
        q_limit = jnp.full((1, tq), n_valid, I32)

    def key_pos(j):
        return j * kb + lax.broadcasted_iota(I32, (kb, tq), 0)

    def key_rows(j):
        return pl.ds(pl.multiple_of(j * kb, kb), kb)

    fold_rows = min(64, kb)

    def fold(x, op):
        return op(x.reshape(kb // fold_rows, fold_rows, tq), axis=0)

    iwt = iwt_ref[...]

    def idx_block(j, carry):
        ik_blk = ik_ref[key_rows(j), :]
        s = None
        for h in range(N_IDX_HEADS):
            d = jnp.dot(ik_blk, iqt_ref[h * IDX_DIM:(h + 1) * IDX_DIM, :], preferred_element_type=F32)
            term = iwt[h:h + 1, :] * jnp.maximum(d, 0.0)
            s = term if s is None else s + term
        s = jnp.where(s == 0.0, 0.0, s)
        s = jnp.where(key_pos(j) < q_limit, s, NEG_INF)
        bits = lax.bitcast_convert_type(s, I32)
        key_ref[j] = jnp.where(bits < 0, bits ^ jnp.int32(0x7FFFFFFF), bits)
        return carry

    lax.fori_loop(0, n_kb, idx_block, 0)

    def count(pred):
        def body(j, acc):
            return acc + fold(jnp.where(pred(key_ref[j], j), 1.0, 0.0), jnp.sum)
        acc = lax.fori_loop(0, n_kb, body, jnp.zeros((fold_rows, tq), F32))
        return jnp.sum(acc, axis=0, keepdims=True)

    def bit_step(i, thr):
        cand = thr + lax.shift_left(jnp.int32(1), 31 - i)
        cnt = count(lambda key, j: key >= cand)
        return jnp.where(cnt >= n_keep, cand, thr)

    thr = lax.fori_loop(0, 32, bit_step, jnp.full((1, tq), INT32_MIN, I32))

    need = n_keep - count(lambda key, j: key > thr)
    n_eq = count(lambda key, j: key == thr)
    pos_all = jnp.int32(2 ** pos_bits - 1)

    def tie_search():
        def step(i, end):
            cand = end + lax.shift_left(jnp.int32(1), pos_bits - 1 - i)
            cnt = count(lambda key, j: (key == thr) & (key_pos(j) < cand))
            return jnp.where(cnt <= need, cand, end)
        return lax.fori_loop(0, pos_bits, step, jnp.zeros((1, tq), I32))

    tie_end = lax.cond(jnp.max(n_eq - need) > 0.0, tie_search, lambda: jnp.full((1, tq), pos_all, I32))

    qs_ref[...] = qt_ref[...] * jnp.asarray(ATTN_SCALE, BF16)
    m_ref[...] = jnp.full(m_ref.shape, NEG_INF, F32)
    l_ref[...] = jnp.zeros(l_ref.shape, F32)
    acc_ref[...] = jnp.zeros(acc_ref.shape, F32)

    def bias_block(j, carry):
        key = key_ref[j]
        pos = key_pos(j)
        keep = (pos < q_limit) & ((key > thr) | ((key == thr) & (pos < tie_end)))
        key_ref[j] = lax.bitcast_convert_type(jnp.where(keep, 0.0, NEG_INF), I32)
        return carry

    lax.fori_loop(0, n_kb, bias_block, 0)

    ks = min(V7X_LANES, kb)

    def issue_scores(j, r0, buf):
        k_rows = pl.ds(pl.multiple_of(j * kb + r0, ks), ks)
        for h in range(N_HEADS):
            buf[h] = jnp.dot(k_ref[h, k_rows, :], qs_ref[h * HEAD_DIM:(h + 1) * HEAD_DIM, :],
                             preferred_element_type=F32)

    def att_block(j, carry):
        tiles = list(range(0, kb, ks))
        bufs = [s_ref, s2_ref]
        issue_scores(j, tiles[0], bufs[0])
        for i, r0 in enumerate(tiles):
            if i + 1 < len(tiles):
                issue_scores(j, tiles[i + 1], bufs[(i + 1) % 2])
            bias = lax.bitcast_convert_type(key_ref[j, r0:r0 + ks, :], F32)
            for h in range(N_HEADS):
                rows = slice(h * HEAD_DIM, (h + 1) * HEAD_DIM)
                s = bufs[i % 2][h] + bias
                m_old = m_ref[h]
                m_new = jnp.maximum(m_old, jnp.max(s, axis=0, keepdims=True))
                alpha = jnp.exp(m_old - m_new)
                p = jnp.exp(s.reshape(ks // 8, 8, tq) - m_new[None])
                l_ref[h] = alpha * l_ref[h] + jnp.sum(p, axis=0)
                pv = jnp.dot(vt_ref[j, rows, r0:r0 + ks], p.reshape(ks, tq).astype(BF16),
                             preferred_element_type=F32)
                acc = acc_ref[rows, :].reshape(HEAD_DIM // 8, 8, tq) * alpha[None]
                acc_ref[rows, :] = acc.reshape(HEAD_DIM, tq) + pv
                m_ref[h] = m_new
        return carry

    lax.fori_loop(0, n_kb, att_block, 0)
    for h in range(N_HEADS):
        rows = slice(h * HEAD_DIM, (h + 1) * HEAD_DIM)
        l_tot = jnp.sum(l_ref[h], axis=0, keepdims=True)
        o_ref[rows, :] = (acc_ref[rows, :] / l_tot).astype(o_ref.dtype)


def _dsa(q, iq, iw, k, v, ik, *, n_keep, causal, n_valid, kb):
    b, t, _ = q.shape
    s_pad = k.shape[1]
    n_kb_total = s_pad // kb
    t_pad = -(-t // V7X_LANES) * V7X_LANES
    tq = min(256, t_pad)
    assert t_pad % tq == 0 and (not causal or (t_pad == t and tq % CHUNK == 0))
    to_lanes = lambda a: jnp.pad(a.transpose(0, 2, 1), ((0, 0), (0, 0), (0, t_pad - t)))
    k_hm = k.astype(BF16).reshape(b, s_pad, N_HEADS, HEAD_DIM).transpose(0, 2, 1, 3)
    vt_blk = v.astype(BF16).reshape(b, n_kb_total, kb, D_ATTN).transpose(0, 1, 3, 2)
    body = functools.partial(_dsa_body, tq=tq, kb=kb, n_keep=float(n_keep), causal=causal, n_valid=n_valid,
                             n_kb_total=n_kb_total, pos_bits=s_pad.bit_length())
    qry = lambda n: pl.BlockSpec((None, n, tq), lambda bi, i: (bi, 0, i))
    out_t = pl.pallas_call(
        body,
        out_shape=jax.ShapeDtypeStruct((b, D_ATTN, t_pad), BF16),
        grid=(b, t_pad // tq),
        in_specs=[qry(D_ATTN), qry(D_IDX), qry(N_IDX_HEADS),
                  pl.BlockSpec((None, N_HEADS, s_pad, HEAD_DIM), lambda bi, i: (bi, 0, 0, 0)),
                  pl.BlockSpec((None, n_kb_total, D_ATTN, kb), lambda bi, i: (bi, 0, 0, 0)),
                  pl.BlockSpec((None, s_pad, IDX_DIM), lambda bi, i: (bi, 0, 0))],
        out_specs=qry(D_ATTN),
        scratch_shapes=[pltpu.VMEM((n_kb_total, kb, tq), I32),
                        pltpu.VMEM((D_ATTN, tq), BF16),
                        pltpu.VMEM((D_ATTN, tq), F32),
                        pltpu.VMEM((N_HEADS, 8, tq), F32),
                        pltpu.VMEM((N_HEADS, 8, tq), F32),
                        pltpu.VMEM((N_HEADS, min(V7X_LANES, kb), tq), F32),
                        pltpu.VMEM((N_HEADS, min(V7X_LANES, kb), tq), F32)],
        compiler_params=_params("parallel", "arbitrary"),
        name="dsa",
    )(to_lanes(q), to_lanes(iq), to_lanes(iw), k_hm, vt_blk, ik.astype(BF16))
    return out_t[:, :, :t].transpose(0, 2, 1)


def _ssm_discretize_body(a_re_ref, a_im_ref, log_dt_ref, ab_re_ref, ab_im_ref, bc_re_ref, bc_im_ref):
    a_re, a_im = a_re_ref[...], a_im_ref[...]
    dt = jnp.exp(log_dt_ref[...])
    mag = jnp.exp(dt * a_re)
    ab_re = mag * jnp.cos(dt * a_im)
    ab_im = mag * jnp.sin(dt * a_im)
    den = a_re * a_re + a_im * a_im
    nr = ab_re - 1.0
    ni = ab_im
    ab_re_ref[...] = ab_re
    ab_im_ref[...] = ab_im
    bc_re_ref[...] = (nr * a_re + ni * a_im) / den
    bc_im_ref[...] = (ni * a_re - nr * a_im) / den


def _ssm_discretize(a_re, a_im, log_dt):
    g, p = a_re.shape
    out = jax.ShapeDtypeStruct((g, p), F32)
    return pl.pallas_call(_ssm_discretize_body, out_shape=(out,) * 4, name="ssm_discretize")(
        a_re, a_im, log_dt.reshape(g, 1))


def _block_diag_halves(w, dtype):
    g, r, c = w.shape
    gh = g // 2
    eye = jnp.eye(gh, dtype=w.dtype)
    halves = [jnp.einsum("grc,gh->grhc", w[i * gh:(i + 1) * gh], eye).reshape(gh * r, gh * c) for i in range(2)]
    return jnp.stack(halves, 0).astype(dtype)


def _ssm_body(u_ref, h0r_ref, h0i_ref, abr_ref, abi_ref, bcr_ref, bci_ref, br_ref, bi_ref, cr_ref, ci_ref,
              d_ref, gw_ref, gb_ref, y_ref, hr_ref, hi_ref, xr_ref, xi_ref, *, tb, nb, precise):
    d_ssm = u_ref.shape[-1]
    n_state = hr_ref.shape[-1]
    kh, sh = d_ssm // 2, n_state // 2

    def mm(a, w):
        if precise:
            return jnp.dot(a, w, preferred_element_type=F32, precision=lax.Precision.HIGHEST)
        return jnp.dot(a.astype(BF16), w, preferred_element_type=F32)

    @pl.when(pl.program_id(0) == 0)
    def _():
        hr_ref[...] = h0r_ref[...]
        hi_ref[...] = h0i_ref[...]

    u = u_ref[...].reshape(tb * nb, d_ssm)

    for half in range(2):
        cols = slice(half * sh, (half + 1) * sh)
        uh = u[:, half * kh:(half + 1) * kh]
        bu_r = mm(uh, br_ref[half])
        bu_i = mm(uh, bi_ref[half])
        bc_r, bc_i = bcr_ref[:, cols], bci_ref[:, cols]
        xr_ref[:, cols] = bc_r * bu_r - bc_i * bu_i
        xi_ref[:, cols] = bc_r * bu_i + bc_i * bu_r

    slab = min(512, n_state)
    for s0 in range(0, n_state, slab):
        cols = slice(s0, s0 + slab)
        a_r = jnp.broadcast_to(abr_ref[:, cols], (nb, slab))
        a_i = jnp.broadcast_to(abi_ref[:, cols], (nb, slab))

        def step(t, h):
            h_r, h_i = h
            rows = pl.ds(pl.multiple_of(t * nb, nb), nb)
            n_r = a_r * h_r - a_i * h_i + xr_ref[rows, cols]
            n_i = a_r * h_i + a_i * h_r + xi_ref[rows, cols]
            xr_ref[rows, cols] = n_r
            xi_ref[rows, cols] = n_i
            return n_r, n_i

        h_r, h_i = lax.fori_loop(0, tb, step, (hr_ref[:, cols], hi_ref[:, cols]), unroll=min(8, tb))
        hr_ref[:, cols] = h_r
        hi_ref[:, cols] = h_i

    ys = []
    for half in range(2):
        cols = slice(half * sh, (half + 1) * sh)
        ys.append(mm(xr_ref[:, cols], cr_ref[half]) - mm(xi_ref[:, cols], ci_ref[half]))
    y = jnp.concatenate(ys, axis=1) + d_ref[...] * u
    y = y * (0.5 * (1.0 + jnp.tanh(math.sqrt(2.0 / math.pi) * (y + 0.044715 * (y * y * y)))))
    z = mm(y, gw_ref[...]) + gb_ref[...]
    y_ref[...] = (y * _sigmoid(z)).reshape(tb, nb, d_ssm)


def _ssm(u_tm, h0_re, h0_im, disc, b_re, b_im, c_re, c_im, d, glu_w, glu_b, *, precise):
    t, nb, d_ssm = u_tm.shape
    n_state = h0_re.shape[1]
    assert nb % 8 == 0
    tb = min(32, t)
    wdt = F32 if precise else BF16
    br = _block_diag_halves(jnp.swapaxes(b_re, 1, 2), wdt)
    bi = _block_diag_halves(jnp.swapaxes(b_im, 1, 2), wdt)
    cr = _block_diag_halves(jnp.swapaxes(c_re, 1, 2), wdt)
    ci = _block_diag_halves(jnp.swapaxes(c_im, 1, 2), wdt)
    row = lambda a: a.reshape(1, -1).astype(F32)
    state_spec = _const_spec((nb, n_state))
    vec_state = _const_spec((1, n_state))
    vec_ch = _const_spec((1, d_ssm))
    return pl.pallas_call(
        functools.partial(_ssm_body, tb=tb, nb=nb, precise=precise),
        out_shape=(jax.ShapeDtypeStruct((t, nb, d_ssm), F32),
                   jax.ShapeDtypeStruct((nb, n_state), F32), jax.ShapeDtypeStruct((nb, n_state), F32)),
        grid=(t // tb,),
        in_specs=[pl.BlockSpec((tb, nb, d_ssm), lambda i: (i, 0, 0)), state_spec, state_spec,
                  vec_state, vec_state, vec_state, vec_state,
                  _const_spec(br.shape), _const_spec(bi.shape), _const_spec(cr.shape), _const_spec(ci.shape),
                  vec_ch, _const_spec((d_ssm, d_ssm)), vec_ch],
        out_specs=(pl.BlockSpec((tb, nb, d_ssm), lambda i: (i, 0, 0)), state_spec, state_spec),
        scratch_shapes=[pltpu.VMEM((tb * nb, n_state), F32), pltpu.VMEM((tb * nb, n_state), F32)],
        compiler_params=_params("arbitrary"),
        name="ssm",
    )(u_tm, h0_re, h0_im, row(disc[0]), row(disc[1]), row(disc[2]), row(disc[3]), br, bi, cr, ci,
      row(d), glu_w.astype(wdt), row(glu_b))


def _merge_body(att_ref, ssm_ref, ga_ref, gb_ref, x_ref, wpa_ref, wpb_ref, wo_ref, o_ref):
    a = jnp.dot(att_ref[...], wpa_ref[...], preferred_element_type=F32)
    s = jnp.dot(ssm_ref[...].astype(BF16), wpb_ref[...], preferred_element_type=F32)
    mix = _sigmoid(ga_ref[...]) * a + _sigmoid(gb_ref[...]) * s
    o_ref[...] = x_ref[...] + jnp.dot(mix.astype(BF16), wo_ref[...], preferred_element_type=F32)


def _merge(att, ssm, ga, gb, x, w_pa, w_pb, w_o):
    b, t, d = x.shape
    tm = min(512, t)
    tok = lambda n: pl.BlockSpec((None, tm, n), lambda bi, i: (bi, i, 0))
    d_ssm = ssm.shape[-1]
    return pl.pallas_call(
        _merge_body,
        out_shape=jax.ShapeDtypeStruct((b, t, d), F32),
        grid=(b, t // tm),
        in_specs=[tok(D_ATTN), tok(d_ssm), tok(d), tok(d), tok(d),
                  _const_spec(w_pa.shape), _const_spec(w_pb.shape), _const_spec(w_o.shape)],
        out_specs=tok(d),
        compiler_params=_params("parallel", "parallel"),
        name="merge",
    )(att, ssm, ga, gb, x, w_pa.astype(BF16), w_pb.astype(BF16), w_o.astype(BF16))


def _ffn_body(x_ref, g_ref, wg_ref, wu_ref, wd_ref, o_ref, xn_ref, *, n_f):
    f = pl.program_id(1)

    @pl.when(f == 0)
    def _():
        xn_ref[...] = _rms(x_ref[...], g_ref[...]).astype(BF16)
        o_ref[...] = x_ref[...]

    xn = xn_ref[...]
    hg = jnp.dot(xn, wg_ref[...], preferred_element_type=F32)
    hu = jnp.dot(xn, wu_ref[...], preferred_element_type=F32)
    h = (hg * _sigmoid(hg)) * hu
    o_ref[...] += jnp.dot(h.astype(BF16), wd_ref[...], preferred_element_type=F32)


def _ffn(x2d, g, wg, wu, wd):
    n, d = x2d.shape
    d_ff = wg.shape[1]
    tm = min(512, n)
    fc = d_ff
    for cand in (1408, 1024, 896, 768, 512):
        if d_ff % cand == 0:
            fc = cand
            break
    n_f = d_ff // fc
    return pl.pallas_call(
        functools.partial(_ffn_body, n_f=n_f),
        out_shape=jax.ShapeDtypeStruct((n, d), F32),
        grid=(n // tm, n_f),
        in_specs=[pl.BlockSpec((tm, d), lambda i, f: (i, 0)), _const_spec((1, d)),
                  pl.BlockSpec((d, fc), lambda i, f: (0, f)), pl.BlockSpec((d, fc), lambda i, f: (0, f)),
                  pl.BlockSpec((fc, d), lambda i, f: (f, 0))],
        out_specs=pl.BlockSpec((tm, d), lambda i, f: (i, 0)),
        scratch_shapes=[pltpu.VMEM((tm, d), BF16)],
        compiler_params=_params("parallel", "arbitrary"),
        name="ffn",
    )(x2d, g.reshape(1, d), wg.astype(BF16), wu.astype(BF16), wd.astype(BF16))


def _moe_body(x_ref, g_ref, r_ref, wg_ref, wu_ref, wd_ref, o_ref,
              xn_ref, gate_ref, gate_t_ref, rank_ref, rank_t_ref, xs_ref, y_ref, *, n_exp, n_f, tm, rows_per_chunk):
    e, f = pl.program_id(1), pl.program_id(2)
    c = rows_per_chunk
    d = x_ref.shape[1]

    @pl.when((e == 0) & (f == 0))
    def _route():
        xn = _rms(x_ref[...], g_ref[...])
        xn_ref[...] = xn.astype(BF16)
        o_ref[...] = x_ref[...]
        logits = jnp.dot(xn, r_ref[...], preferred_element_type=F32, precision=lax.Precision.HIGHEST)
        lane = lax.broadcasted_iota(I32, logits.shape, 1).astype(F32)
        logits = jnp.where(lane < n_exp, logits, NEG_INF)
        big = float(logits.shape[1])
        m1 = jnp.max(logits, axis=1, keepdims=True)
        i1 = jnp.min(jnp.where(logits == m1, lane, big), axis=1, keepdims=True)
        rest = jnp.where(lane == i1, NEG_INF, logits)
        m2 = jnp.max(rest, axis=1, keepdims=True)
        i2 = jnp.min(jnp.where(rest == m2, lane, big), axis=1, keepdims=True)
        e2 = jnp.exp(m2 - m1)
        w1 = 1.0 / (1.0 + e2)
        w2 = e2 / (1.0 + e2)
        gates = jnp.where(lane == i1, w1, 0.0) + jnp.where(lane == i2, w2, 0.0)
        gates_t = gates.T
        gate_ref[...] = gates
        gate_t_ref[...] = gates_t
        t_row = lax.broadcasted_iota(I32, (tm, tm), 0)
        t_col = lax.broadcasted_iota(I32, (tm, tm), 1)
        member = jnp.where(gates > 0.0, 1.0, 0.0).astype(BF16)
        member_t = jnp.where(gates_t > 0.0, 1.0, 0.0).astype(BF16)
        rank_ref[...] = jnp.dot(jnp.where(t_col < t_row, 1.0, 0.0).astype(BF16), member,
                                preferred_element_type=F32)
        rank_t_ref[...] = jnp.dot(member_t, jnp.where(t_row < t_col, 1.0, 0.0).astype(BF16),
                                  preferred_element_type=F32)

    gate_row = gate_t_ref[pl.ds(e, 1), :]
    rank_row = rank_t_ref[pl.ds(e, 1), :]
    routed_row = gate_row > 0.0
    n_routed = jnp.sum(jnp.where(routed_row, 1.0, 0.0)).astype(I32)
    n_chunks = (n_routed + (c - 1)) // c

    def chunk_rows(ci):
        return pl.ds(pl.multiple_of(ci * c, c), c)

    def pack_matrix(ci):
        slot = (ci * c + lax.broadcasted_iota(I32, (c, tm), 0)).astype(F32)
        return routed_row & (rank_row == slot)

    @pl.when(f == 0)
    def _pack():
        def body(ci, carry):
            p = jnp.where(pack_matrix(ci), 1.0, 0.0).astype(BF16)
            xs_ref[chunk_rows(ci), :] = jnp.dot(p, xn_ref[...], preferred_element_type=F32).astype(BF16)
            y_ref[chunk_rows(ci), :] = jnp.zeros((c, d), F32)
            return carry
        lax.fori_loop(0, n_chunks, body, 0)

    def expert_body(ci, carry):
        xs = xs_ref[chunk_rows(ci), :]
        hg = jnp.dot(xs, wg_ref[...], preferred_element_type=F32)
        hu = jnp.dot(xs, wu_ref[...], preferred_element_type=F32)
        h = (hg * _sigmoid(hg)) * hu
        y_ref[chunk_rows(ci), :] += jnp.dot(h.astype(BF16), wd_ref[...], preferred_element_type=F32)
        return carry

    lax.fori_loop(0, n_chunks, expert_body, 0)

    @pl.when(f == n_f - 1)
    def _unpack():
        lane = lax.broadcasted_iota(I32, gate_ref.shape, 1)
        rank_col = jnp.sum(jnp.where(lane == e, rank_ref[...], 0.0), axis=1, keepdims=True)
        routed_col = jnp.sum(jnp.where(lane == e, gate_ref[...], 0.0), axis=1, keepdims=True) > 0.0

        def body(ci, carry):
            gate_slot = jnp.sum(jnp.where(pack_matrix(ci), gate_row, 0.0), axis=1, keepdims=True)
            y = y_ref[chunk_rows(ci), :] * gate_slot
            y_hi = y.astype(BF16)
            y_lo = (y - y_hi.astype(F32)).astype(BF16)
            slot = (ci * c + lax.broadcasted_iota(I32, (tm, c), 1)).astype(F32)
            unpack = jnp.where(routed_col & (rank_col == slot), 1.0, 0.0).astype(BF16)
            o_ref[...] += jnp.dot(jnp.concatenate([unpack, unpack], axis=1), jnp.concatenate([y_hi, y_lo], axis=0),
                                  preferred_element_type=F32)
            return carry
        lax.fori_loop(0, n_chunks, body, 0)


def _moe(x2d, g, router, wg, wu, wd):
    n, d = x2d.shape
    n_exp, _, d_ff = wg.shape
    assert TOP_K_EXPERTS == 2 and n_exp <= V7X_LANES
    tm = min(1024, n)
    rows_per_chunk = min(V7X_LANES, tm)
    assert n % tm == 0 and tm % V7X_LANES == 0
    fc = d_ff
    for cand in (896, 1024, 768, 512):
        if d_ff % cand == 0:
            fc = cand
            break
    n_f = d_ff // fc
    r_pad = jnp.concatenate([router, jnp.zeros((d, V7X_LANES - n_exp), router.dtype)], axis=1)
    return pl.pallas_call(
        functools.partial(_moe_body, n_exp=n_exp, n_f=n_f, tm=tm, rows_per_chunk=rows_per_chunk),
        out_shape=jax.ShapeDtypeStruct((n, d), F32),
        grid=(n // tm, n_exp, n_f),
        in_specs=[pl.BlockSpec((tm, d), lambda i, e, f: (i, 0)), _const_spec((1, d)), _const_spec((d, V7X_LANES)),
                  pl.BlockSpec((None, d, fc), lambda i, e, f: (e, 0, f)),
                  pl.BlockSpec((None, d, fc), lambda i, e, f: (e, 0, f)),
                  pl.BlockSpec((None, fc, d), lambda i, e, f: (e, f, 0))],
        out_specs=pl.BlockSpec((tm, d), lambda i, e, f: (i, 0)),
        scratch_shapes=[pltpu.VMEM((tm, d), BF16),
                        pltpu.VMEM((tm, V7X_LANES), F32), pltpu.VMEM((V7X_LANES, tm), F32),
                        pltpu.VMEM((tm, V7X_LANES), F32), pltpu.VMEM((V7X_LANES, tm), F32),
                        pltpu.VMEM((tm, d), BF16), pltpu.VMEM((tm, d), F32)],
        compiler_params=_params("parallel", "arbitrary", "arbitrary"),
        name="moe",
    )(x2d, g.reshape(1, d), r_pad, wg.astype(BF16), wu.astype(BF16), wd.astype(BF16))


def _final_norm_body(x_ref, g_ref, o_ref):
    o_ref[...] = _rms(x_ref[...], g_ref[...])


def _final_norm(x2d, g):
    n, d = x2d.shape
    tm = min(1024, n)
    return pl.pallas_call(
        _final_norm_body,
        out_shape=jax.ShapeDtypeStruct((n, d), F32),
        grid=(n // tm,),
        in_specs=[pl.BlockSpec((tm, d), lambda i: (i, 0)), _const_spec((1, d))],
        out_specs=pl.BlockSpec((tm, d), lambda i: (i, 0)),
        compiler_params=_params("parallel"),
        name="final_norm",
    )(x2d, g.reshape(1, d))


def _layer(x, past, h0, lw, *, layer_idx, precise_ssm):
    b, t, d = x.shape
    d_ssm = lw["ssm_d"].shape[0]
    q, k, v, iq, ik, iw, u, ga, gb = _inproj(x, lw["norm_mix_g"], lw["w_in_packed"], d_ssm)

    if past is None:
        kb = min(512, t)
        k_all, v_all, ik_all, s_valid = k, v, ik, t
    else:
        ck, cv, cik = past
        k_all = jnp.concatenate([ck.reshape(b, -1, D_ATTN), k], axis=1)
        v_all = jnp.concatenate([cv.reshape(b, -1, D_ATTN), v], axis=1)
        ik_all = jnp.concatenate([cik, ik], axis=1)
        s_valid = k_all.shape[1]
        kb = 512
        pad = (-s_valid) % kb
        padf = lambda a: jnp.pad(a, ((0, 0), (0, pad), (0, 0)))
        k_all, v_all, ik_all = padf(k_all), padf(v_all), padf(ik_all)
    att = _dsa(q, iq, iw, k_all, v_all, ik_all, n_keep=min(TOPK_MAX, s_valid // 4), causal=past is None,
               n_valid=s_valid, kb=kb)

    ssm_tm, h_re, h_im = _ssm(u.transpose(1, 0, 2), h0[0], h0[1], lw["disc"], lw["ssm_b_re"], lw["ssm_b_im"], lw["ssm_c_re"],
                              lw["ssm_c_im"], lw["ssm_d"], lw["glu_w"], lw["glu_b"], precise=precise_ssm)
    x = _merge(att, ssm_tm.transpose(1, 0, 2), ga, gb, x, lw["w_branch_attn"], lw["w_branch_ssm"], lw["w_out"])
    x2d = x.reshape(b * t, d)
    if layer_idx % 2 == 0:
        x2d = _ffn(x2d, lw["norm_ffn_g"], lw["ffn_w_gate"], lw["ffn_w_up"], lw["ffn_w_down"])
    else:
        x2d = _moe(x2d, lw["norm_ffn_g"], lw["moe_router"], lw["moe_w_gate"], lw["moe_w_up"], lw["moe_w_down"])
    g_shape = (b, -1, SSM_STATE)
    return (x2d.reshape(b, t, d), k.reshape(b, t, N_HEADS, HEAD_DIM), v.reshape(b, t, N_HEADS, HEAD_DIM), ik,
            h_re.reshape(g_shape), h_im.reshape(g_shape))


def kernel(x_prompt, x_sample, cache_k, cache_v, cache_idx_k, state_ssm_re, state_ssm_im, norm_mix_g, w_in,
           ssm_a_re, ssm_a_im, ssm_log_dt, ssm_b_re, ssm_b_im, ssm_c_re, ssm_c_im, ssm_d, glu_w, glu_b,
           w_branch_attn, w_branch_ssm, w_out, norm_ffn_g, ffn_w_gate, ffn_w_up, ffn_w_down, moe_router,
           moe_w_gate, moe_w_up, moe_w_down, final_norm_g):
    depth = w_in.shape[0]
    d_model = x_prompt.shape[-1]
    d_ssm = ssm_d.shape[1]
    xp, xs = x_prompt, x_sample
    outs_p, outs_s = [], []
    for l in range(depth):
        lw = dict(
            norm_mix_g=norm_mix_g[l], w_in_packed=_pack_w_in(w_in[l], d_ssm, d_model),
            disc=_ssm_discretize(ssm_a_re[l], ssm_a_im[l], ssm_log_dt[l]),
            ssm_b_re=ssm_b_re[l], ssm_b_im=ssm_b_im[l], ssm_c_re=ssm_c_re[l], ssm_c_im=ssm_c_im[l],
            ssm_d=ssm_d[l], glu_w=glu_w[l], glu_b=glu_b[l],
            w_branch_attn=w_branch_attn[l], w_branch_ssm=w_branch_ssm[l], w_out=w_out[l], norm_ffn_g=norm_ffn_g[l])
        i = l // 2
        if l % 2 == 0:
            lw.update(ffn_w_gate=ffn_w_gate[i], ffn_w_up=ffn_w_up[i], ffn_w_down=ffn_w_down[i])
        else:
            lw.update(moe_router=moe_router[i], moe_w_gate=moe_w_gate[i], moe_w_up=moe_w_up[i],
                      moe_w_down=moe_w_down[i])
        n_state = ssm_a_re.shape[1] * ssm_a_re.shape[2]
        zeros = jnp.zeros((xp.shape[0], n_state), F32)
        xp, *rest_p = _layer(xp, None, (zeros, zeros), lw, layer_idx=l, precise_ssm=False)
        outs_p.append(rest_p)
        h0 = (state_ssm_re[l].reshape(xs.shape[0], n_state), state_ssm_im[l].reshape(xs.shape[0], n_state))
        xs, *rest_s = _layer(xs, (cache_k[l], cache_v[l], cache_idx_k[l]), h0, lw, layer_idx=l, precise_ssm=True)
        outs_s.append(rest_s)

    y_prompt = _final_norm(xp.reshape(-1, d_model), final_norm_g).reshape(xp.shape)
    y_sample = _final_norm(xs.reshape(-1, d_model), final_norm_g).reshape(xs.shape)
    stack = lambda outs, j: jnp.stack([o[j] for o in outs], 0)
    return (y_prompt, y_sample,
            stack(outs_p, 0), stack(outs_p, 1), stack(outs_p, 2), stack(outs_p, 3), stack(outs_p, 4),
            stack(outs_s, 0), stack(outs_s, 1), stack(outs_s, 2), stack(outs_s, 3), stack(outs_s, 4))
```

```python
import functools
import math

import jax
import jax.numpy as jnp
from jax import lax
from jax.experimental import pallas as pl
from jax.experimental.pallas import tpu as pltpu

F32 = jnp.float32
BF16 = jnp.bfloat16
I32 = jnp.int32

CHUNK = 64
N_HEADS = 8
HEAD_DIM = 64
N_IDX_HEADS = 8
IDX_DIM = 64
TOPK_MAX = 256
SSM_GROUP = 16
SSM_STATE = 64
TOP_K_EXPERTS = 2
RMS_EPS = 1e-6
NEG_INF = -1e30
INT32_MIN = -(2 ** 31)

D_ATTN = N_HEADS * HEAD_DIM
D_IDX = N_IDX_HEADS * IDX_DIM
IDX_W_SCALE = float(D_IDX) ** -0.5
ATTN_SCALE = float(HEAD_DIM) ** -0.5

V7X_LANES = 128
V7X_BF16_SUBLANES = 16
V7X_VMEM_LIMIT_BYTES = 56 * 1024 * 1024


def _params(*semantics):
    return pltpu.CompilerParams(dimension_semantics=semantics, vmem_limit_bytes=V7X_VMEM_LIMIT_BYTES)


def _sigmoid(x):
    return 1.0 / (1.0 + jnp.exp(-x))


def _rms(x, g):
    return (x * lax.rsqrt(jnp.mean(x * x, axis=-1, keepdims=True) + RMS_EPS)) * g


def _const_spec(shape):
    return pl.BlockSpec(shape, lambda *_: (0,) * len(shape))


_SEG_Q, _SEG_K, _SEG_V, _SEG_IQ = 0, D_ATTN, 2 * D_ATTN, 3 * D_ATTN
_SEG_SMALL = 3 * D_ATTN + D_IDX


def _pack_w_in(w_in, d_ssm, d_model):
    offs = [0]
    for s in (D_ATTN, D_ATTN, D_ATTN, D_IDX, IDX_DIM, N_IDX_HEADS, d_ssm, d_model, d_model):
        offs.append(offs[-1] + s)
    pad = V7X_LANES - IDX_DIM - N_IDX_HEADS
    small = jnp.concatenate([w_in[:, offs[4]:offs[6]], jnp.zeros((w_in.shape[0], pad), w_in.dtype)], axis=1)
    return jnp.concatenate([w_in[:, :offs[4]], small, w_in[:, offs[6]:]], axis=1).astype(BF16)


def _inproj_body(x_ref, g_ref, w_ref, q_ref, k_ref, v_ref, iq_ref, ik_ref, iw_ref, u_ref, ga_ref, gb_ref, *, d_ssm, d_model):
    hb = _rms(x_ref[...], g_ref[...]).astype(BF16)

    def proj(lo, n):
        return jnp.dot(hb, w_ref[:, lo:lo + n], preferred_element_type=F32)

    q_ref[...] = proj(_SEG_Q, D_ATTN).astype(BF16)
    k_ref[...] = proj(_SEG_K, D_ATTN)
    v_ref[...] = proj(_SEG_V, D_ATTN)
    iq_ref[...] = proj(_SEG_IQ, D_IDX).astype(BF16)
    small = proj(_SEG_SMALL, V7X_LANES)
    ik_ref[...] = small[:, :IDX_DIM]
    iw_ref[...] = small[:, IDX_DIM:IDX_DIM + N_IDX_HEADS] * IDX_W_SCALE
    seg_u = _SEG_SMALL + V7X_LANES
    u_ref[...] = proj(seg_u, d_ssm)
    ga_ref[...] = proj(seg_u + d_ssm, d_model)
    gb_ref[...] = proj(seg_u + d_ssm + d_model, d_model)


def _inproj(x, g, w_packed, d_ssm):
    b, t, d = x.shape
    tm = min(256, t)
    n_w = w_packed.shape[1]

    def tok(n, dtype):
        return jax.ShapeDtypeStruct((b, t, n), dtype), pl.BlockSpec((None, tm, n), lambda bi, i: (bi, i, 0))

    shapes, specs = zip(
        tok(D_ATTN, BF16), tok(D_ATTN, F32), tok(D_ATTN, F32), tok(D_IDX, BF16), tok(IDX_DIM, F32),
        tok(N_IDX_HEADS, F32), tok(d_ssm, F32), tok(d, F32), tok(d, F32))
    return pl.pallas_call(
        functools.partial(_inproj_body, d_ssm=d_ssm, d_model=d),
        out_shape=shapes,
        grid=(b, t // tm),
        in_specs=[pl.BlockSpec((None, tm, d), lambda bi, i: (bi, i, 0)), _const_spec((1, d)), _const_spec((d, n_w))],
        out_specs=specs,
        compiler_params=_params("parallel", "parallel"),
        name="inproj",
    )(x, g.reshape(1, d), w_packed)


def _dsa_body(qt_ref, iqt_ref, iwt_ref, k_ref, vt_ref, ik_ref, o_ref, key_ref, qs_ref, acc_ref, m_ref, s_ref, s2_ref, *,
              tq, kb, n_keep, causal, n_valid, n_kb_total, pos_bits):
    col0 = pl.program_id(1) * tq
    if causal:
        n_kb = (col0 + tq + kb - 1) // kb
        q_pos = col0 + lax.broadcasted_iota(I32, (1, tq), 1)
        shift = CHUNK.bit_length() - 1
        q_limit = ((q_pos >> shift) + 1) << shift
    else:
        n_kb = n_kb_total
        q_limit = jnp.full((1, tq), n_valid, I32)

    def key_pos(j):
        return j * kb + lax.broadcasted_iota(I32, (kb, tq), 0)

    def key_rows(j):
        return pl.ds(pl.multiple_of(j * kb, kb), kb)

    fold_rows = min(64, kb)

    def fold(x, op):
        return op(x.reshape(kb // fold_rows, fold_rows, tq), axis=0)

    iwt = iwt_ref[...]

    def idx_block(j, carry):
        ik_blk = ik_ref[key_rows(j), :]
        s = None
        for h in range(N_IDX_HEADS):
            d = jnp.dot(ik_blk, iqt_ref[h * IDX_DIM:(h + 1) * IDX_DIM, :], preferred_element_type=F32)
            term = iwt[h:h + 1, :] * jnp.maximum(d, 0.0)
            s = term if s is None else s + term
        s = jnp.where(s == 0.0, 0.0, s)
        s = jnp.where(key_pos(j) < q_limit, s, NEG_INF)
        bits = lax.bitcast_convert_type(s, I32)
        key_ref[j] = jnp.where(bits < 0, bits ^ jnp.int32(0x7FFFFFFF), bits)
        return carry

    lax.fori_loop(0, n_kb, idx_block, 0)

    def count(pred):
        def body(j, acc):
            return acc + fold(jnp.where(pred(key_ref[j], j), 1.0, 0.0), jnp.sum)
        acc = lax.fori_loop(0, n_kb, body, jnp.zeros((fold_rows, tq), F32))
        return jnp.sum(acc, axis=0, keepdims=True)

    def bit_step(i, thr):
        cand = thr + lax.shift_left(jnp.int32(1), 31 - i)
        cnt = count(lambda key, j: key >= cand)
        return jnp.where(cnt >= n_keep, cand, thr)

    thr = lax.fori_loop(0, 32, bit_step, jnp.full((1, tq), INT32_MIN, I32))

    need = n_keep - count(lambda key, j: key > thr)
    n_eq = count(lambda key, j: key == thr)
    pos_all = jnp.int32(2 ** pos_bits - 1)

    def tie_search():
        def step(i, end):
            cand = end + lax.shift_left(jnp.int32(1), pos_bits - 1 - i)
            cnt = count(lambda key, j: (key == thr) & (key_pos(j) < cand))
            return jnp.where(cnt <= need, cand, end)
        return lax.fori_loop(0, pos_bits, step, jnp.zeros((1, tq), I32))

    tie_end = lax.cond(jnp.max(n_eq - need) > 0.0, tie_search, lambda: jnp.full((1, tq), pos_all, I32))

    qs_ref[...] = qt_ref[...] * jnp.asarray(ATTN_SCALE, BF16)
    m_ref[...] = jnp.full(m_ref.shape, NEG_INF, F32)
    acc_ref[...] = jnp.zeros(acc_ref.shape, F32)

    def bias_block(j, carry):
        key = key_ref[j]
        pos = key_pos(j)
        keep = (pos < q_limit) & ((key > thr) | ((key == thr) & (pos < tie_end)))
        key_ref[j] = lax.bitcast_convert_type(jnp.where(keep, 0.0, NEG_INF), I32)
        return carry

    lax.fori_loop(0, n_kb, bias_block, 0)

    ks = min(V7X_LANES, kb)
    v_rows = acc_ref.shape[1]

    def issue_scores(j, r0, buf):
        k_rows = pl.ds(pl.multiple_of(j * kb + r0, ks), ks)
        for h in range(N_HEADS):
            buf[h] = jnp.dot(k_ref[h, k_rows, :], qs_ref[h * HEAD_DIM:(h + 1) * HEAD_DIM, :],
                             preferred_element_type=F32)

    def att_block(j, carry):
        tiles = list(range(0, kb, ks))
        bufs = [s_ref, s2_ref]
        issue_scores(j, tiles[0], bufs[0])
        for i, r0 in enumerate(tiles):
            if i + 1 < len(tiles):
                issue_scores(j, tiles[i + 1], bufs[(i + 1) % 2])
            bias = lax.bitcast_convert_type(key_ref[j, r0:r0 + ks, :], F32)
            for h in range(N_HEADS):
                s = bufs[i % 2][h] + bias
                m_old = m_ref[h]
                m_new = jnp.maximum(m_old, jnp.max(s, axis=0, keepdims=True))
                alpha = jnp.exp(m_old - m_new)
                p = jnp.exp((s.reshape(ks // 8, 8, tq) - m_new[None]).reshape(ks, tq).astype(BF16))
                pv = jnp.dot(vt_ref[j, h, :, r0:r0 + ks], p, preferred_element_type=F32)
                acc = acc_ref[h].reshape(v_rows // 8, 8, tq) * alpha[None]
                acc_ref[h] = acc.reshape(v_rows, tq) + pv
                m_ref[h] = m_new
        return carry

    lax.fori_loop(0, n_kb, att_block, 0)
    for h in range(N_HEADS):
        rows = slice(h * HEAD_DIM, (h + 1) * HEAD_DIM)
        o_ref[rows, :] = (acc_ref[h, :HEAD_DIM, :] / acc_ref[h, HEAD_DIM:HEAD_DIM + 1, :]).astype(o_ref.dtype)


def _dsa(q, iq, iw, k, v, ik, *, n_keep, causal, n_valid, kb):
    b, t, _ = q.shape
    s_pad = k.shape[1]
    n_kb_total = s_pad // kb
    t_pad = -(-t // V7X_LANES) * V7X_LANES
    tq = min(256, t_pad)
    assert t_pad % tq == 0 and (not causal or (t_pad == t and tq % CHUNK == 0))
    to_lanes = lambda a: jnp.pad(a.transpose(0, 2, 1), ((0, 0), (0, 0), (0, t_pad - t)))
    k_hm = k.astype(BF16).reshape(b, s_pad, N_HEADS, HEAD_DIM).transpose(0, 2, 1, 3)
    vt_blk = v.astype(BF16).reshape(b, n_kb_total, kb, N_HEADS, HEAD_DIM).transpose(0, 1, 3, 4, 2)
    vt_blk = jnp.concatenate([vt_blk, jnp.ones((b, n_kb_total, N_HEADS, V7X_BF16_SUBLANES, kb), BF16)], axis=3)
    v_rows = HEAD_DIM + V7X_BF16_SUBLANES
    body = functools.partial(_dsa_body, tq=tq, kb=kb, n_keep=float(n_keep), causal=causal, n_valid=n_valid,
                             n_kb_total=n_kb_total, pos_bits=s_pad.bit_length())
    qry = lambda n: pl.BlockSpec((None, n, tq), lambda bi, i: (bi, 0, i))
    out_t = pl.pallas_call(
        body,
        out_shape=jax.ShapeDtypeStruct((b, D_ATTN, t_pad), BF16),
        grid=(b, t_pad // tq),
        in_specs=[qry(D_ATTN), qry(D_IDX), qry(N_IDX_HEADS),
                  pl.BlockSpec((None, N_HEADS, s_pad, HEAD_DIM), lambda bi, i: (bi, 0, 0, 0)),
                  pl.BlockSpec((None, n_kb_total, N_HEADS, v_rows, kb), lambda bi, i: (bi, 0, 0, 0, 0)),
                  pl.BlockSpec((None, s_pad, IDX_DIM), lambda bi, i: (bi, 0, 0))],
        out_specs=qry(D_ATTN),
        scratch_shapes=[pltpu.VMEM((n_kb_total, kb, tq), I32),
                        pltpu.VMEM((D_ATTN, tq), BF16),
                        pltpu.VMEM((N_HEADS, v_rows, tq), F32),
                        pltpu.VMEM((N_HEADS, 8, tq), F32),
                        pltpu.VMEM((N_HEADS, min(V7X_LANES, kb), tq), F32),
                        pltpu.VMEM((N_HEADS, min(V7X_LANES, kb), tq), F32)],
        compiler_params=_params("parallel", "arbitrary"),
        name="dsa",
    )(to_lanes(q), to_lanes(iq), to_lanes(iw), k_hm, vt_blk, ik.astype(BF16))
    return out_t[:, :, :t].transpose(0, 2, 1)


def _ssm_discretize_body(a_re_ref, a_im_ref, log_dt_ref, ab_re_ref, ab_im_ref, bc_re_ref, bc_im_ref):
    a_re, a_im = a_re_ref[...], a_im_ref[...]
    dt = jnp.exp(log_dt_ref[...])
    mag = jnp.exp(dt * a_re)
    ab_re = mag * jnp.cos(dt * a_im)
    ab_im = mag * jnp.sin(dt * a_im)
    den = a_re * a_re + a_im * a_im
    nr = ab_re - 1.0
    ni = ab_im
    ab_re_ref[...] = ab_re
    ab_im_ref[...] = ab_im
    bc_re_ref[...] = (nr * a_re + ni * a_im) / den
    bc_im_ref[...] = (ni * a_re - nr * a_im) / den


def _ssm_discretize(a_re, a_im, log_dt):
    g, p = a_re.shape
    out = jax.ShapeDtypeStruct((g, p), F32)
    return pl.pallas_call(_ssm_discretize_body, out_shape=(out,) * 4, name="ssm_discretize")(
        a_re, a_im, log_dt.reshape(g, 1))


def _block_diag_halves(w, dtype):
    g, r, c = w.shape
    gh = g // 2
    eye = jnp.eye(gh, dtype=w.dtype)
    halves = [jnp.einsum("grc,gh->grhc", w[i * gh:(i + 1) * gh], eye).reshape(gh * r, gh * c) for i in range(2)]
    return jnp.stack(halves, 0).astype(dtype)


def _ssm_body(u_ref, h0r_ref, h0i_ref, abr_ref, abi_ref, bcr_ref, bci_ref, br_ref, bi_ref, cr_ref, ci_ref,
              d_ref, gw_ref, gb_ref, y_ref, hr_ref, hi_ref, xr_ref, xi_ref, *, tb, nb, precise):
    d_ssm = u_ref.shape[-1]
    n_state = hr_ref.shape[-1]
    kh, sh = d_ssm // 2, n_state // 2

    def mm(a, w):
        if precise:
            return jnp.dot(a, w, preferred_element_type=F32, precision=lax.Precision.HIGHEST)
        return jnp.dot(a.astype(BF16), w, preferred_element_type=F32)

    @pl.when(pl.program_id(0) == 0)
    def _():
        hr_ref[...] = h0r_ref[...]
        hi_ref[...] = h0i_ref[...]

    u = u_ref[...].reshape(tb * nb, d_ssm)

    for half in range(2):
        cols = slice(half * sh, (half + 1) * sh)
        uh = u[:, half * kh:(half + 1) * kh]
        bu_r = mm(uh, br_ref[half])
        bu_i = mm(uh, bi_ref[half])
        bc_r, bc_i = bcr_ref[:, cols], bci_ref[:, cols]
        xr_ref[:, cols] = bc_r * bu_r - bc_i * bu_i
        xi_ref[:, cols] = bc_r * bu_i + bc_i * bu_r

    slab = min(512, n_state)
    for s0 in range(0, n_state, slab):
        cols = slice(s0, s0 + slab)
        a_r = jnp.broadcast_to(abr_ref[:, cols], (nb, slab))
        a_i = jnp.broadcast_to(abi_ref[:, cols], (nb, slab))

        def step(t, h):
            h_r, h_i = h
            rows = pl.ds(pl.multiple_of(t * nb, nb), nb)
            n_r = a_r * h_r - a_i * h_i + xr_ref[rows, cols]
            n_i = a_r * h_i + a_i * h_r + xi_ref[rows, cols]
            xr_ref[rows, cols] = n_r
            xi_ref[rows, cols] = n_i
            return n_r, n_i

        h_r, h_i = lax.fori_loop(0, tb, step, (hr_ref[:, cols], hi_ref[:, cols]), unroll=min(8, tb))
        hr_ref[:, cols] = h_r
        hi_ref[:, cols] = h_i

    ys = []
    for half in range(2):
        cols = slice(half * sh, (half + 1) * sh)
        ys.append(mm(xr_ref[:, cols], cr_ref[half]) - mm(xi_ref[:, cols], ci_ref[half]))
    y = jnp.concatenate(ys, axis=1) + d_ref[...] * u
    y = y * (0.5 * (1.0 + jnp.tanh(math.sqrt(2.0 / math.pi) * (y + 0.044715 * (y * y * y)))))
    z = mm(y, gw_ref[...]) + gb_ref[...]
    y_ref[...] = (y * _sigmoid(z)).reshape(tb, nb, d_ssm)


def _ssm(u_tm, h0_re, h0_im, disc, b_re, b_im, c_re, c_im, d, glu_w, glu_b, *, precise):
    t, nb, d_ssm = u_tm.shape
    n_state = h0_re.shape[1]
    assert nb % 8 == 0
    tb = min(32, t)
    wdt = F32 if precise else BF16
    br = _block_diag_halves(jnp.swapaxes(b_re, 1, 2), wdt)
    bi = _block_diag_halves(jnp.swapaxes(b_im, 1, 2), wdt)
    cr = _block_diag_halves(jnp.swapaxes(c_re, 1, 2), wdt)
    ci = _block_diag_halves(jnp.swapaxes(c_im, 1, 2), wdt)
    row = lambda a: a.reshape(1, -1).astype(F32)
    state_spec = _const_spec((nb, n_state))
    vec_state = _const_spec((1, n_state))
    vec_ch = _const_spec((1, d_ssm))
    return pl.pallas_call(
        functools.partial(_ssm_body, tb=tb, nb=nb, precise=precise),
        out_shape=(jax.ShapeDtypeStruct((t, nb, d_ssm), F32),
                   jax.ShapeDtypeStruct((nb, n_state), F32), jax.ShapeDtypeStruct((nb, n_state), F32)),
        grid=(t // tb,),
        in_specs=[pl.BlockSpec((tb, nb, d_ssm), lambda i: (i, 0, 0)), state_spec, state_spec,
                  vec_state, vec_state, vec_state, vec_state,
                  _const_spec(br.shape), _const_spec(bi.shape), _const_spec(cr.shape), _const_spec(ci.shape),
                  vec_ch, _const_spec((d_ssm, d_ssm)), vec_ch],
        out_specs=(pl.BlockSpec((tb, nb, d_ssm), lambda i: (i, 0, 0)), state_spec, state_spec),
        scratch_shapes=[pltpu.VMEM((tb * nb, n_state), F32), pltpu.VMEM((tb * nb, n_state), F32)],
        compiler_params=_params("arbitrary"),
        name="ssm",
    )(u_tm, h0_re, h0_im, row(disc[0]), row(disc[1]), row(disc[2]), row(disc[3]), br, bi, cr, ci,
      row(d), glu_w.astype(wdt), row(glu_b))


def _merge_body(att_ref, ssm_ref, ga_ref, gb_ref, x_ref, wpa_ref, wpb_ref, wo_ref, o_ref):
    a = jnp.dot(att_ref[...], wpa_ref[...], preferred_element_type=F32)
    s = jnp.dot(ssm_ref[...].astype(BF16), wpb_ref[...], preferred_element_type=F32)
    mix = _sigmoid(ga_ref[...]) * a + _sigmoid(gb_ref[...]) * s
    o_ref[...] = x_ref[...] + jnp.dot(mix.astype(BF16), wo_ref[...], preferred_element_type=F32)


def _merge(att, ssm, ga, gb, x, w_pa, w_pb, w_o):
    b, t, d = x.shape
    tm = min(512, t)
    tok = lambda n: pl.BlockSpec((None, tm, n), lambda bi, i: (bi, i, 0))
    d_ssm = ssm.shape[-1]
    return pl.pallas_call(
        _merge_body,
        out_shape=jax.ShapeDtypeStruct((b, t, d), F32),
        grid=(b, t // tm),
        in_specs=[tok(D_ATTN), tok(d_ssm), tok(d), tok(d), tok(d),
                  _const_spec(w_pa.shape), _const_spec(w_pb.shape), _const_spec(w_o.shape)],
        out_specs=tok(d),
        compiler_params=_params("parallel", "parallel"),
        name="merge",
    )(att, ssm, ga, gb, x, w_pa.astype(BF16), w_pb.astype(BF16), w_o.astype(BF16))


def _ffn_body(x_ref, g_ref, wg_ref, wu_ref, wd_ref, o_ref, xn_ref, *, n_f):
    f = pl.program_id(1)

    @pl.when(f == 0)
    def _():
        xn_ref[...] = _rms(x_ref[...], g_ref[...]).astype(BF16)
        o_ref[...] = x_ref[...]

    xn = xn_ref[...]
    hg = jnp.dot(xn, wg_ref[...], preferred_element_type=F32)
    hu = jnp.dot(xn, wu_ref[...], preferred_element_type=F32)
    h = (hg * _sigmoid(hg)) * hu
    o_ref[...] += jnp.dot(h.astype(BF16), wd_ref[...], preferred_element_type=F32)


def _ffn(x2d, g, wg, wu, wd):
    n, d = x2d.shape
    d_ff = wg.shape[1]
    tm = min(512, n)
    fc = d_ff
    for cand in (1408, 1024, 896, 768, 512):
        if d_ff % cand == 0:
            fc = cand
            break
    n_f = d_ff // fc
    return pl.pallas_call(
        functools.partial(_ffn_body, n_f=n_f),
        out_shape=jax.ShapeDtypeStruct((n, d), F32),
        grid=(n // tm, n_f),
        in_specs=[pl.BlockSpec((tm, d), lambda i, f: (i, 0)), _const_spec((1, d)),
                  pl.BlockSpec((d, fc), lambda i, f: (0, f)), pl.BlockSpec((d, fc), lambda i, f: (0, f)),
                  pl.BlockSpec((fc, d), lambda i, f: (f, 0))],
        out_specs=pl.BlockSpec((tm, d), lambda i, f: (i, 0)),
        scratch_shapes=[pltpu.VMEM((tm, d), BF16)],
        compiler_params=_params("parallel", "arbitrary"),
        name="ffn",
    )(x2d, g.reshape(1, d), wg.astype(BF16), wu.astype(BF16), wd.astype(BF16))


def _moe_route_body(x_ref, g_ref, r_ref, xn_ref, gate_ref, gate_t_ref, rank_ref, rank_t_ref, *, n_exp, tm):
    xn = _rms(x_ref[...], g_ref[...])
    xn_ref[...] = xn.astype(BF16)
    logits = jnp.dot(xn, r_ref[...], preferred_element_type=F32, precision=lax.Precision.HIGHEST)
    lane = lax.broadcasted_iota(I32, logits.shape, 1).astype(F32)
    logits = jnp.where(lane < n_exp, logits, NEG_INF)
    big = float(logits.shape[1])
    m1 = jnp.max(logits, axis=1, keepdims=True)
    i1 = jnp.min(jnp.where(logits == m1, lane, big), axis=1, keepdims=True)
    rest = jnp.where(lane == i1, NEG_INF, logits)
    m2 = jnp.max(rest, axis=1, keepdims=True)
    i2 = jnp.min(jnp.where(rest == m2, lane, big), axis=1, keepdims=True)
    e2 = jnp.exp(m2 - m1)
    w1 = 1.0 / (1.0 + e2)
    w2 = e2 / (1.0 + e2)
    gates = jnp.where(lane == i1, w1, 0.0) + jnp.where(lane == i2, w2, 0.0)
    gates_t = gates.T
    gate_ref[...] = gates
    gate_t_ref[...] = gates_t
    t_row = lax.broadcasted_iota(I32, (tm, tm), 0)
    t_col = lax.broadcasted_iota(I32, (tm, tm), 1)
    member = jnp.where(gates > 0.0, 1.0, 0.0).astype(BF16)
    member_t = jnp.where(gates_t > 0.0, 1.0, 0.0).astype(BF16)
    rank_ref[...] = jnp.dot(jnp.where(t_col < t_row, 1.0, 0.0).astype(BF16), member, preferred_element_type=F32)
    rank_t_ref[...] = jnp.dot(member_t, jnp.where(t_row < t_col, 1.0, 0.0).astype(BF16),
                              preferred_element_type=F32)


def _moe_expert_body(acc_ref, xn_ref, gate_ref, gate_t_ref, rank_ref, rank_t_ref, wg_ref, wu_ref, wd_ref, o_ref, *,
                     fc, tm, rows_per_chunk, single_tile):
    e = pl.program_id(0)
    c = rows_per_chunk
    d_ff = wg_ref.shape[1]
    if single_tile:
        @pl.when(e == 0)
        def _():
            o_ref[...] = acc_ref[...]
    else:
        o_ref[...] = acc_ref[...]

    gate_row = gate_t_ref[pl.ds(e, 1), :]
    rank_row = rank_t_ref[pl.ds(e, 1), :]
    routed_row = gate_row > 0.0
    n_routed = jnp.sum(jnp.where(routed_row, 1.0, 0.0)).astype(I32)
    n_chunks = (n_routed + (c - 1)) // c

    lane = lax.broadcasted_iota(I32, gate_ref.shape, 1)
    rank_col = jnp.sum(jnp.where(lane == e, rank_ref[...], 0.0), axis=1, keepdims=True)
    routed_col = jnp.sum(jnp.where(lane == e, gate_ref[...], 0.0), axis=1, keepdims=True) > 0.0

    def chunk(ci, carry):
        slot_rows = (ci * c + lax.broadcasted_iota(I32, (c, tm), 0)).astype(F32)
        pack = routed_row & (rank_row == slot_rows)
        xs = jnp.dot(jnp.where(pack, 1.0, 0.0).astype(BF16), xn_ref[...], preferred_element_type=F32).astype(BF16)
        y = None
        for f0 in range(0, d_ff, fc):
            hg = jnp.dot(xs, wg_ref[:, f0:f0 + fc], preferred_element_type=F32)
            hu = jnp.dot(xs, wu_ref[:, f0:f0 + fc], preferred_element_type=F32)
            h = ((hg * _sigmoid(hg)) * hu).astype(BF16)
            part = jnp.dot(h, wd_ref[f0:f0 + fc, :], preferred_element_type=F32)
            y = part if y is None else y + part
        y = y * jnp.sum(jnp.where(pack, gate_row, 0.0), axis=1, keepdims=True)
        y_hi = y.astype(BF16)
        y_lo = (y - y_hi.astype(F32)).astype(BF16)
        slot_cols = (ci * c + lax.broadcasted_iota(I32, (tm, c), 1)).astype(F32)
        unpack = jnp.where(routed_col & (rank_col == slot_cols), 1.0, 0.0).astype(BF16)
        o_ref[...] += jnp.dot(jnp.concatenate([unpack, unpack], axis=1), jnp.concatenate([y_hi, y_lo], axis=0),
                              preferred_element_type=F32)
        return carry

    lax.fori_loop(0, n_chunks, chunk, 0)


def _moe(x2d, g, router, wg, wu, wd):
    n, d = x2d.shape
    n_exp, _, d_ff = wg.shape
    assert TOP_K_EXPERTS == 2 and n_exp <= V7X_LANES
    tm = min(1024, n)
    rows_per_chunk = min(V7X_LANES, tm)
    assert n % tm == 0 and tm % V7X_LANES == 0
    n_tiles = n // tm
    assert n_tiles == 1 or n_tiles >= 4
    fc = d_ff
    for cand in (896, 1024, 768, 512):
        if d_ff % cand == 0:
            fc = cand
            break
    r_pad = jnp.concatenate([router, jnp.zeros((d, V7X_LANES - n_exp), router.dtype)], axis=1)
    tok = lambda width: pl.BlockSpec((tm, width), lambda i: (i, 0))
    per_tile = pl.BlockSpec((None, V7X_LANES, tm), lambda i: (i, 0, 0))
    xn, gate, gate_t, rank, rank_t = pl.pallas_call(
        functools.partial(_moe_route_body, n_exp=n_exp, tm=tm),
        out_shape=(jax.ShapeDtypeStruct((n, d), BF16),
                   jax.ShapeDtypeStruct((n, V7X_LANES), F32), jax.ShapeDtypeStruct((n_tiles, V7X_LANES, tm), F32),
                   jax.ShapeDtypeStruct((n, V7X_LANES), F32), jax.ShapeDtypeStruct((n_tiles, V7X_LANES, tm), F32)),
        grid=(n_tiles,),
        in_specs=[tok(d), _const_spec((1, d)), _const_spec((d, V7X_LANES))],
        out_specs=(tok(d), tok(V7X_LANES), per_tile, tok(V7X_LANES), per_tile),
        compiler_params=_params("parallel"),
        name="moe_route",
    )(x2d, g.reshape(1, d), r_pad)

    tok2 = lambda width: pl.BlockSpec((tm, width), lambda e, i: (i, 0))
    per_tile2 = pl.BlockSpec((None, V7X_LANES, tm), lambda e, i: (i, 0, 0))
    w_spec = lambda rows, cols: pl.BlockSpec((None, rows, cols), lambda e, i: (e, 0, 0), pipeline_mode=pl.Buffered(1))
    return pl.pallas_call(
        functools.partial(_moe_expert_body, fc=fc, tm=tm, rows_per_chunk=rows_per_chunk, single_tile=n_tiles == 1),
        out_shape=jax.ShapeDtypeStruct((n, d), F32),
        grid=(n_exp, n_tiles),
        in_specs=[tok2(d), tok2(d), tok2(V7X_LANES), per_tile2, tok2(V7X_LANES), per_tile2,
                  w_spec(d, d_ff), w_spec(d, d_ff), w_spec(d_ff, d)],
        out_specs=tok2(d),
        input_output_aliases={0: 0},
        compiler_params=_params("arbitrary", "arbitrary"),
        name="moe_experts",
    )(x2d, xn, gate, gate_t, rank, rank_t, wg.astype(BF16), wu.astype(BF16), wd.astype(BF16))


def _final_norm_body(x_ref, g_ref, o_ref):
    o_ref[...] = _rms(x_ref[...], g_ref[...])


def _final_norm(x2d, g):
    n, d = x2d.shape
    tm = min(1024, n)
    return pl.pallas_call(
        _final_norm_body,
        out_shape=jax.ShapeDtypeStruct((n, d), F32),
        grid=(n // tm,),
        in_specs=[pl.BlockSpec((tm, d), lambda i: (i, 0)), _const_spec((1, d))],
        out_specs=pl.BlockSpec((tm, d), lambda i: (i, 0)),
        compiler_params=_params("parallel"),
        name="final_norm",
    )(x2d, g.reshape(1, d))


def _layer(x, past, h0, lw, *, layer_idx, precise_ssm):
    b, t, d = x.shape
    d_ssm = lw["ssm_d"].shape[0]
    q, k, v, iq, ik, iw, u, ga, gb = _inproj(x, lw["norm_mix_g"], lw["w_in_packed"], d_ssm)

    if past is None:
        kb = min(512, t)
        k_all, v_all, ik_all, s_valid = k, v, ik, t
    else:
        ck, cv, cik = past
        k_all = jnp.concatenate([ck.reshape(b, -1, D_ATTN), k], axis=1)
        v_all = jnp.concatenate([cv.reshape(b, -1, D_ATTN), v], axis=1)
        ik_all = jnp.concatenate([cik, ik], axis=1)
        s_valid = k_all.shape[1]
        kb = 512
        pad = (-s_valid) % kb
        padf = lambda a: jnp.pad(a, ((0, 0), (0, pad), (0, 0)))
        k_all, v_all, ik_all = padf(k_all), padf(v_all), padf(ik_all)
    att = _dsa(q, iq, iw, k_all, v_all, ik_all, n_keep=min(TOPK_MAX, s_valid // 4), causal=past is None,
               n_valid=s_valid, kb=kb)

    ssm_tm, h_re, h_im = _ssm(u.transpose(1, 0, 2), h0[0], h0[1], lw["disc"], lw["ssm_b_re"], lw["ssm_b_im"], lw["ssm_c_re"],
                              lw["ssm_c_im"], lw["ssm_d"], lw["glu_w"], lw["glu_b"], precise=precise_ssm)
    x = _merge(att, ssm_tm.transpose(1, 0, 2), ga, gb, x, lw["w_branch_attn"], lw["w_branch_ssm"], lw["w_out"])
    x2d = x.reshape(b * t, d)
    if layer_idx % 2 == 0:
        x2d = _ffn(x2d, lw["norm_ffn_g"], lw["ffn_w_gate"], lw["ffn_w_up"], lw["ffn_w_down"])
    else:
        x2d = _moe(x2d, lw["norm_ffn_g"], lw["moe_router"], lw["moe_w_gate"], lw["moe_w_up"], lw["moe_w_down"])
    g_shape = (b, -1, SSM_STATE)
    return (x2d.reshape(b, t, d), k.reshape(b, t, N_HEADS, HEAD_DIM), v.reshape(b, t, N_HEADS, HEAD_DIM), ik,
            h_re.reshape(g_shape), h_im.reshape(g_shape))


def kernel(x_prompt, x_sample, cache_k, cache_v, cache_idx_k, state_ssm_re, state_ssm_im, norm_mix_g, w_in,
           ssm_a_re, ssm_a_im, ssm_log_dt, ssm_b_re, ssm_b_im, ssm_c_re, ssm_c_im, ssm_d, glu_w, glu_b,
           w_branch_attn, w_branch_ssm, w_out, norm_ffn_g, ffn_w_gate, ffn_w_up, ffn_w_down, moe_router,
           moe_w_gate, moe_w_up, moe_w_down, final_norm_g):
    depth = w_in.shape[0]
    d_model = x_prompt.shape[-1]
    d_ssm = ssm_d.shape[1]
    xp, xs = x_prompt, x_sample
    outs_p, outs_s = [], []
    for l in range(depth):
        lw = dict(
            norm_mix_g=norm_mix_g[l], w_in_packed=_pack_w_in(w_in[l], d_ssm, d_model),
            disc=_ssm_discretize(ssm_a_re[l], ssm_a_im[l], ssm_log_dt[l]),
            ssm_b_re=ssm_b_re[l], ssm_b_im=ssm_b_im[l], ssm_c_re=ssm_c_re[l], ssm_c_im=ssm_c_im[l],
            ssm_d=ssm_d[l], glu_w=glu_w[l], glu_b=glu_b[l],
            w_branch_attn=w_branch_attn[l], w_branch_ssm=w_branch_ssm[l], w_out=w_out[l], norm_ffn_g=norm_ffn_g[l])
        i = l // 2
        if l % 2 == 0:
            lw.update(ffn_w_gate=ffn_w_gate[i], ffn_w_up=ffn_w_up[i], ffn_w_down=ffn_w_down[i])
        else:
            lw.update(moe_router=moe_router[i], moe_w_gate=moe_w_gate[i], moe_w_up=moe_w_up[i],
                      moe_w_down=moe_w_down[i])
        n_state = ssm_a_re.shape[1] * ssm_a_re.shape[2]
        zeros = jnp.zeros((xp.shape[0], n_state), F32)
        xp, *rest_p = _layer(xp, None, (zeros, zeros), lw, layer_idx=l, precise_ssm=False)
        outs_p.append(rest_p)
        h0 = (state_ssm_re[l].reshape(xs.shape[0], n_state), state_ssm_im[l].reshape(xs.shape[0], n_state))
        xs, *rest_s = _layer(xs, (cache_k[l], cache_v[l], cache_idx_k[l]), h0, lw, layer_idx=l, precise_ssm=True)
        outs_s.append(rest_s)

    y_prompt = _final_norm(xp.reshape(-1, d_model), final_norm_g).reshape(xp.shape)
    y_sample = _final_norm(xs.reshape(-1, d_model), final_norm_g).reshape(xs.shape)
    stack = lambda outs, j: jnp.stack([o[j] for o in outs], 0)
    return (y_prompt, y_sample,
            stack(outs_p, 0), stack(outs_p, 1), stack(outs_p, 2), stack(outs_p, 3), stack(outs_p, 4),
            stack(outs_s, 0), stack(outs_s, 1), stack(outs_s, 2), stack(outs_s, 3), stack(outs_s, 4))
```

```python
import functools
import math

import jax
import jax.numpy as jnp
from jax import lax
from jax.experimental import pallas as pl
from jax.experimental.pallas import tpu as pltpu

F32 = jnp.float32
BF16 = jnp.bfloat16
I32 = jnp.int32

CHUNK = 64
N_HEADS = 8
HEAD_DIM = 64
N_IDX_HEADS = 8
IDX_DIM = 64
TOPK_MAX = 256
SSM_GROUP = 16
SSM_STATE = 64
TOP_K_EXPERTS = 2
RMS_EPS = 1e-6
NEG_INF = -1e30
INT32_MIN = -(2 ** 31)

D_ATTN = N_HEADS * HEAD_DIM
D_IDX = N_IDX_HEADS * IDX_DIM
IDX_W_SCALE = float(D_IDX) ** -0.5
ATTN_SCALE = float(HEAD_DIM) ** -0.5

V7X_LANES = 128
V7X_BF16_SUBLANES = 16
V7X_VMEM_LIMIT_BYTES = 56 * 1024 * 1024


def _params(*semantics):
    return pltpu.CompilerParams(dimension_semantics=semantics, vmem_limit_bytes=V7X_VMEM_LIMIT_BYTES)


def _sigmoid(x):
    return 1.0 / (1.0 + jnp.exp(-x))


def _rms(x, g):
    return (x * lax.rsqrt(jnp.mean(x * x, axis=-1, keepdims=True) + RMS_EPS)) * g


def _const_spec(shape):
    return pl.BlockSpec(shape, lambda *_: (0,) * len(shape))


_SEG_Q, _SEG_K, _SEG_V, _SEG_IQ = 0, D_ATTN, 2 * D_ATTN, 3 * D_ATTN
_SEG_SMALL = 3 * D_ATTN + D_IDX


def _pack_w_in(w_in, d_ssm, d_model):
    offs = [0]
    for s in (D_ATTN, D_ATTN, D_ATTN, D_IDX, IDX_DIM, N_IDX_HEADS, d_ssm, d_model, d_model):
        offs.append(offs[-1] + s)
    pad = V7X_LANES - IDX_DIM - N_IDX_HEADS
    small = jnp.concatenate([w_in[:, offs[4]:offs[6]], jnp.zeros((w_in.shape[0], pad), w_in.dtype)], axis=1)
    return jnp.concatenate([w_in[:, :offs[4]], small, w_in[:, offs[6]:]], axis=1).astype(BF16)


def _inproj_body(x_ref, g_ref, w_ref, q_ref, k_ref, v_ref, iq_ref, ik_ref, iw_ref, u_ref, ga_ref, gb_ref, *, d_ssm, d_model):
    hb = _rms(x_ref[...], g_ref[...]).astype(BF16)

    def proj(lo, n):
        return jnp.dot(hb, w_ref[:, lo:lo + n], preferred_element_type=F32)

    q_ref[...] = proj(_SEG_Q, D_ATTN).astype(BF16)
    k_ref[...] = proj(_SEG_K, D_ATTN)
    v_ref[...] = proj(_SEG_V, D_ATTN)
    iq_ref[...] = proj(_SEG_IQ, D_IDX).astype(BF16)
    small = proj(_SEG_SMALL, V7X_LANES)
    ik_ref[...] = small[:, :IDX_DIM]
    iw_ref[...] = small[:, IDX_DIM:IDX_DIM + N_IDX_HEADS] * IDX_W_SCALE
    seg_u = _SEG_SMALL + V7X_LANES
    u_ref[...] = proj(seg_u, d_ssm)
    ga_ref[...] = proj(seg_u + d_ssm, d_model)
    gb_ref[...] = proj(seg_u + d_ssm + d_model, d_model)


def _inproj(x, g, w_packed, d_ssm):
    b, t, d = x.shape
    tm = min(256, t)
    n_w = w_packed.shape[1]

    def tok(n, dtype):
        return jax.ShapeDtypeStruct((b, t, n), dtype), pl.BlockSpec((None, tm, n), lambda bi, i: (bi, i, 0))

    shapes, specs = zip(
        tok(D_ATTN, BF16), tok(D_ATTN, F32), tok(D_ATTN, F32), tok(D_IDX, BF16), tok(IDX_DIM, F32),
        tok(N_IDX_HEADS, F32), tok(d_ssm, F32), tok(d, F32), tok(d, F32))
    return pl.pallas_call(
        functools.partial(_inproj_body, d_ssm=d_ssm, d_model=d),
        out_shape=shapes,
        grid=(b, t // tm),
        in_specs=[pl.BlockSpec((None, tm, d), lambda bi, i: (bi, i, 0)), _const_spec((1, d)), _const_spec((d, n_w))],
        out_specs=specs,
        compiler_params=_params("parallel", "parallel"),
        name="inproj",
    )(x, g.reshape(1, d), w_packed)


def _dsa_body(qt_ref, iqt_ref, iwt_ref, k_ref, vt_ref, ik_ref, o_ref, key_ref, top_ref, qs_ref, acc_ref, m_ref, s_ref, s2_ref, *,
              tq, kb, n_keep, causal, n_valid, n_kb_total, pos_bits):
    col0 = pl.program_id(1) * tq
    if causal:
        n_kb = (col0 + tq + kb - 1) // kb
        q_pos = col0 + lax.broadcasted_iota(I32, (1, tq), 1)
        shift = CHUNK.bit_length() - 1
        q_limit = ((q_pos >> shift) + 1) << shift
    else:
        n_kb = n_kb_total
        q_limit = jnp.full((1, tq), n_valid, I32)

    def key_pos(j):
        return j * kb + lax.broadcasted_iota(I32, (kb, tq), 0)

    def key_rows(j):
        return pl.ds(pl.multiple_of(j * kb, kb), kb)

    fold_rows = min(64, kb)

    def fold(x, op):
        return op(x.reshape(kb // fold_rows, fold_rows, tq), axis=0)

    iwt = iwt_ref[...]

    def idx_block(j, carry):
        ik_blk = ik_ref[key_rows(j), :]
        s = None
        for h in range(N_IDX_HEADS):
            d = jnp.dot(ik_blk, iqt_ref[h * IDX_DIM:(h + 1) * IDX_DIM, :], preferred_element_type=F32)
            term = iwt[h:h + 1, :] * jnp.maximum(d, 0.0)
            s = term if s is None else s + term
        s = jnp.where(s == 0.0, 0.0, s)
        s = jnp.where(key_pos(j) < q_limit, s, NEG_INF)
        bits = lax.bitcast_convert_type(s, I32)
        key_ref[j] = jnp.where(bits < 0, bits ^ jnp.int32(0x7FFFFFFF), bits)
        top_ref[j] = lax.bitcast_convert_type(bits & jnp.int32(-65536), F32).astype(BF16)
        return carry

    lax.fori_loop(0, n_kb, idx_block, 0)

    def count(pred):
        def body(j, acc):
            return acc + fold(jnp.where(pred(key_ref[j], j), 1.0, 0.0), jnp.sum)
        acc = lax.fori_loop(0, n_kb, body, jnp.zeros((fold_rows, tq), F32))
        return jnp.sum(acc, axis=0, keepdims=True)

    def count_top(cand):
        one, zero = jnp.ones((), BF16), jnp.zeros((), BF16)
        assert (kb // fold_rows) * n_kb_total <= 256

        def body(j, acc):
            hit = jnp.where(top_ref[j] >= cand, one, zero)
            parts = [hit[r:r + fold_rows] for r in range(0, kb, fold_rows)]
            while len(parts) > 1:
                parts = [a + b for a, b in zip(parts[::2], parts[1::2])] + parts[len(parts) & ~1:]
            return acc + parts[0]
        acc = lax.fori_loop(0, n_kb, body, jnp.zeros((fold_rows, tq), BF16))
        return jnp.sum(acc.astype(F32), axis=0, keepdims=True)

    def bit_step(i, thr):
        cand = thr + lax.shift_left(jnp.int32(1), 31 - i)
        cnt = count(lambda key, j: key >= cand)
        return jnp.where(cnt >= n_keep, cand, thr)

    def top_step(i, top):
        cand = top + lax.shift_left(jnp.int32(1), 31 - i - 16)
        pattern = jnp.where(cand < 0, cand ^ jnp.int32(0x7FFF), cand)
        cand_bf16 = lax.bitcast_convert_type(lax.shift_left(pattern, 16), F32).astype(BF16)
        return jnp.where(count_top(cand_bf16) >= n_keep, cand, top)

    def tie_counts(thr):
        return n_keep - count(lambda key, j: key > thr), count(lambda key, j: key == thr)

    thr_sign = bit_step(0, jnp.full((1, tq), INT32_MIN, I32))
    top = lax.fori_loop(1, 16, top_step, lax.shift_right_arithmetic(thr_sign, 16))
    thr = lax.fori_loop(16, 32, bit_step, lax.shift_left(top, 16))
    need, n_eq = tie_counts(thr)

    def int_search():
        thr_int = lax.fori_loop(1, 32, bit_step, thr_sign)
        return (thr_int,) + tie_counts(thr_int)

    consistent = jnp.min(jnp.where((need >= 1.0) & (need <= n_eq), 1.0, 0.0)) > 0.0
    thr, need, n_eq = lax.cond(consistent, lambda: (thr, need, n_eq), int_search)

    pos_all = jnp.int32(2 ** pos_bits - 1)

    def tie_search():
        def step(i, end):
            cand = end + lax.shift_left(jnp.int32(1), pos_bits - 1 - i)
            cnt = count(lambda key, j: (key == thr) & (key_pos(j) < cand))
            return jnp.where(cnt <= need, cand, end)
        return lax.fori_loop(0, pos_bits, step, jnp.zeros((1, tq), I32))

    tie_end = lax.cond(jnp.max(n_eq - need) > 0.0, tie_search, lambda: jnp.full((1, tq), pos_all, I32))

    qs_ref[...] = qt_ref[...] * jnp.asarray(ATTN_SCALE, BF16)
    m_ref[...] = jnp.full(m_ref.shape, NEG_INF, F32)
    acc_ref[...] = jnp.zeros(acc_ref.shape, F32)

    def bias_block(j, carry):
        key = key_ref[j]
        pos = key_pos(j)
        keep = (pos < q_limit) & ((key > thr) | ((key == thr) & (pos < tie_end)))
        key_ref[j] = lax.bitcast_convert_type(jnp.where(keep, 0.0, NEG_INF), I32)
        return carry

    lax.fori_loop(0, n_kb, bias_block, 0)

    ks = min(V7X_LANES, kb)
    v_rows = acc_ref.shape[1]

    def issue_scores(j, r0, buf):
        k_rows = pl.ds(pl.multiple_of(j * kb + r0, ks), ks)
        for h in range(N_HEADS):
            buf[h] = jnp.dot(k_ref[h, k_rows, :], qs_ref[h * HEAD_DIM:(h + 1) * HEAD_DIM, :],
                             preferred_element_type=F32)

    def att_block(j, carry):
        tiles = list(range(0, kb, ks))
        bufs = [s_ref, s2_ref]
        issue_scores(j, tiles[0], bufs[0])
        for i, r0 in enumerate(tiles):
            if i + 1 < len(tiles):
                issue_scores(j, tiles[i + 1], bufs[(i + 1) % 2])
            bias = lax.bitcast_convert_type(key_ref[j, r0:r0 + ks, :], F32)
            for h in range(N_HEADS):
                s = bufs[i % 2][h] + bias
                m_old = m_ref[h]
                m_new = jnp.maximum(m_old, jnp.max(s, axis=0, keepdims=True))
                alpha = jnp.exp(m_old - m_new)
                p = jnp.exp((s.reshape(ks // 8, 8, tq) - m_new[None]).reshape(ks, tq).astype(BF16))
                pv = jnp.dot(vt_ref[j, h, :, r0:r0 + ks], p, preferred_element_type=F32)
                acc = acc_ref[h].reshape(v_rows // 8, 8, tq) * alpha[None]
                acc_ref[h] = acc.reshape(v_rows, tq) + pv
                m_ref[h] = m_new
        return carry

    lax.fori_loop(0, n_kb, att_block, 0)
    for h in range(N_HEADS):
        rows = slice(h * HEAD_DIM, (h + 1) * HEAD_DIM)
        o_ref[rows, :] = (acc_ref[h, :HEAD_DIM, :] / acc_ref[h, HEAD_DIM:HEAD_DIM + 1, :]).astype(o_ref.dtype)


def _dsa(q, iq, iw, k, v, ik, *, n_keep, causal, n_valid, kb):
    b, t, _ = q.shape
    s_pad = k.shape[1]
    n_kb_total = s_pad // kb
    t_pad = -(-t // V7X_LANES) * V7X_LANES
    tq = min(256, t_pad)
    assert t_pad % tq == 0 and (not causal or (t_pad == t and tq % CHUNK == 0))
    to_lanes = lambda a: jnp.pad(a.transpose(0, 2, 1), ((0, 0), (0, 0), (0, t_pad - t)))
    k_hm = k.astype(BF16).reshape(b, s_pad, N_HEADS, HEAD_DIM).transpose(0, 2, 1, 3)
    vt_blk = v.astype(BF16).reshape(b, n_kb_total, kb, N_HEADS, HEAD_DIM).transpose(0, 1, 3, 4, 2)
    vt_blk = jnp.concatenate([vt_blk, jnp.ones((b, n_kb_total, N_HEADS, V7X_BF16_SUBLANES, kb), BF16)], axis=3)
    v_rows = HEAD_DIM + V7X_BF16_SUBLANES
    body = functools.partial(_dsa_body, tq=tq, kb=kb, n_keep=float(n_keep), causal=causal, n_valid=n_valid,
                             n_kb_total=n_kb_total, pos_bits=s_pad.bit_length())
    qry = lambda n: pl.BlockSpec((None, n, tq), lambda bi, i: (bi, 0, i))
    out_t = pl.pallas_call(
        body,
        out_shape=jax.ShapeDtypeStruct((b, D_ATTN, t_pad), BF16),
        grid=(b, t_pad // tq),
        in_specs=[qry(D_ATTN), qry(D_IDX), qry(N_IDX_HEADS),
                  pl.BlockSpec((None, N_HEADS, s_pad, HEAD_DIM), lambda bi, i: (bi, 0, 0, 0)),
                  pl.BlockSpec((None, n_kb_total, N_HEADS, v_rows, kb), lambda bi, i: (bi, 0, 0, 0, 0)),
                  pl.BlockSpec((None, s_pad, IDX_DIM), lambda bi, i: (bi, 0, 0))],
        out_specs=qry(D_ATTN),
        scratch_shapes=[pltpu.VMEM((n_kb_total, kb, tq), I32),
                        pltpu.VMEM((n_kb_total, kb, tq), BF16),
                        pltpu.VMEM((D_ATTN, tq), BF16),
                        pltpu.VMEM((N_HEADS, v_rows, tq), F32),
                        pltpu.VMEM((N_HEADS, 8, tq), F32),
                        pltpu.VMEM((N_HEADS, min(V7X_LANES, kb), tq), F32),
                        pltpu.VMEM((N_HEADS, min(V7X_LANES, kb), tq), F32)],
        compiler_params=_params("parallel", "arbitrary"),
        name="dsa",
    )(to_lanes(q), to_lanes(iq), to_lanes(iw), k_hm, vt_blk, ik.astype(BF16))
    return out_t[:, :, :t].transpose(0, 2, 1)


def _ssm_discretize_body(a_re_ref, a_im_ref, log_dt_ref, ab_re_ref, ab_im_ref, bc_re_ref, bc_im_ref):
    a_re, a_im = a_re_ref[...], a_im_ref[...]
    dt = jnp.exp(log_dt_ref[...])
    mag = jnp.exp(dt * a_re)
    ab_re = mag * jnp.cos(dt * a_im)
    ab_im = mag * jnp.sin(dt * a_im)
    den = a_re * a_re + a_im * a_im
    nr = ab_re - 1.0
    ni = ab_im
    ab_re_ref[...] = ab_re
    ab_im_ref[...] = ab_im
    bc_re_ref[...] = (nr * a_re + ni * a_im) / den
    bc_im_ref[...] = (ni * a_re - nr * a_im) / den


def _ssm_discretize(a_re, a_im, log_dt):
    g, p = a_re.shape
    out = jax.ShapeDtypeStruct((g, p), F32)
    return pl.pallas_call(_ssm_discretize_body, out_shape=(out,) * 4, name="ssm_discretize")(
        a_re, a_im, log_dt.reshape(g, 1))


def _block_diag_halves(w, dtype):
    g, r, c = w.shape
    gh = g // 2
    eye = jnp.eye(gh, dtype=w.dtype)
    halves = [jnp.einsum("grc,gh->grhc", w[i * gh:(i + 1) * gh], eye).reshape(gh * r, gh * c) for i in range(2)]
    return jnp.stack(halves, 0).astype(dtype)


def _ssm_body(u_ref, h0r_ref, h0i_ref, abr_ref, abi_ref, bcr_ref, bci_ref, br_ref, bi_ref, cr_ref, ci_ref,
              d_ref, gw_ref, gb_ref, y_ref, hr_ref, hi_ref, xr_ref, xi_ref, *, tb, nb, precise):
    d_ssm = u_ref.shape[-1]
    n_state = hr_ref.shape[-1]
    kh, sh = d_ssm // 2, n_state // 2

    def mm(a, w):
        if precise:
            return jnp.dot(a, w, preferred_element_type=F32, precision=lax.Precision.HIGHEST)
        return jnp.dot(a.astype(BF16), w, preferred_element_type=F32)

    @pl.when(pl.program_id(0) == 0)
    def _():
        hr_ref[...] = h0r_ref[...]
        hi_ref[...] = h0i_ref[...]

    u = u_ref[...].reshape(tb * nb, d_ssm)

    for half in range(2):
        cols = slice(half * sh, (half + 1) * sh)
        uh = u[:, half * kh:(half + 1) * kh]
        bu_r = mm(uh, br_ref[half])
        bu_i = mm(uh, bi_ref[half])
        bc_r, bc_i = bcr_ref[:, cols], bci_ref[:, cols]
        xr_ref[:, cols] = bc_r * bu_r - bc_i * bu_i
        xi_ref[:, cols] = bc_r * bu_i + bc_i * bu_r

    slab = min(512, n_state)
    for s0 in range(0, n_state, slab):
        cols = slice(s0, s0 + slab)
        a_r = jnp.broadcast_to(abr_ref[:, cols], (nb, slab))
        a_i = jnp.broadcast_to(abi_ref[:, cols], (nb, slab))

        def step(t, h):
            h_r, h_i = h
            rows = pl.ds(pl.multiple_of(t * nb, nb), nb)
            n_r = a_r * h_r - a_i * h_i + xr_ref[rows, cols]
            n_i = a_r * h_i + a_i * h_r + xi_ref[rows, cols]
            xr_ref[rows, cols] = n_r
            xi_ref[rows, cols] = n_i
            return n_r, n_i

        h_r, h_i = lax.fori_loop(0, tb, step, (hr_ref[:, cols], hi_ref[:, cols]), unroll=min(8, tb))
        hr_ref[:, cols] = h_r
        hi_ref[:, cols] = h_i

    ys = []
    for half in range(2):
        cols = slice(half * sh, (half + 1) * sh)
        ys.append(mm(xr_ref[:, cols], cr_ref[half]) - mm(xi_ref[:, cols], ci_ref[half]))
    y = jnp.concatenate(ys, axis=1) + d_ref[...] * u
    y = y * (0.5 * (1.0 + jnp.tanh(math.sqrt(2.0 / math.pi) * (y + 0.044715 * (y * y * y)))))
    z = mm(y, gw_ref[...]) + gb_ref[...]
    y_ref[...] = (y * _sigmoid(z)).reshape(tb, nb, d_ssm)


def _ssm(u_tm, h0_re, h0_im, disc, b_re, b_im, c_re, c_im, d, glu_w, glu_b, *, precise):
    t, nb, d_ssm = u_tm.shape
    n_state = h0_re.shape[1]
    assert nb % 8 == 0
    tb = min(32, t)
    wdt = F32 if precise else BF16
    br = _block_diag_halves(jnp.swapaxes(b_re, 1, 2), wdt)
    bi = _block_diag_halves(jnp.swapaxes(b_im, 1, 2), wdt)
    cr = _block_diag_halves(jnp.swapaxes(c_re, 1, 2), wdt)
    ci = _block_diag_halves(jnp.swapaxes(c_im, 1, 2), wdt)
    row = lambda a: a.reshape(1, -1).astype(F32)
    state_spec = _const_spec((nb, n_state))
    vec_state = _const_spec((1, n_state))
    vec_ch = _const_spec((1, d_ssm))
    return pl.pallas_call(
        functools.partial(_ssm_body, tb=tb, nb=nb, precise=precise),
        out_shape=(jax.ShapeDtypeStruct((t, nb, d_ssm), F32),
                   jax.ShapeDtypeStruct((nb, n_state), F32), jax.ShapeDtypeStruct((nb, n_state), F32)),
        grid=(t // tb,),
        in_specs=[pl.BlockSpec((tb, nb, d_ssm), lambda i: (i, 0, 0)), state_spec, state_spec,
                  vec_state, vec_state, vec_state, vec_state,
                  _const_spec(br.shape), _const_spec(bi.shape), _const_spec(cr.shape), _const_spec(ci.shape),
                  vec_ch, _const_spec((d_ssm, d_ssm)), vec_ch],
        out_specs=(pl.BlockSpec((tb, nb, d_ssm), lambda i: (i, 0, 0)), state_spec, state_spec),
        scratch_shapes=[pltpu.VMEM((tb * nb, n_state), F32), pltpu.VMEM((tb * nb, n_state), F32)],
        compiler_params=_params("arbitrary"),
        name="ssm",
    )(u_tm, h0_re, h0_im, row(disc[0]), row(disc[1]), row(disc[2]), row(disc[3]), br, bi, cr, ci,
      row(d), glu_w.astype(wdt), row(glu_b))


def _merge_body(att_ref, ssm_ref, ga_ref, gb_ref, x_ref, wpa_ref, wpb_ref, wo_ref, o_ref):
    a = jnp.dot(att_ref[...], wpa_ref[...], preferred_element_type=F32)
    s = jnp.dot(ssm_ref[...].astype(BF16), wpb_ref[...], preferred_element_type=F32)
    mix = _sigmoid(ga_ref[...]) * a + _sigmoid(gb_ref[...]) * s
    o_ref[...] = x_ref[...] + jnp.dot(mix.astype(BF16), wo_ref[...], preferred_element_type=F32)


def _merge(att, ssm, ga, gb, x, w_pa, w_pb, w_o):
    b, t, d = x.shape
    tm = min(512, t)
    tok = lambda n: pl.BlockSpec((None, tm, n), lambda bi, i: (bi, i, 0))
    d_ssm = ssm.shape[-1]
    return pl.pallas_call(
        _merge_body,
        out_shape=jax.ShapeDtypeStruct((b, t, d), F32),
        grid=(b, t // tm),
        in_specs=[tok(D_ATTN), tok(d_ssm), tok(d), tok(d), tok(d),
                  _const_spec(w_pa.shape), _const_spec(w_pb.shape), _const_spec(w_o.shape)],
        out_specs=tok(d),
        compiler_params=_params("parallel", "parallel"),
        name="merge",
    )(att, ssm, ga, gb, x, w_pa.astype(BF16), w_pb.astype(BF16), w_o.astype(BF16))


def _ffn_body(x_ref, g_ref, wg_ref, wu_ref, wd_ref, o_ref, xn_ref, *, n_f):
    f = pl.program_id(1)

    @pl.when(f == 0)
    def _():
        xn_ref[...] = _rms(x_ref[...], g_ref[...]).astype(BF16)
        o_ref[...] = x_ref[...]

    xn = xn_ref[...]
    hg = jnp.dot(xn, wg_ref[...], preferred_element_type=F32)
    hu = jnp.dot(xn, wu_ref[...], preferred_element_type=F32)
    h = (hg * _sigmoid(hg)) * hu
    o_ref[...] += jnp.dot(h.astype(BF16), wd_ref[...], preferred_element_type=F32)


def _ffn(x2d, g, wg, wu, wd):
    n, d = x2d.shape
    d_ff = wg.shape[1]
    tm = min(512, n)
    fc = d_ff
    for cand in (1408, 1024, 896, 768, 512):
        if d_ff % cand == 0:
            fc = cand
            break
    n_f = d_ff // fc
    return pl.pallas_call(
        functools.partial(_ffn_body, n_f=n_f),
        out_shape=jax.ShapeDtypeStruct((n, d), F32),
        grid=(n // tm, n_f),
        in_specs=[pl.BlockSpec((tm, d), lambda i, f: (i, 0)), _const_spec((1, d)),
                  pl.BlockSpec((d, fc), lambda i, f: (0, f)), pl.BlockSpec((d, fc), lambda i, f: (0, f)),
                  pl.BlockSpec((fc, d), lambda i, f: (f, 0))],
        out_specs=pl.BlockSpec((tm, d), lambda i, f: (i, 0)),
        scratch_shapes=[pltpu.VMEM((tm, d), BF16)],
        compiler_params=_params("parallel", "arbitrary"),
        name="ffn",
    )(x2d, g.reshape(1, d), wg.astype(BF16), wu.astype(BF16), wd.astype(BF16))


def _moe_route_body(x_ref, g_ref, r_ref, xn_ref, gate_ref, gate_t_ref, rank_ref, rank_t_ref, *, n_exp, tm):
    xn = _rms(x_ref[...], g_ref[...])
    xn_ref[...] = xn.astype(BF16)
    logits = jnp.dot(xn, r_ref[...], preferred_element_type=F32, precision=lax.Precision.HIGHEST)
    lane = lax.broadcasted_iota(I32, logits.shape, 1).astype(F32)
    logits = jnp.where(lane < n_exp, logits, NEG_INF)
    big = float(logits.shape[1])
    m1 = jnp.max(logits, axis=1, keepdims=True)
    i1 = jnp.min(jnp.where(logits == m1, lane, big), axis=1, keepdims=True)
    rest = jnp.where(lane == i1, NEG_INF, logits)
    m2 = jnp.max(rest, axis=1, keepdims=True)
    i2 = jnp.min(jnp.where(rest == m2, lane, big), axis=1, keepdims=True)
    e2 = jnp.exp(m2 - m1)
    w1 = 1.0 / (1.0 + e2)
    w2 = e2 / (1.0 + e2)
    gates = jnp.where(lane == i1, w1, 0.0) + jnp.where(lane == i2, w2, 0.0)
    gates_t = gates.T
    gate_ref[...] = gates
    gate_t_ref[...] = gates_t
    t_row = lax.broadcasted_iota(I32, (tm, tm), 0)
    t_col = lax.broadcasted_iota(I32, (tm, tm), 1)
    member = jnp.where(gates > 0.0, 1.0, 0.0).astype(BF16)
    member_t = jnp.where(gates_t > 0.0, 1.0, 0.0).astype(BF16)
    rank_ref[...] = jnp.dot(jnp.where(t_col < t_row, 1.0, 0.0).astype(BF16), member, preferred_element_type=F32)
    rank_t_ref[...] = jnp.dot(member_t, jnp.where(t_row < t_col, 1.0, 0.0).astype(BF16),
                              preferred_element_type=F32)


def _moe_expert_body(acc_ref, xn_ref, gate_ref, gate_t_ref, rank_ref, rank_t_ref, wg_ref, wu_ref, wd_ref, o_ref, *,
                     fc, tm, rows_per_chunk, single_tile):
    e = pl.program_id(0)
    c = rows_per_chunk
    d_ff = wg_ref.shape[1]
    if single_tile:
        @pl.when(e == 0)
        def _():
            o_ref[...] = acc_ref[...]
    else:
        o_ref[...] = acc_ref[...]

    gate_row = gate_t_ref[pl.ds(e, 1), :]
    rank_row = rank_t_ref[pl.ds(e, 1), :]
    routed_row = gate_row > 0.0
    n_routed = jnp.sum(jnp.where(routed_row, 1.0, 0.0)).astype(I32)
    n_chunks = (n_routed + (c - 1)) // c

    lane = lax.broadcasted_iota(I32, gate_ref.shape, 1)
    rank_col = jnp.sum(jnp.where(lane == e, rank_ref[...], 0.0), axis=1, keepdims=True)
    routed_col = jnp.sum(jnp.where(lane == e, gate_ref[...], 0.0), axis=1, keepdims=True) > 0.0

    def chunk(ci, carry):
        slot_rows = (ci * c + lax.broadcasted_iota(I32, (c, tm), 0)).astype(F32)
        pack = routed_row & (rank_row == slot_rows)
        xs = jnp.dot(jnp.where(pack, 1.0, 0.0).astype(BF16), xn_ref[...], preferred_element_type=F32).astype(BF16)
        y = None
        for f0 in range(0, d_ff, fc):
            hg = jnp.dot(xs, wg_ref[:, f0:f0 + fc], preferred_element_type=F32)
            hu = jnp.dot(xs, wu_ref[:, f0:f0 + fc], preferred_element_type=F32)
            h = ((hg * _sigmoid(hg)) * hu).astype(BF16)
            part = jnp.dot(h, wd_ref[f0:f0 + fc, :], preferred_element_type=F32)
            y = part if y is None else y + part
        y = y * jnp.sum(jnp.where(pack, gate_row, 0.0), axis=1, keepdims=True)
        y_hi = y.astype(BF16)
        y_lo = (y - y_hi.astype(F32)).astype(BF16)
        slot_cols = (ci * c + lax.broadcasted_iota(I32, (tm, c), 1)).astype(F32)
        unpack = jnp.where(routed_col & (rank_col == slot_cols), 1.0, 0.0).astype(BF16)
        o_ref[...] += jnp.dot(jnp.concatenate([unpack, unpack], axis=1), jnp.concatenate([y_hi, y_lo], axis=0),
                              preferred_element_type=F32)
        return carry

    lax.fori_loop(0, n_chunks, chunk, 0)


def _moe(x2d, g, router, wg, wu, wd):
    n, d = x2d.shape
    n_exp, _, d_ff = wg.shape
    assert TOP_K_EXPERTS == 2 and n_exp <= V7X_LANES
    tm = min(1024, n)
    rows_per_chunk = min(V7X_LANES, tm)
    assert n % tm == 0 and tm % V7X_LANES == 0
    n_tiles = n // tm
    assert n_tiles == 1 or n_tiles >= 4
    fc = d_ff
    for cand in (896, 1024, 768, 512):
        if d_ff % cand == 0:
            fc = cand
            break
    r_pad = jnp.concatenate([router, jnp.zeros((d, V7X_LANES - n_exp), router.dtype)], axis=1)
    tok = lambda width: pl.BlockSpec((tm, width), lambda i: (i, 0))
    per_tile = pl.BlockSpec((None, V7X_LANES, tm), lambda i: (i, 0, 0))
    xn, gate, gate_t, rank, rank_t = pl.pallas_call(
        functools.partial(_moe_route_body, n_exp=n_exp, tm=tm),
        out_shape=(jax.ShapeDtypeStruct((n, d), BF16),
                   jax.ShapeDtypeStruct((n, V7X_LANES), F32), jax.ShapeDtypeStruct((n_tiles, V7X_LANES, tm), F32),
                   jax.ShapeDtypeStruct((n, V7X_LANES), F32), jax.ShapeDtypeStruct((n_tiles, V7X_LANES, tm), F32)),
        grid=(n_tiles,),
        in_specs=[tok(d), _const_spec((1, d)), _const_spec((d, V7X_LANES))],
        out_specs=(tok(d), tok(V7X_LANES), per_tile, tok(V7X_LANES), per_tile),
        compiler_params=_params("parallel"),
        name="moe_route",
    )(x2d, g.reshape(1, d), r_pad)

    tok2 = lambda width: pl.BlockSpec((tm, width), lambda e, i: (i, 0))
    per_tile2 = pl.BlockSpec((None, V7X_LANES, tm), lambda e, i: (i, 0, 0))
    w_spec = lambda rows, cols: pl.BlockSpec((None, rows, cols), lambda e, i: (e, 0, 0), pipeline_mode=pl.Buffered(1))
    return pl.pallas_call(
        functools.partial(_moe_expert_body, fc=fc, tm=tm, rows_per_chunk=rows_per_chunk, single_tile=n_tiles == 1),
        out_shape=jax.ShapeDtypeStruct((n, d), F32),
        grid=(n_exp, n_tiles),
        in_specs=[tok2(d), tok2(d), tok2(V7X_LANES), per_tile2, tok2(V7X_LANES), per_tile2,
                  w_spec(d, d_ff), w_spec(d, d_ff), w_spec(d_ff, d)],
        out_specs=tok2(d),
        input_output_aliases={0: 0},
        compiler_params=_params("arbitrary", "arbitrary"),
        name="moe_experts",
    )(x2d, xn, gate, gate_t, rank, rank_t, wg.astype(BF16), wu.astype(BF16), wd.astype(BF16))


def _final_norm_body(x_ref, g_ref, o_ref):
    o_ref[...] = _rms(x_ref[...], g_ref[...])


def _final_norm(x2d, g):
    n, d = x2d.shape
    tm = min(1024, n)
    return pl.pallas_call(
        _final_norm_body,
        out_shape=jax.ShapeDtypeStruct((n, d), F32),
        grid=(n // tm,),
        in_specs=[pl.BlockSpec((tm, d), lambda i: (i, 0)), _const_spec((1, d))],
        out_specs=pl.BlockSpec((tm, d), lambda i: (i, 0)),
        compiler_params=_params("parallel"),
        name="final_norm",
    )(x2d, g.reshape(1, d))


def _layer(x, past, h0, lw, *, layer_idx, precise_ssm):
    b, t, d = x.shape
    d_ssm = lw["ssm_d"].shape[0]
    q, k, v, iq, ik, iw, u, ga, gb = _inproj(x, lw["norm_mix_g"], lw["w_in_packed"], d_ssm)

    if past is None:
        kb = min(512, t)
        k_all, v_all, ik_all, s_valid = k, v, ik, t
    else:
        ck, cv, cik = past
        k_all = jnp.concatenate([ck.reshape(b, -1, D_ATTN), k], axis=1)
        v_all = jnp.concatenate([cv.reshape(b, -1, D_ATTN), v], axis=1)
        ik_all = jnp.concatenate([cik, ik], axis=1)
        s_valid = k_all.shape[1]
        kb = 512
        pad = (-s_valid) % kb
        padf = lambda a: jnp.pad(a, ((0, 0), (0, pad), (0, 0)))
        k_all, v_all, ik_all = padf(k_all), padf(v_all), padf(ik_all)
    att = _dsa(q, iq, iw, k_all, v_all, ik_all, n_keep=min(TOPK_MAX, s_valid // 4), causal=past is None,
               n_valid=s_valid, kb=kb)

    ssm_tm, h_re, h_im = _ssm(u.transpose(1, 0, 2), h0[0], h0[1], lw["disc"], lw["ssm_b_re"], lw["ssm_b_im"], lw["ssm_c_re"],
                              lw["ssm_c_im"], lw["ssm_d"], lw["glu_w"], lw["glu_b"], precise=precise_ssm)
    x = _merge(att, ssm_tm.transpose(1, 0, 2), ga, gb, x, lw["w_branch_attn"], lw["w_branch_ssm"], lw["w_out"])
    x2d = x.reshape(b * t, d)
    if layer_idx % 2 == 0:
        x2d = _ffn(x2d, lw["norm_ffn_g"], lw["ffn_w_gate"], lw["ffn_w_up"], lw["ffn_w_down"])
    else:
        x2d = _moe(x2d, lw["norm_ffn_g"], lw["moe_router"], lw["moe_w_gate"], lw["moe_w_up"], lw["moe_w_down"])
    g_shape = (b, -1, SSM_STATE)
    return (x2d.reshape(b, t, d), k.reshape(b, t, N_HEADS, HEAD_DIM), v.reshape(b, t, N_HEADS, HEAD_DIM), ik,
            h_re.reshape(g_shape), h_im.reshape(g_shape))


def kernel(x_prompt, x_sample, cache_k, cache_v, cache_idx_k, state_ssm_re, state_ssm_im, norm_mix_g, w_in,
           ssm_a_re, ssm_a_im, ssm_log_dt, ssm_b_re, ssm_b_im, ssm_c_re, ssm_c_im, ssm_d, glu_w, glu_b,
           w_branch_attn, w_branch_ssm, w_out, norm_ffn_g, ffn_w_gate, ffn_w_up, ffn_w_down, moe_router,
           moe_w_gate, moe_w_up, moe_w_down, final_norm_g):
    depth = w_in.shape[0]
    d_model = x_prompt.shape[-1]
    d_ssm = ssm_d.shape[1]
    xp, xs = x_prompt, x_sample
    outs_p, outs_s = [], []
    for l in range(depth):
        lw = dict(
            norm_mix_g=norm_mix_g[l], w_in_packed=_pack_w_in(w_in[l], d_ssm, d_model),
            disc=_ssm_discretize(ssm_a_re[l], ssm_a_im[l], ssm_log_dt[l]),
            ssm_b_re=ssm_b_re[l], ssm_b_im=ssm_b_im[l], ssm_c_re=ssm_c_re[l], ssm_c_im=ssm_c_im[l],
            ssm_d=ssm_d[l], glu_w=glu_w[l], glu_b=glu_b[l],
            w_branch_attn=w_branch_attn[l], w_branch_ssm=w_branch_ssm[l], w_out=w_out[l], norm_ffn_g=norm_ffn_g[l])
        i = l // 2
        if l % 2 == 0:
            lw.update(ffn_w_gate=ffn_w_gate[i], ffn_w_up=ffn_w_up[i], ffn_w_down=ffn_w_down[i])
        else:
            lw.update(moe_router=moe_router[i], moe_w_gate=moe_w_gate[i], moe_w_up=moe_w_up[i],
                      moe_w_down=moe_w_down[i])
        n_state = ssm_a_re.shape[1] * ssm_a_re.shape[2]
        zeros = jnp.zeros((xp.shape[0], n_state), F32)
        xp, *rest_p = _layer(xp, None, (zeros, zeros), lw, layer_idx=l, precise_ssm=False)
        outs_p.append(rest_p)
        h0 = (state_ssm_re[l].reshape(xs.shape[0], n_state), state_ssm_im[l].reshape(xs.shape[0], n_state))
        xs, *rest_s = _layer(xs, (cache_k[l], cache_v[l], cache_idx_k[l]), h0, lw, layer_idx=l, precise_ssm=True)
        outs_s.append(rest_s)

    y_prompt = _final_norm(xp.reshape(-1, d_model), final_norm_g).reshape(xp.shape)
    y_sample = _final_norm(xs.reshape(-1, d_model), final_norm_g).reshape(xs.shape)
    stack = lambda outs, j: jnp.stack([o[j] for o in outs], 0)
    return (y_prompt, y_sample,
            stack(outs_p, 0), stack(outs_p, 1), stack(outs_p, 2), stack(outs_p, 3), stack(outs_p, 4),
            stack(outs_s, 0), stack(outs_s, 1), stack(outs_s, 2), stack(outs_s, 3), stack(outs_s, 4))
```

```python
import functools
import math

import jax
import jax.numpy as jnp
from jax import lax
from jax.experimental import pallas as pl
from jax.experimental.pallas import tpu as pltpu

F32 = jnp.float32
BF16 = jnp.bfloat16
I32 = jnp.int32

CHUNK = 64
N_HEADS = 8
HEAD_DIM = 64
N_IDX_HEADS = 8
IDX_DIM = 64
TOPK_MAX = 256
SSM_GROUP = 16
SSM_STATE = 64
TOP_K_EXPERTS = 2
RMS_EPS = 1e-6
NEG_INF = -1e30
INT32_MIN = -(2 ** 31)

D_ATTN = N_HEADS * HEAD_DIM
D_IDX = N_IDX_HEADS * IDX_DIM
IDX_W_SCALE = float(D_IDX) ** -0.5
ATTN_SCALE = float(HEAD_DIM) ** -0.5

V7X_LANES = 128
V7X_BF16_SUBLANES = 16
V7X_VMEM_LIMIT_BYTES = 56 * 1024 * 1024


def _params(*semantics):
    return pltpu.CompilerParams(dimension_semantics=semantics, vmem_limit_bytes=V7X_VMEM_LIMIT_BYTES)


def _sigmoid(x):
    return 1.0 / (1.0 + jnp.exp(-x))


def _rms(x, g):
    return (x * lax.rsqrt(jnp.mean(x * x, axis=-1, keepdims=True) + RMS_EPS)) * g


def _const_spec(shape):
    return pl.BlockSpec(shape, lambda *_: (0,) * len(shape))


_SEG_Q, _SEG_K, _SEG_V, _SEG_IQ = 0, D_ATTN, 2 * D_ATTN, 3 * D_ATTN
_SEG_SMALL = 3 * D_ATTN + D_IDX


def _pack_w_in(w_in, d_ssm, d_model):
    offs = [0]
    for s in (D_ATTN, D_ATTN, D_ATTN, D_IDX, IDX_DIM, N_IDX_HEADS, d_ssm, d_model, d_model):
        offs.append(offs[-1] + s)
    pad = V7X_LANES - IDX_DIM - N_IDX_HEADS
    small = jnp.concatenate([w_in[:, offs[4]:offs[6]], jnp.zeros((w_in.shape[0], pad), w_in.dtype)], axis=1)
    return jnp.concatenate([w_in[:, :offs[4]], small, w_in[:, offs[6]:]], axis=1).astype(BF16)


def _inproj_body(x_ref, g_ref, w_ref, q_ref, k_ref, v_ref, iq_ref, ik_ref, iw_ref, u_ref, ga_ref, gb_ref, *, d_ssm, d_model):
    hb = _rms(x_ref[...], g_ref[...]).astype(BF16)

    def proj(lo, n):
        return jnp.dot(hb, w_ref[:, lo:lo + n], preferred_element_type=F32)

    q_ref[...] = proj(_SEG_Q, D_ATTN).astype(BF16)
    k_ref[...] = proj(_SEG_K, D_ATTN)
    v_ref[...] = proj(_SEG_V, D_ATTN)
    iq_ref[...] = proj(_SEG_IQ, D_IDX).astype(BF16)
    small = proj(_SEG_SMALL, V7X_LANES)
    ik_ref[...] = small[:, :IDX_DIM]
    iw_ref[...] = small[:, IDX_DIM:IDX_DIM + N_IDX_HEADS] * IDX_W_SCALE
    seg_u = _SEG_SMALL + V7X_LANES
    u_ref[...] = proj(seg_u, d_ssm)
    ga_ref[...] = proj(seg_u + d_ssm, d_model)
    gb_ref[...] = proj(seg_u + d_ssm + d_model, d_model)


def _inproj(x, g, w_packed, d_ssm):
    b, t, d = x.shape
    tm = min(256, t)
    n_w = w_packed.shape[1]

    def tok(n, dtype):
        return jax.ShapeDtypeStruct((b, t, n), dtype), pl.BlockSpec((None, tm, n), lambda bi, i: (bi, i, 0))

    shapes, specs = zip(
        tok(D_ATTN, BF16), tok(D_ATTN, F32), tok(D_ATTN, F32), tok(D_IDX, BF16), tok(IDX_DIM, F32),
        tok(N_IDX_HEADS, F32), tok(d_ssm, F32), tok(d, F32), tok(d, F32))
    return pl.pallas_call(
        functools.partial(_inproj_body, d_ssm=d_ssm, d_model=d),
        out_shape=shapes,
        grid=(b, t // tm),
        in_specs=[pl.BlockSpec((None, tm, d), lambda bi, i: (bi, i, 0)), _const_spec((1, d)), _const_spec((d, n_w))],
        out_specs=specs,
        compiler_params=_params("parallel", "parallel"),
        name="inproj",
    )(x, g.reshape(1, d), w_packed)


def _dsa_body(qt_ref, iqt_ref, iwt_ref, k_ref, vt_ref, ik_ref, o_ref, key_ref, qs_ref, acc_ref, m_ref, s_ref, s2_ref, *,
              tq, kb, n_keep, causal, n_valid, n_kb_total, pos_bits):
    col0 = pl.program_id(1) * tq
    if causal:
        n_kb = (col0 + tq + kb - 1) // kb
        q_pos = col0 + lax.broadcasted_iota(I32, (1, tq), 1)
        shift = CHUNK.bit_length() - 1
        q_limit = ((q_pos >> shift) + 1) << shift
    else:
        n_kb = n_kb_total
        q_limit = jnp.full((1, tq), n_valid, I32)

    ks = min(V7X_LANES, kb)
    cr = min(64, kb)

    def key_pos(j, r0, rows):
        return j * kb + r0 + lax.broadcasted_iota(I32, (rows, tq), 0)

    iwt = iwt_ref[...]

    def idx_block(j, carry):
        for r0 in range(0, kb, ks):
            ik_tile = ik_ref[pl.ds(pl.multiple_of(j * kb + r0, ks), ks), :]
            s = None
            for h in range(N_IDX_HEADS):
                d = jnp.dot(ik_tile, iqt_ref[h * IDX_DIM:(h + 1) * IDX_DIM, :], preferred_element_type=F32)
                term = iwt[h:h + 1, :] * jnp.maximum(d, 0.0)
                s = term if s is None else s + term
            s = jnp.where(s == 0.0, 0.0, s)
            s = jnp.where(key_pos(j, r0, ks) < q_limit, s, NEG_INF)
            bits = lax.bitcast_convert_type(s, I32)
            key_ref[j, r0:r0 + ks, :] = jnp.where(bits < 0, bits ^ jnp.int32(0x7FFFFFFF), bits)
        return carry

    lax.fori_loop(0, n_kb, idx_block, 0)

    def count(pred):
        def body(j, accs):
            accs = list(accs)
            for n, r0 in enumerate(range(0, kb, cr)):
                hit = pred(key_ref[j, r0:r0 + cr, :], functools.partial(key_pos, j, r0, cr))
                accs[n % 2] = accs[n % 2] + jnp.where(hit, 1.0, 0.0)
            return tuple(accs)
        zero = jnp.zeros((cr, tq), F32)
        acc_a, acc_b = lax.fori_loop(0, n_kb, body, (zero, zero))
        return jnp.sum(acc_a + acc_b, axis=0, keepdims=True)

    def bit_step(i, thr):
        cand = thr + lax.shift_left(jnp.int32(1), 31 - i)
        cnt = count(lambda key, pos: key >= cand)
        return jnp.where(cnt >= n_keep, cand, thr)

    thr = lax.fori_loop(0, 32, bit_step, jnp.full((1, tq), INT32_MIN, I32))

    need = n_keep - count(lambda key, pos: key > thr)
    n_eq = count(lambda key, pos: key == thr)
    pos_all = jnp.int32(2 ** pos_bits - 1)

    def tie_search():
        def step(i, end):
            cand = end + lax.shift_left(jnp.int32(1), pos_bits - 1 - i)
            cnt = count(lambda key, pos: (key == thr) & (pos() < cand))
            return jnp.where(cnt <= need, cand, end)
        return lax.fori_loop(0, pos_bits, step, jnp.zeros((1, tq), I32))

    tie_end = lax.cond(jnp.max(n_eq - need) > 0.0, tie_search, lambda: jnp.full((1, tq), pos_all, I32))

    qs_ref[...] = qt_ref[...] * jnp.asarray(ATTN_SCALE, BF16)
    m_ref[...] = jnp.full(m_ref.shape, NEG_INF, F32)
    acc_ref[...] = jnp.zeros(acc_ref.shape, F32)

    def bias_block(j, carry):
        for r0 in range(0, kb, cr):
            key = key_ref[j, r0:r0 + cr, :]
            pos = key_pos(j, r0, cr)
            keep = (pos < q_limit) & ((key > thr) | ((key == thr) & (pos < tie_end)))
            key_ref[j, r0:r0 + cr, :] = lax.bitcast_convert_type(jnp.where(keep, 0.0, NEG_INF), I32)
        return carry

    lax.fori_loop(0, n_kb, bias_block, 0)

    v_rows = acc_ref.shape[1]

    def issue_scores(j, r0, buf):
        k_rows = pl.ds(pl.multiple_of(j * kb + r0, ks), ks)
        for h in range(N_HEADS):
            buf[h] = jnp.dot(k_ref[h, k_rows, :], qs_ref[h * HEAD_DIM:(h + 1) * HEAD_DIM, :],
                             preferred_element_type=F32)

    def att_block(j, carry):
        tiles = list(range(0, kb, ks))
        bufs = [s_ref, s2_ref]
        issue_scores(j, tiles[0], bufs[0])
        for i, r0 in enumerate(tiles):
            if i + 1 < len(tiles):
                issue_scores(j, tiles[i + 1], bufs[(i + 1) % 2])
            bias = lax.bitcast_convert_type(key_ref[j, r0:r0 + ks, :], F32)
            for h in range(N_HEADS):
                s = bufs[i % 2][h] + bias
                m_old = m_ref[h]
                m_new = jnp.maximum(m_old, jnp.max(s, axis=0, keepdims=True))
                alpha = jnp.exp(m_old - m_new)
                p = jnp.exp((s.reshape(ks // 8, 8, tq) - m_new[None]).reshape(ks, tq).astype(BF16))
                pv = jnp.dot(vt_ref[j, h, :, r0:r0 + ks], p, preferred_element_type=F32)
                acc = acc_ref[h].reshape(v_rows // 8, 8, tq) * alpha[None]
                acc_ref[h] = acc.reshape(v_rows, tq) + pv
                m_ref[h] = m_new
        return carry

    lax.fori_loop(0, n_kb, att_block, 0)
    for h in range(N_HEADS):
        rows = slice(h * HEAD_DIM, (h + 1) * HEAD_DIM)
        o_ref[rows, :] = (acc_ref[h, :HEAD_DIM, :] / acc_ref[h, HEAD_DIM:HEAD_DIM + 1, :]).astype(o_ref.dtype)


def _dsa(q, iq, iw, k, v, ik, *, n_keep, causal, n_valid, kb):
    b, t, _ = q.shape
    s_pad = k.shape[1]
    n_kb_total = s_pad // kb
    t_pad = -(-t // V7X_LANES) * V7X_LANES
    tq = min(256, t_pad)
    assert t_pad % tq == 0 and (not causal or (t_pad == t and tq % CHUNK == 0))
    to_lanes = lambda a: jnp.pad(a.transpose(0, 2, 1), ((0, 0), (0, 0), (0, t_pad - t)))
    k_hm = k.astype(BF16).reshape(b, s_pad, N_HEADS, HEAD_DIM).transpose(0, 2, 1, 3)
    vt_blk = v.astype(BF16).reshape(b, n_kb_total, kb, N_HEADS, HEAD_DIM).transpose(0, 1, 3, 4, 2)
    vt_blk = jnp.concatenate([vt_blk, jnp.ones((b, n_kb_total, N_HEADS, V7X_BF16_SUBLANES, kb), BF16)], axis=3)
    v_rows = HEAD_DIM + V7X_BF16_SUBLANES
    body = functools.partial(_dsa_body, tq=tq, kb=kb, n_keep=float(n_keep), causal=causal, n_valid=n_valid,
                             n_kb_total=n_kb_total, pos_bits=s_pad.bit_length())
    qry = lambda n: pl.BlockSpec((None, n, tq), lambda bi, i: (bi, 0, i))
    out_t = pl.pallas_call(
        body,
        out_shape=jax.ShapeDtypeStruct((b, D_ATTN, t_pad), BF16),
        grid=(b, t_pad // tq),
        in_specs=[qry(D_ATTN), qry(D_IDX), qry(N_IDX_HEADS),
                  pl.BlockSpec((None, N_HEADS, s_pad, HEAD_DIM), lambda bi, i: (bi, 0, 0, 0)),
                  pl.BlockSpec((None, n_kb_total, N_HEADS, v_rows, kb), lambda bi, i: (bi, 0, 0, 0, 0)),
                  pl.BlockSpec((None, s_pad, IDX_DIM), lambda bi, i: (bi, 0, 0))],
        out_specs=qry(D_ATTN),
        scratch_shapes=[pltpu.VMEM((n_kb_total, kb, tq), I32),
                        pltpu.VMEM((D_ATTN, tq), BF16),
                        pltpu.VMEM((N_HEADS, v_rows, tq), F32),
                        pltpu.VMEM((N_HEADS, 8, tq), F32),
                        pltpu.VMEM((N_HEADS, min(V7X_LANES, kb), tq), F32),
                        pltpu.VMEM((N_HEADS, min(V7X_LANES, kb), tq), F32)],
        compiler_params=_params("parallel", "arbitrary"),
        name="dsa",
    )(to_lanes(q), to_lanes(iq), to_lanes(iw), k_hm, vt_blk, ik.astype(BF16))
    return out_t[:, :, :t].transpose(0, 2, 1)


def _ssm_discretize_body(a_re_ref, a_im_ref, log_dt_ref, ab_re_ref, ab_im_ref, bc_re_ref, bc_im_ref):
    a_re, a_im = a_re_ref[...], a_im_ref[...]
    dt = jnp.exp(log_dt_ref[...])
    mag = jnp.exp(dt * a_re)
    ab_re = mag * jnp.cos(dt * a_im)
    ab_im = mag * jnp.sin(dt * a_im)
    den = a_re * a_re + a_im * a_im
    nr = ab_re - 1.0
    ni = ab_im
    ab_re_ref[...] = ab_re
    ab_im_ref[...] = ab_im
    bc_re_ref[...] = (nr * a_re + ni * a_im) / den
    bc_im_ref[...] = (ni * a_re - nr * a_im) / den


def _ssm_discretize(a_re, a_im, log_dt):
    g, p = a_re.shape
    out = jax.ShapeDtypeStruct((g, p), F32)
    return pl.pallas_call(_ssm_discretize_body, out_shape=(out,) * 4, name="ssm_discretize")(
        a_re, a_im, log_dt.reshape(g, 1))


def _block_diag_halves(w, dtype):
    g, r, c = w.shape
    gh = g // 2
    eye = jnp.eye(gh, dtype=w.dtype)
    halves = [jnp.einsum("grc,gh->grhc", w[i * gh:(i + 1) * gh], eye).reshape(gh * r, gh * c) for i in range(2)]
    return jnp.stack(halves, 0).astype(dtype)


def _ssm_body(u_ref, h0r_ref, h0i_ref, abr_ref, abi_ref, bcr_ref, bci_ref, br_ref, bi_ref, cr_ref, ci_ref,
              d_ref, gw_ref, gb_ref, y_ref, hr_ref, hi_ref, xr_ref, xi_ref, *, tb, nb, precise):
    d_ssm = u_ref.shape[-1]
    n_state = hr_ref.shape[-1]
    kh, sh = d_ssm // 2, n_state // 2

    def mm(a, w):
        if precise:
            return jnp.dot(a, w, preferred_element_type=F32, precision=lax.Precision.HIGHEST)
        return jnp.dot(a.astype(BF16), w, preferred_element_type=F32)

    @pl.when(pl.program_id(0) == 0)
    def _():
        hr_ref[...] = h0r_ref[...]
        hi_ref[...] = h0i_ref[...]

    u = u_ref[...].reshape(tb * nb, d_ssm)

    for half in range(2):
        cols = slice(half * sh, (half + 1) * sh)
        uh = u[:, half * kh:(half + 1) * kh]
        bu_r = mm(uh, br_ref[half])
        bu_i = mm(uh, bi_ref[half])
        bc_r, bc_i = bcr_ref[:, cols], bci_ref[:, cols]
        xr_ref[:, cols] = bc_r * bu_r - bc_i * bu_i
        xi_ref[:, cols] = bc_r * bu_i + bc_i * bu_r

    slab = min(512, n_state)
    for s0 in range(0, n_state, slab):
        cols = slice(s0, s0 + slab)
        a_r = jnp.broadcast_to(abr_ref[:, cols], (nb, slab))
        a_i = jnp.broadcast_to(abi_ref[:, cols], (nb, slab))

        def step(t, h):
            h_r, h_i = h
            rows = pl.ds(pl.multiple_of(t * nb, nb), nb)
            n_r = a_r * h_r - a_i * h_i + xr_ref[rows, cols]
            n_i = a_r * h_i + a_i * h_r + xi_ref[rows, cols]
            xr_ref[rows, cols] = n_r
            xi_ref[rows, cols] = n_i
            return n_r, n_i

        h_r, h_i = lax.fori_loop(0, tb, step, (hr_ref[:, cols], hi_ref[:, cols]), unroll=min(8, tb))
        hr_ref[:, cols] = h_r
        hi_ref[:, cols] = h_i

    ys = []
    for half in range(2):
        cols = slice(half * sh, (half + 1) * sh)
        ys.append(mm(xr_ref[:, cols], cr_ref[half]) - mm(xi_ref[:, cols], ci_ref[half]))
    y = jnp.concatenate(ys, axis=1) + d_ref[...] * u
    y = y * (0.5 * (1.0 + jnp.tanh(math.sqrt(2.0 / math.pi) * (y + 0.044715 * (y * y * y)))))
    z = mm(y, gw_ref[...]) + gb_ref[...]
    y_ref[...] = (y * _sigmoid(z)).reshape(tb, nb, d_ssm)


def _ssm(u_tm, h0_re, h0_im, disc, b_re, b_im, c_re, c_im, d, glu_w, glu_b, *, precise):
    t, nb, d_ssm = u_tm.shape
    n_state = h0_re.shape[1]
    assert nb % 8 == 0
    tb = min(32, t)
    wdt = F32 if precise else BF16
    br = _block_diag_halves(jnp.swapaxes(b_re, 1, 2), wdt)
    bi = _block_diag_halves(jnp.swapaxes(b_im, 1, 2), wdt)
    cr = _block_diag_halves(jnp.swapaxes(c_re, 1, 2), wdt)
    ci = _block_diag_halves(jnp.swapaxes(c_im, 1, 2), wdt)
    row = lambda a: a.reshape(1, -1).astype(F32)
    state_spec = _const_spec((nb, n_state))
    vec_state = _const_spec((1, n_state))
    vec_ch = _const_spec((1, d_ssm))
    return pl.pallas_call(
        functools.partial(_ssm_body, tb=tb, nb=nb, precise=precise),
        out_shape=(jax.ShapeDtypeStruct((t, nb, d_ssm), F32),
                   jax.ShapeDtypeStruct((nb, n_state), F32), jax.ShapeDtypeStruct((nb, n_state), F32)),
        grid=(t // tb,),
        in_specs=[pl.BlockSpec((tb, nb, d_ssm), lambda i: (i, 0, 0)), state_spec, state_spec,
                  vec_state, vec_state, vec_state, vec_state,
                  _const_spec(br.shape), _const_spec(bi.shape), _const_spec(cr.shape), _const_spec(ci.shape),
                  vec_ch, _const_spec((d_ssm, d_ssm)), vec_ch],
        out_specs=(pl.BlockSpec((tb, nb, d_ssm), lambda i: (i, 0, 0)), state_spec, state_spec),
        scratch_shapes=[pltpu.VMEM((tb * nb, n_state), F32), pltpu.VMEM((tb * nb, n_state), F32)],
        compiler_params=_params("arbitrary"),
        name="ssm",
    )(u_tm, h0_re, h0_im, row(disc[0]), row(disc[1]), row(disc[2]), row(disc[3]), br, bi, cr, ci,
      row(d), glu_w.astype(wdt), row(glu_b))


def _merge_body(att_ref, ssm_ref, ga_ref, gb_ref, x_ref, wpa_ref, wpb_ref, wo_ref, o_ref):
    a = jnp.dot(att_ref[...], wpa_ref[...], preferred_element_type=F32)
    s = jnp.dot(ssm_ref[...].astype(BF16), wpb_ref[...], preferred_element_type=F32)
    mix = _sigmoid(ga_ref[...]) * a + _sigmoid(gb_ref[...]) * s
    o_ref[...] = x_ref[...] + jnp.dot(mix.astype(BF16), wo_ref[...], preferred_element_type=F32)


def _merge(att, ssm, ga, gb, x, w_pa, w_pb, w_o):
    b, t, d = x.shape
    tm = min(512, t)
    tok = lambda n: pl.BlockSpec((None, tm, n), lambda bi, i: (bi, i, 0))
    d_ssm = ssm.shape[-1]
    return pl.pallas_call(
        _merge_body,
        out_shape=jax.ShapeDtypeStruct((b, t, d), F32),
        grid=(b, t // tm),
        in_specs=[tok(D_ATTN), tok(d_ssm), tok(d), tok(d), tok(d),
                  _const_spec(w_pa.shape), _const_spec(w_pb.shape), _const_spec(w_o.shape)],
        out_specs=tok(d),
        compiler_params=_params("parallel", "parallel"),
        name="merge",
    )(att, ssm, ga, gb, x, w_pa.astype(BF16), w_pb.astype(BF16), w_o.astype(BF16))


def _ffn_body(x_ref, g_ref, wg_ref, wu_ref, wd_ref, o_ref, xn_ref, *, n_f):
    f = pl.program_id(1)

    @pl.when(f == 0)
    def _():
        xn_ref[...] = _rms(x_ref[...], g_ref[...]).astype(BF16)
        o_ref[...] = x_ref[...]

    xn = xn_ref[...]
    hg = jnp.dot(xn, wg_ref[...], preferred_element_type=F32)
    hu = jnp.dot(xn, wu_ref[...], preferred_element_type=F32)
    h = (hg * _sigmoid(hg)) * hu
    o_ref[...] += jnp.dot(h.astype(BF16), wd_ref[...], preferred_element_type=F32)


def _ffn(x2d, g, wg, wu, wd):
    n, d = x2d.shape
    d_ff = wg.shape[1]
    tm = min(512, n)
    fc = d_ff
    for cand in (1408, 1024, 896, 768, 512):
        if d_ff % cand == 0:
            fc = cand
            break
    n_f = d_ff // fc
    return pl.pallas_call(
        functools.partial(_ffn_body, n_f=n_f),
        out_shape=jax.ShapeDtypeStruct((n, d), F32),
        grid=(n // tm, n_f),
        in_specs=[pl.BlockSpec((tm, d), lambda i, f: (i, 0)), _const_spec((1, d)),
                  pl.BlockSpec((d, fc), lambda i, f: (0, f)), pl.BlockSpec((d, fc), lambda i, f: (0, f)),
                  pl.BlockSpec((fc, d), lambda i, f: (f, 0))],
        out_specs=pl.BlockSpec((tm, d), lambda i, f: (i, 0)),
        scratch_shapes=[pltpu.VMEM((tm, d), BF16)],
        compiler_params=_params("parallel", "arbitrary"),
        name="ffn",
    )(x2d, g.reshape(1, d), wg.astype(BF16), wu.astype(BF16), wd.astype(BF16))


def _moe_route_body(x_ref, g_ref, r_ref, xn_ref, gate_ref, gate_t_ref, rank_ref, rank_t_ref, *, n_exp, tm):
    xn = _rms(x_ref[...], g_ref[...])
    xn_ref[...] = xn.astype(BF16)
    logits = jnp.dot(xn, r_ref[...], preferred_element_type=F32, precision=lax.Precision.HIGHEST)
    lane = lax.broadcasted_iota(I32, logits.shape, 1).astype(F32)
    logits = jnp.where(lane < n_exp, logits, NEG_INF)
    big = float(logits.shape[1])
    m1 = jnp.max(logits, axis=1, keepdims=True)
    i1 = jnp.min(jnp.where(logits == m1, lane, big), axis=1, keepdims=True)
    rest = jnp.where(lane == i1, NEG_INF, logits)
    m2 = jnp.max(rest, axis=1, keepdims=True)
    i2 = jnp.min(jnp.where(rest == m2, lane, big), axis=1, keepdims=True)
    e2 = jnp.exp(m2 - m1)
    w1 = 1.0 / (1.0 + e2)
    w2 = e2 / (1.0 + e2)
    gates = jnp.where(lane == i1, w1, 0.0) + jnp.where(lane == i2, w2, 0.0)
    gates_t = gates.T
    gate_ref[...] = gates
    gate_t_ref[...] = gates_t
    t_row = lax.broadcasted_iota(I32, (tm, tm), 0)
    t_col = lax.broadcasted_iota(I32, (tm, tm), 1)
    member = jnp.where(gates > 0.0, 1.0, 0.0).astype(BF16)
    member_t = jnp.where(gates_t > 0.0, 1.0, 0.0).astype(BF16)
    rank_ref[...] = jnp.dot(jnp.where(t_col < t_row, 1.0, 0.0).astype(BF16), member, preferred_element_type=F32)
    rank_t_ref[...] = jnp.dot(member_t, jnp.where(t_row < t_col, 1.0, 0.0).astype(BF16),
                              preferred_element_type=F32)


def _moe_expert_body(acc_ref, xn_ref, gate_ref, gate_t_ref, rank_ref, rank_t_ref, wg_ref, wu_ref, wd_ref, o_ref, *,
                     fc, tm, rows_per_chunk, single_tile):
    e = pl.program_id(0)
    c = rows_per_chunk
    d_ff = wg_ref.shape[1]
    if single_tile:
        @pl.when(e == 0)
        def _():
            o_ref[...] = acc_ref[...]
    else:
        o_ref[...] = acc_ref[...]

    gate_row = gate_t_ref[pl.ds(e, 1), :]
    rank_row = rank_t_ref[pl.ds(e, 1), :]
    routed_row = gate_row > 0.0
    n_routed = jnp.sum(jnp.where(routed_row, 1.0, 0.0)).astype(I32)
    n_chunks = (n_routed + (c - 1)) // c

    lane = lax.broadcasted_iota(I32, gate_ref.shape, 1)
    rank_col = jnp.sum(jnp.where(lane == e, rank_ref[...], 0.0), axis=1, keepdims=True)
    routed_col = jnp.sum(jnp.where(lane == e, gate_ref[...], 0.0), axis=1, keepdims=True) > 0.0

    def chunk(ci, carry):
        slot_rows = (ci * c + lax.broadcasted_iota(I32, (c, tm), 0)).astype(F32)
        pack = routed_row & (rank_row == slot_rows)
        xs = jnp.dot(jnp.where(pack, 1.0, 0.0).astype(BF16), xn_ref[...], preferred_element_type=F32).astype(BF16)
        y = None
        for f0 in range(0, d_ff, fc):
            hg = jnp.dot(xs, wg_ref[:, f0:f0 + fc], preferred_element_type=F32)
            hu = jnp.dot(xs, wu_ref[:, f0:f0 + fc], preferred_element_type=F32)
            h = ((hg * _sigmoid(hg)) * hu).astype(BF16)
            part = jnp.dot(h, wd_ref[f0:f0 + fc, :], preferred_element_type=F32)
            y = part if y is None else y + part
        y = y * jnp.sum(jnp.where(pack, gate_row, 0.0), axis=1, keepdims=True)
        y_hi = y.astype(BF16)
        y_lo = (y - y_hi.astype(F32)).astype(BF16)
        slot_cols = (ci * c + lax.broadcasted_iota(I32, (tm, c), 1)).astype(F32)
        unpack = jnp.where(routed_col & (rank_col == slot_cols), 1.0, 0.0).astype(BF16)
        o_ref[...] += jnp.dot(jnp.concatenate([unpack, unpack], axis=1), jnp.concatenate([y_hi, y_lo], axis=0),
                              preferred_element_type=F32)
        return carry

    lax.fori_loop(0, n_chunks, chunk, 0)


def _moe(x2d, g, router, wg, wu, wd):
    n, d = x2d.shape
    n_exp, _, d_ff = wg.shape
    assert TOP_K_EXPERTS == 2 and n_exp <= V7X_LANES
    tm = min(1024, n)
    rows_per_chunk = min(V7X_LANES, tm)
    assert n % tm == 0 and tm % V7X_LANES == 0
    n_tiles = n // tm
    assert n_tiles == 1 or n_tiles >= 4
    fc = d_ff
    for cand in (896, 1024, 768, 512):
        if d_ff % cand == 0:
            fc = cand
            break
    r_pad = jnp.concatenate([router, jnp.zeros((d, V7X_LANES - n_exp), router.dtype)], axis=1)
    tok = lambda width: pl.BlockSpec((tm, width), lambda i: (i, 0))
    per_tile = pl.BlockSpec((None, V7X_LANES, tm), lambda i: (i, 0, 0))
    xn, gate, gate_t, rank, rank_t = pl.pallas_call(
        functools.partial(_moe_route_body, n_exp=n_exp, tm=tm),
        out_shape=(jax.ShapeDtypeStruct((n, d), BF16),
                   jax.ShapeDtypeStruct((n, V7X_LANES), F32), jax.ShapeDtypeStruct((n_tiles, V7X_LANES, tm), F32),
                   jax.ShapeDtypeStruct((n, V7X_LANES), F32), jax.ShapeDtypeStruct((n_tiles, V7X_LANES, tm), F32)),
        grid=(n_tiles,),
        in_specs=[tok(d), _const_spec((1, d)), _const_spec((d, V7X_LANES))],
        out_specs=(tok(d), tok(V7X_LANES), per_tile, tok(V7X_LANES), per_tile),
        compiler_params=_params("parallel"),
        name="moe_route",
    )(x2d, g.reshape(1, d), r_pad)

    tok2 = lambda width: pl.BlockSpec((tm, width), lambda e, i: (i, 0))
    per_tile2 = pl.BlockSpec((None, V7X_LANES, tm), lambda e, i: (i, 0, 0))
    w_spec = lambda rows, cols: pl.BlockSpec((None, rows, cols), lambda e, i: (e, 0, 0), pipeline_mode=pl.Buffered(1))
    return pl.pallas_call(
        functools.partial(_moe_expert_body, fc=fc, tm=tm, rows_per_chunk=rows_per_chunk, single_tile=n_tiles == 1),
        out_shape=jax.ShapeDtypeStruct((n, d), F32),
        grid=(n_exp, n_tiles),
        in_specs=[tok2(d), tok2(d), tok2(V7X_LANES), per_tile2, tok2(V7X_LANES), per_tile2,
                  w_spec(d, d_ff), w_spec(d, d_ff), w_spec(d_ff, d)],
        out_specs=tok2(d),
        input_output_aliases={0: 0},
        compiler_params=_params("arbitrary", "arbitrary"),
        name="moe_experts",
    )(x2d, xn, gate, gate_t, rank, rank_t, wg.astype(BF16), wu.astype(BF16), wd.astype(BF16))


def _final_norm_body(x_ref, g_ref, o_ref):
    o_ref[...] = _rms(x_ref[...], g_ref[...])


def _final_norm(x2d, g):
    n, d = x2d.shape
    tm = min(1024, n)
    return pl.pallas_call(
        _final_norm_body,
        out_shape=jax.ShapeDtypeStruct((n, d), F32),
        grid=(n // tm,),
        in_specs=[pl.BlockSpec((tm, d), lambda i: (i, 0)), _const_spec((1, d))],
        out_specs=pl.BlockSpec((tm, d), lambda i: (i, 0)),
        compiler_params=_params("parallel"),
        name="final_norm",
    )(x2d, g.reshape(1, d))


def _layer(x, past, h0, lw, *, layer_idx, precise_ssm):
    b, t, d = x.shape
    d_ssm = lw["ssm_d"].shape[0]
    q, k, v, iq, ik, iw, u, ga, gb = _inproj(x, lw["norm_mix_g"], lw["w_in_packed"], d_ssm)

    if past is None:
        kb = min(512, t)
        k_all, v_all, ik_all, s_valid = k, v, ik, t
    else:
        ck, cv, cik = past
        k_all = jnp.concatenate([ck.reshape(b, -1, D_ATTN), k], axis=1)
        v_all = jnp.concatenate([cv.reshape(b, -1, D_ATTN), v], axis=1)
        ik_all = jnp.concatenate([cik, ik], axis=1)
        s_valid = k_all.shape[1]
        kb = 512
        pad = (-s_valid) % kb
        padf = lambda a: jnp.pad(a, ((0, 0), (0, pad), (0, 0)))
        k_all, v_all, ik_all = padf(k_all), padf(v_all), padf(ik_all)
    att = _dsa(q, iq, iw, k_all, v_all, ik_all, n_keep=min(TOPK_MAX, s_valid // 4), causal=past is None,
               n_valid=s_valid, kb=kb)

    ssm_tm, h_re, h_im = _ssm(u.transpose(1, 0, 2), h0[0], h0[1], lw["disc"], lw["ssm_b_re"], lw["ssm_b_im"], lw["ssm_c_re"],
                              lw["ssm_c_im"], lw["ssm_d"], lw["glu_w"], lw["glu_b"], precise=precise_ssm)
    x = _merge(att, ssm_tm.transpose(1, 0, 2), ga, gb, x, lw["w_branch_attn"], lw["w_branch_ssm"], lw["w_out"])
    x2d = x.reshape(b * t, d)
    if layer_idx % 2 == 0:
        x2d = _ffn(x2d, lw["norm_ffn_g"], lw["ffn_w_gate"], lw["ffn_w_up"], lw["ffn_w_down"])
    else:
        x2d = _moe(x2d, lw["norm_ffn_g"], lw["moe_router"], lw["moe_w_gate"], lw["moe_w_up"], lw["moe_w_down"])
    g_shape = (b, -1, SSM_STATE)
    return (x2d.reshape(b, t, d), k.reshape(b, t, N_HEADS, HEAD_DIM), v.reshape(b, t, N_HEADS, HEAD_DIM), ik,
            h_re.reshape(g_shape), h_im.reshape(g_shape))


def kernel(x_prompt, x_sample, cache_k, cache_v, cache_idx_k, state_ssm_re, state_ssm_im, norm_mix_g, w_in,
           ssm_a_re, ssm_a_im, ssm_log_dt, ssm_b_re, ssm_b_im, ssm_c_re, ssm_c_im, ssm_d, glu_w, glu_b,
           w_branch_attn, w_branch_ssm, w_out, norm_ffn_g, ffn_w_gate, ffn_w_up, ffn_w_down, moe_router,
           moe_w_gate, moe_w_up, moe_w_down, final_norm_g):
    depth = w_in.shape[0]
    d_model = x_prompt.shape[-1]
    d_ssm = ssm_d.shape[1]
    xp, xs = x_prompt, x_sample
    outs_p, outs_s = [], []
    for l in range(depth):
        lw = dict(
            norm_mix_g=norm_mix_g[l], w_in_packed=_pack_w_in(w_in[l], d_ssm, d_model),
            disc=_ssm_discretize(ssm_a_re[l], ssm_a_im[l], ssm_log_dt[l]),
            ssm_b_re=ssm_b_re[l], ssm_b_im=ssm_b_im[l], ssm_c_re=ssm_c_re[l], ssm_c_im=ssm_c_im[l],
            ssm_d=ssm_d[l], glu_w=glu_w[l], glu_b=glu_b[l],
            w_branch_attn=w_branch_attn[l], w_branch_ssm=w_branch_ssm[l], w_out=w_out[l], norm_ffn_g=norm_ffn_g[l])
        i = l // 2
        if l % 2 == 0:
            lw.update(ffn_w_gate=ffn_w_gate[i], ffn_w_up=ffn_w_up[i], ffn_w_down=ffn_w_down[i])
        else:
            lw.update(moe_router=moe_router[i], moe_w_gate=moe_w_gate[i], moe_w_up=moe_w_up[i],
                      moe_w_down=moe_w_down[i])
        n_state = ssm_a_re.shape[1] * ssm_a_re.shape[2]
        zeros = jnp.zeros((xp.shape[0], n_state), F32)
        xp, *rest_p = _layer(xp, None, (zeros, zeros), lw, layer_idx=l, precise_ssm=False)
        outs_p.append(rest_p)
        h0 = (state_ssm_re[l].reshape(xs.shape[0], n_state), state_ssm_im[l].reshape(xs.shape[0], n_state))
        xs, *rest_s = _layer(xs, (cache_k[l], cache_v[l], cache_idx_k[l]), h0, lw, layer_idx=l, precise_ssm=True)
        outs_s.append(rest_s)

    y_prompt = _final_norm(xp.reshape(-1, d_model), final_norm_g).reshape(xp.shape)
    y_sample = _final_norm(xs.reshape(-1, d_model), final_norm_g).reshape(xs.shape)
    stack = lambda outs, j: jnp.stack([o[j] for o in outs], 0)
    return (y_prompt, y_sample,
            stack(outs_p, 0), stack(outs_p, 1), stack(outs_p, 2), stack(outs_p, 3), stack(outs_p, 4),
            stack(outs_s, 0), stack(outs_s, 1), stack(outs_s, 2), stack(outs_s, 3), stack(outs_s, 4))
```

```python
import functools
import math

import jax
import jax.numpy as jnp
from jax import lax
from jax.experimental import pallas as pl
from jax.experimental.pallas import tpu as pltpu

F32 = jnp.float32
BF16 = jnp.bfloat16
I32 = jnp.int32

CHUNK = 64
N_HEADS = 8
HEAD_DIM = 64
N_IDX_HEADS = 8
IDX_DIM = 64
TOPK_MAX = 256
SSM_GROUP = 16
SSM_STATE = 64
TOP_K_EXPERTS = 2
RMS_EPS = 1e-6
NEG_INF = -1e30
INT32_MIN = -(2 ** 31)

D_ATTN = N_HEADS * HEAD_DIM
D_IDX = N_IDX_HEADS * IDX_DIM
IDX_W_SCALE = float(D_IDX) ** -0.5
ATTN_SCALE = float(HEAD_DIM) ** -0.5

V7X_LANES = 128
V7X_BF16_SUBLANES = 16
V7X_VMEM_LIMIT_BYTES = 56 * 1024 * 1024


def _params(*semantics):
    return pltpu.CompilerParams(dimension_semantics=semantics, vmem_limit_bytes=V7X_VMEM_LIMIT_BYTES)


def _sigmoid(x):
    return 1.0 / (1.0 + jnp.exp(-x))


def _rms(x, g):
    return (x * lax.rsqrt(jnp.mean(x * x, axis=-1, keepdims=True) + RMS_EPS)) * g


def _const_spec(shape):
    return pl.BlockSpec(shape, lambda *_: (0,) * len(shape))


_SEG_Q, _SEG_K, _SEG_V, _SEG_IQ = 0, D_ATTN, 2 * D_ATTN, 3 * D_ATTN
_SEG_SMALL = 3 * D_ATTN + D_IDX


def _pack_w_in(w_in, d_ssm, d_model):
    offs = [0]
    for s in (D_ATTN, D_ATTN, D_ATTN, D_IDX, IDX_DIM, N_IDX_HEADS, d_ssm, d_model, d_model):
        offs.append(offs[-1] + s)
    pad = V7X_LANES - IDX_DIM - N_IDX_HEADS
    small = jnp.concatenate([w_in[:, offs[4]:offs[6]], jnp.zeros((w_in.shape[0], pad), w_in.dtype)], axis=1)
    return jnp.concatenate([w_in[:, :offs[4]], small, w_in[:, offs[6]:]], axis=1).astype(BF16)


def _inproj_body(x_ref, g_ref, w_ref, q_ref, k_ref, v_ref, iq_ref, ik_ref, iw_ref, u_ref, ga_ref, gb_ref, *, d_ssm, d_model):
    hb = _rms(x_ref[...], g_ref[...]).astype(BF16)

    def proj(lo, n):
        return jnp.dot(hb, w_ref[:, lo:lo + n], preferred_element_type=F32)

    q_ref[...] = proj(_SEG_Q, D_ATTN).T.astype(BF16)
    k_ref[...] = proj(_SEG_K, D_ATTN)
    v_ref[...] = proj(_SEG_V, D_ATTN)
    iq_ref[...] = proj(_SEG_IQ, D_IDX).T.astype(BF16)
    small = proj(_SEG_SMALL, V7X_LANES)
    ik_ref[...] = small[:, :IDX_DIM]
    iw_ref[...] = small.T[IDX_DIM:IDX_DIM + N_IDX_HEADS, :] * IDX_W_SCALE
    seg_u = _SEG_SMALL + V7X_LANES
    u_ref[...] = proj(seg_u, d_ssm)
    ga_ref[...] = proj(seg_u + d_ssm, d_model)
    gb_ref[...] = proj(seg_u + d_ssm + d_model, d_model)


def _inproj(x, g, w_packed, d_ssm):
    b, t, d = x.shape
    tm = min(256, t)
    assert t % tm == 0 and tm % V7X_LANES == 0
    n_w = w_packed.shape[1]

    def tok(n, dtype):
        return jax.ShapeDtypeStruct((b, t, n), dtype), pl.BlockSpec((None, tm, n), lambda bi, i: (bi, i, 0))

    def tok_t(n, dtype):
        return jax.ShapeDtypeStruct((b, n, t), dtype), pl.BlockSpec((None, n, tm), lambda bi, i: (bi, 0, i))

    shapes, specs = zip(
        tok_t(D_ATTN, BF16), tok(D_ATTN, F32), tok(D_ATTN, F32), tok_t(D_IDX, BF16), tok(IDX_DIM, F32),
        tok_t(N_IDX_HEADS, F32), tok(d_ssm, F32), tok(d, F32), tok(d, F32))
    return pl.pallas_call(
        functools.partial(_inproj_body, d_ssm=d_ssm, d_model=d),
        out_shape=shapes,
        grid=(b, t // tm),
        in_specs=[pl.BlockSpec((None, tm, d), lambda bi, i: (bi, i, 0)), _const_spec((1, d)), _const_spec((d, n_w))],
        out_specs=specs,
        compiler_params=_params("parallel", "parallel"),
        name="inproj",
    )(x, g.reshape(1, d), w_packed)


def _dsa_body(qt_ref, iqt_ref, iwt_ref, k_ref, vt_ref, ik_ref, o_ref, key_ref, qs_ref, acc_ref, m_ref, s_ref, s2_ref, *,
              tq, kb, n_keep, causal, n_valid, n_kb_total, pos_bits):
    col0 = pl.program_id(1) * tq
    if causal:
        n_kb = (col0 + tq + kb - 1) // kb
        q_pos = col0 + lax.broadcasted_iota(I32, (1, tq), 1)
        shift = CHUNK.bit_length() - 1
        q_limit = ((q_pos >> shift) + 1) << shift
    else:
        n_kb = n_kb_total
        q_limit = jnp.full((1, tq), n_valid, I32)

    ks = min(V7X_LANES, kb)
    cr = min(64, kb)

    def key_pos(j, r0, rows):
        return j * kb + r0 + lax.broadcasted_iota(I32, (rows, tq), 0)

    iwt = iwt_ref[...]

    def idx_block(j, carry):
        for r0 in range(0, kb, ks):
            ik_tile = ik_ref[pl.ds(pl.multiple_of(j * kb + r0, ks), ks), :]
            s = None
            for h in range(N_IDX_HEADS):
                d = jnp.dot(ik_tile, iqt_ref[h * IDX_DIM:(h + 1) * IDX_DIM, :], preferred_element_type=F32)
                term = iwt[h:h + 1, :] * jnp.maximum(d, 0.0)
                s = term if s is None else s + term
            s = jnp.where(s == 0.0, 0.0, s)
            s = jnp.where(key_pos(j, r0, ks) < q_limit, s, NEG_INF)
            bits = lax.bitcast_convert_type(s, I32)
            key_ref[j, r0:r0 + ks, :] = jnp.where(bits < 0, bits ^ jnp.int32(0x7FFFFFFF), bits)
        return carry

    lax.fori_loop(0, n_kb, idx_block, 0)

    def count(pred):
        def body(j, accs):
            accs = list(accs)
            for n, r0 in enumerate(range(0, kb, cr)):
                hit = pred(key_ref[j, r0:r0 + cr, :], functools.partial(key_pos, j, r0, cr))
                accs[n % 2] = accs[n % 2] + jnp.where(hit, 1.0, 0.0)
            return tuple(accs)
        zero = jnp.zeros((cr, tq), F32)
        acc_a, acc_b = lax.fori_loop(0, n_kb, body, (zero, zero))
        return jnp.sum(acc_a + acc_b, axis=0, keepdims=True)

    def bit_step(i, thr):
        cand = thr + lax.shift_left(jnp.int32(1), 31 - i)
        cnt = count(lambda key, pos: key >= cand)
        return jnp.where(cnt >= n_keep, cand, thr)

    thr = lax.fori_loop(0, 32, bit_step, jnp.full((1, tq), INT32_MIN, I32))

    need = n_keep - count(lambda key, pos: key > thr)
    n_eq = count(lambda key, pos: key == thr)
    pos_all = jnp.int32(2 ** pos_bits - 1)

    def tie_search():
        def step(i, end):
            cand = end + lax.shift_left(jnp.int32(1), pos_bits - 1 - i)
            cnt = count(lambda key, pos: (key == thr) & (pos() < cand))
            return jnp.where(cnt <= need, cand, end)
        return lax.fori_loop(0, pos_bits, step, jnp.zeros((1, tq), I32))

    tie_end = lax.cond(jnp.max(n_eq - need) > 0.0, tie_search, lambda: jnp.full((1, tq), pos_all, I32))

    qs_ref[...] = qt_ref[...] * jnp.asarray(ATTN_SCALE, BF16)
    m_ref[...] = jnp.full(m_ref.shape, NEG_INF, F32)
    acc_ref[...] = jnp.zeros(acc_ref.shape, F32)

    def bias_block(j, carry):
        for r0 in range(0, kb, cr):
            key = key_ref[j, r0:r0 + cr, :]
            pos = key_pos(j, r0, cr)
            keep = (pos < q_limit) & ((key > thr) | ((key == thr) & (pos < tie_end)))
            key_ref[j, r0:r0 + cr, :] = lax.bitcast_convert_type(jnp.where(keep, 0.0, NEG_INF), I32)
        return carry

    lax.fori_loop(0, n_kb, bias_block, 0)

    v_rows = acc_ref.shape[1]

    def issue_scores(j, r0, buf):
        k_rows = pl.ds(pl.multiple_of(j * kb + r0, ks), ks)
        for h in range(N_HEADS):
            buf[h] = jnp.dot(k_ref[h, k_rows, :], qs_ref[h * HEAD_DIM:(h + 1) * HEAD_DIM, :],
                             preferred_element_type=F32)

    def att_block(j, carry):
        tiles = list(range(0, kb, ks))
        bufs = [s_ref, s2_ref]
        issue_scores(j, tiles[0], bufs[0])
        for i, r0 in enumerate(tiles):
            if i + 1 < len(tiles):
                issue_scores(j, tiles[i + 1], bufs[(i + 1) % 2])
            bias = lax.bitcast_convert_type(key_ref[j, r0:r0 + ks, :], F32)
            for h in range(N_HEADS):
                s = bufs[i % 2][h] + bias
                m_old = m_ref[h]
                m_new = jnp.maximum(m_old, jnp.max(s, axis=0, keepdims=True))
                alpha = jnp.exp(m_old - m_new)
                p = jnp.exp((s.reshape(ks // 8, 8, tq) - m_new[None]).reshape(ks, tq).astype(BF16))
                pv = jnp.dot(vt_ref[j, h, :, r0:r0 + ks], p, preferred_element_type=F32)
                acc = acc_ref[h].reshape(v_rows // 8, 8, tq) * alpha[None]
                acc_ref[h] = acc.reshape(v_rows, tq) + pv
                m_ref[h] = m_new
        return carry

    lax.fori_loop(0, n_kb, att_block, 0)
    for h in range(N_HEADS):
        rows = slice(h * HEAD_DIM, (h + 1) * HEAD_DIM)
        o_ref[rows, :] = (acc_ref[h, :HEAD_DIM, :] / acc_ref[h, HEAD_DIM:HEAD_DIM + 1, :]).astype(o_ref.dtype)


def _dsa(qt, iqt, iwt, k, v, ik, *, n_keep, causal, n_valid, kb):
    b, _, t_pad = qt.shape
    s_pad = k.shape[1]
    n_kb_total = s_pad // kb
    tq = min(256, t_pad)
    assert t_pad % tq == 0 and tq % V7X_LANES == 0 and (not causal or tq % CHUNK == 0)
    k_hm = k.astype(BF16).reshape(b, s_pad, N_HEADS, HEAD_DIM).transpose(0, 2, 1, 3)
    vt_blk = v.astype(BF16).reshape(b, n_kb_total, kb, N_HEADS, HEAD_DIM).transpose(0, 1, 3, 4, 2)
    vt_blk = jnp.concatenate([vt_blk, jnp.ones((b, n_kb_total, N_HEADS, V7X_BF16_SUBLANES, kb), BF16)], axis=3)
    v_rows = HEAD_DIM + V7X_BF16_SUBLANES
    body = functools.partial(_dsa_body, tq=tq, kb=kb, n_keep=float(n_keep), causal=causal, n_valid=n_valid,
                             n_kb_total=n_kb_total, pos_bits=s_pad.bit_length())
    qry = lambda n: pl.BlockSpec((None, n, tq), lambda bi, i: (bi, 0, i))
    return pl.pallas_call(
        body,
        out_shape=jax.ShapeDtypeStruct((b, D_ATTN, t_pad), BF16),
        grid=(b, t_pad // tq),
        in_specs=[qry(D_ATTN), qry(D_IDX), qry(N_IDX_HEADS),
                  pl.BlockSpec((None, N_HEADS, s_pad, HEAD_DIM), lambda bi, i: (bi, 0, 0, 0)),
                  pl.BlockSpec((None, n_kb_total, N_HEADS, v_rows, kb), lambda bi, i: (bi, 0, 0, 0, 0)),
                  pl.BlockSpec((None, s_pad, IDX_DIM), lambda bi, i: (bi, 0, 0))],
        out_specs=qry(D_ATTN),
        scratch_shapes=[pltpu.VMEM((n_kb_total, kb, tq), I32),
                        pltpu.VMEM((D_ATTN, tq), BF16),
                        pltpu.VMEM((N_HEADS, v_rows, tq), F32),
                        pltpu.VMEM((N_HEADS, 8, tq), F32),
                        pltpu.VMEM((N_HEADS, min(V7X_LANES, kb), tq), F32),
                        pltpu.VMEM((N_HEADS, min(V7X_LANES, kb), tq), F32)],
        compiler_params=_params("parallel", "arbitrary"),
        name="dsa",
    )(qt, iqt, iwt, k_hm, vt_blk, ik.astype(BF16))


def _ssm_discretize_body(a_re_ref, a_im_ref, log_dt_ref, ab_re_ref, ab_im_ref, bc_re_ref, bc_im_ref):
    a_re, a_im = a_re_ref[...], a_im_ref[...]
    dt = jnp.exp(log_dt_ref[...])
    mag = jnp.exp(dt * a_re)
    ab_re = mag * jnp.cos(dt * a_im)
    ab_im = mag * jnp.sin(dt * a_im)
    den = a_re * a_re + a_im * a_im
    nr = ab_re - 1.0
    ni = ab_im
    ab_re_ref[...] = ab_re
    ab_im_ref[...] = ab_im
    bc_re_ref[...] = (nr * a_re + ni * a_im) / den
    bc_im_ref[...] = (ni * a_re - nr * a_im) / den


def _ssm_discretize(a_re, a_im, log_dt):
    g, p = a_re.shape
    out = jax.ShapeDtypeStruct((g, p), F32)
    return pl.pallas_call(_ssm_discretize_body, out_shape=(out,) * 4, name="ssm_discretize")(
        a_re, a_im, log_dt.reshape(g, 1))


def _block_diag_halves(w, dtype):
    g, r, c = w.shape
    gh = g // 2
    eye = jnp.eye(gh, dtype=w.dtype)
    halves = [jnp.einsum("grc,gh->grhc", w[i * gh:(i + 1) * gh], eye).reshape(gh * r, gh * c) for i in range(2)]
    return jnp.stack(halves, 0).astype(dtype)


def _ssm_body(u_ref, h0r_ref, h0i_ref, abr_ref, abi_ref, bcr_ref, bci_ref, br_ref, bi_ref, cr_ref, ci_ref,
              d_ref, gw_ref, gb_ref, y_ref, hr_ref, hi_ref, xr_ref, xi_ref, *, tb, nb, precise):
    d_ssm = u_ref.shape[-1]
    n_state = hr_ref.shape[-1]
    kh, sh = d_ssm // 2, n_state // 2

    def mm(a, w):
        if precise:
            return jnp.dot(a, w, preferred_element_type=F32, precision=lax.Precision.HIGHEST)
        return jnp.dot(a.astype(BF16), w, preferred_element_type=F32)

    @pl.when(pl.program_id(0) == 0)
    def _():
        hr_ref[...] = h0r_ref[...]
        hi_ref[...] = h0i_ref[...]

    u = u_ref[...].reshape(tb * nb, d_ssm)

    for half in range(2):
        cols = slice(half * sh, (half + 1) * sh)
        uh = u[:, half * kh:(half + 1) * kh]
        bu_r = mm(uh, br_ref[half])
        bu_i = mm(uh, bi_ref[half])
        bc_r, bc_i = bcr_ref[:, cols], bci_ref[:, cols]
        xr_ref[:, cols] = bc_r * bu_r - bc_i * bu_i
        xi_ref[:, cols] = bc_r * bu_i + bc_i * bu_r

    slab = min(512, n_state)
    for s0 in range(0, n_state, slab):
        cols = slice(s0, s0 + slab)
        a_r = jnp.broadcast_to(abr_ref[:, cols], (nb, slab))
        a_i = jnp.broadcast_to(abi_ref[:, cols], (nb, slab))

        def step(t, h):
            h_r, h_i = h
            rows = pl.ds(pl.multiple_of(t * nb, nb), nb)
            n_r = a_r * h_r - a_i * h_i + xr_ref[rows, cols]
            n_i = a_r * h_i + a_i * h_r + xi_ref[rows, cols]
            xr_ref[rows, cols] = n_r
            xi_ref[rows, cols] = n_i
            return n_r, n_i

        h_r, h_i = lax.fori_loop(0, tb, step, (hr_ref[:, cols], hi_ref[:, cols]), unroll=min(8, tb))
        hr_ref[:, cols] = h_r
        hi_ref[:, cols] = h_i

    ys = []
    for half in range(2):
        cols = slice(half * sh, (half + 1) * sh)
        ys.append(mm(xr_ref[:, cols], cr_ref[half]) - mm(xi_ref[:, cols], ci_ref[half]))
    y = jnp.concatenate(ys, axis=1) + d_ref[...] * u
    y = y * (0.5 * (1.0 + jnp.tanh(math.sqrt(2.0 / math.pi) * (y + 0.044715 * (y * y * y)))))
    z = mm(y, gw_ref[...]) + gb_ref[...]
    y_ref[...] = (y * _sigmoid(z)).reshape(tb, nb, d_ssm)


def _ssm(u_tm, h0_re, h0_im, disc, b_re, b_im, c_re, c_im, d, glu_w, glu_b, *, precise):
    t, nb, d_ssm = u_tm.shape
    n_state = h0_re.shape[1]
    assert nb % 8 == 0
    tb = min(32, t)
    wdt = F32 if precise else BF16
    br = _block_diag_halves(jnp.swapaxes(b_re, 1, 2), wdt)
    bi = _block_diag_halves(jnp.swapaxes(b_im, 1, 2), wdt)
    cr = _block_diag_halves(jnp.swapaxes(c_re, 1, 2), wdt)
    ci = _block_diag_halves(jnp.swapaxes(c_im, 1, 2), wdt)
    row = lambda a: a.reshape(1, -1).astype(F32)
    state_spec = _const_spec((nb, n_state))
    vec_state = _const_spec((1, n_state))
    vec_ch = _const_spec((1, d_ssm))
    return pl.pallas_call(
        functools.partial(_ssm_body, tb=tb, nb=nb, precise=precise),
        out_shape=(jax.ShapeDtypeStruct((t, nb, d_ssm), F32),
                   jax.ShapeDtypeStruct((nb, n_state), F32), jax.ShapeDtypeStruct((nb, n_state), F32)),
        grid=(t // tb,),
        in_specs=[pl.BlockSpec((tb, nb, d_ssm), lambda i: (i, 0, 0)), state_spec, state_spec,
                  vec_state, vec_state, vec_state, vec_state,
                  _const_spec(br.shape), _const_spec(bi.shape), _const_spec(cr.shape), _const_spec(ci.shape),
                  vec_ch, _const_spec((d_ssm, d_ssm)), vec_ch],
        out_specs=(pl.BlockSpec((tb, nb, d_ssm), lambda i: (i, 0, 0)), state_spec, state_spec),
        scratch_shapes=[pltpu.VMEM((tb * nb, n_state), F32), pltpu.VMEM((tb * nb, n_state), F32)],
        compiler_params=_params("arbitrary"),
        name="ssm",
    )(u_tm, h0_re, h0_im, row(disc[0]), row(disc[1]), row(disc[2]), row(disc[3]), br, bi, cr, ci,
      row(d), glu_w.astype(wdt), row(glu_b))


def _merge_body(att_t_ref, ssm_ref, ga_ref, gb_ref, x_ref, wpa_ref, wpb_ref, wo_ref, o_ref):
    att = att_t_ref[...].astype(F32).T.astype(BF16)
    a = jnp.dot(att, wpa_ref[...], preferred_element_type=F32)
    s = jnp.dot(ssm_ref[...].astype(BF16), wpb_ref[...], preferred_element_type=F32)
    mix = _sigmoid(ga_ref[...]) * a + _sigmoid(gb_ref[...]) * s
    o_ref[...] = x_ref[...] + jnp.dot(mix.astype(BF16), wo_ref[...], preferred_element_type=F32)


def _merge(att_t, ssm, ga, gb, x, w_pa, w_pb, w_o):
    b, t, d = x.shape
    tm = min(512, t)
    assert t % tm == 0 and tm % V7X_LANES == 0
    tok = lambda n: pl.BlockSpec((None, tm, n), lambda bi, i: (bi, i, 0))
    d_ssm = ssm.shape[-1]
    return pl.pallas_call(
        _merge_body,
        out_shape=jax.ShapeDtypeStruct((b, t, d), F32),
        grid=(b, t // tm),
        in_specs=[pl.BlockSpec((None, D_ATTN, tm), lambda bi, i: (bi, 0, i)), tok(d_ssm), tok(d), tok(d), tok(d),
                  _const_spec(w_pa.shape), _const_spec(w_pb.shape), _const_spec(w_o.shape)],
        out_specs=tok(d),
        compiler_params=_params("parallel", "parallel"),
        name="merge",
    )(att_t, ssm, ga, gb, x, w_pa.astype(BF16), w_pb.astype(BF16), w_o.astype(BF16))


def _ffn_body(x_ref, g_ref, wg_ref, wu_ref, wd_ref, o_ref, xn_ref, *, n_f):
    f = pl.program_id(1)

    @pl.when(f == 0)
    def _():
        xn_ref[...] = _rms(x_ref[...], g_ref[...]).astype(BF16)
        o_ref[...] = x_ref[...]

    xn = xn_ref[...]
    hg = jnp.dot(xn, wg_ref[...], preferred_element_type=F32)
    hu = jnp.dot(xn, wu_ref[...], preferred_element_type=F32)
    h = (hg * _sigmoid(hg)) * hu
    o_ref[...] += jnp.dot(h.astype(BF16), wd_ref[...], preferred_element_type=F32)


def _ffn(x2d, g, wg, wu, wd):
    n, d = x2d.shape
    d_ff = wg.shape[1]
    tm = min(512, n)
    fc = d_ff
    for cand in (1408, 1024, 896, 768, 512):
        if d_ff % cand == 0:
            fc = cand
            break
    n_f = d_ff // fc
    return pl.pallas_call(
        functools.partial(_ffn_body, n_f=n_f),
        out_shape=jax.ShapeDtypeStruct((n, d), F32),
        grid=(n // tm, n_f),
        in_specs=[pl.BlockSpec((tm, d), lambda i, f: (i, 0)), _const_spec((1, d)),
                  pl.BlockSpec((d, fc), lambda i, f: (0, f)), pl.BlockSpec((d, fc), lambda i, f: (0, f)),
                  pl.BlockSpec((fc, d), lambda i, f: (f, 0))],
        out_specs=pl.BlockSpec((tm, d), lambda i, f: (i, 0)),
        scratch_shapes=[pltpu.VMEM((tm, d), BF16)],
        compiler_params=_params("parallel", "arbitrary"),
        name="ffn",
    )(x2d, g.reshape(1, d), wg.astype(BF16), wu.astype(BF16), wd.astype(BF16))


def _moe_route_body(x_ref, g_ref, r_ref, xn_ref, gate_ref, gate_t_ref, rank_ref, rank_t_ref, *, n_exp, tm):
    xn = _rms(x_ref[...], g_ref[...])
    xn_ref[...] = xn.astype(BF16)
    logits = jnp.dot(xn, r_ref[...], preferred_element_type=F32, precision=lax.Precision.HIGHEST)
    lane = lax.broadcasted_iota(I32, logits.shape, 1).astype(F32)
    logits = jnp.where(lane < n_exp, logits, NEG_INF)
    big = float(logits.shape[1])
    m1 = jnp.max(logits, axis=1, keepdims=True)
    i1 = jnp.min(jnp.where(logits == m1, lane, big), axis=1, keepdims=True)
    rest = jnp.where(lane == i1, NEG_INF, logits)
    m2 = jnp.max(rest, axis=1, keepdims=True)
    i2 = jnp.min(jnp.where(rest == m2, lane, big), axis=1, keepdims=True)
    e2 = jnp.exp(m2 - m1)
    w1 = 1.0 / (1.0 + e2)
    w2 = e2 / (1.0 + e2)
    gates = jnp.where(lane == i1, w1, 0.0) + jnp.where(lane == i2, w2, 0.0)
    gates_t = gates.T
    gate_ref[...] = gates
    gate_t_ref[...] = gates_t
    t_row = lax.broadcasted_iota(I32, (tm, tm), 0)
    t_col = lax.broadcasted_iota(I32, (tm, tm), 1)
    member = jnp.where(gates > 0.0, 1.0, 0.0).astype(BF16)
    member_t = jnp.where(gates_t > 0.0, 1.0, 0.0).astype(BF16)
    rank_ref[...] = jnp.dot(jnp.where(t_col < t_row, 1.0, 0.0).astype(BF16), member, preferred_element_type=F32)
    rank_t_ref[...] = jnp.dot(member_t, jnp.where(t_row < t_col, 1.0, 0.0).astype(BF16),
                              preferred_element_type=F32)


def _moe_expert_body(acc_ref, xn_ref, gate_ref, gate_t_ref, rank_ref, rank_t_ref, wg_ref, wu_ref, wd_ref, o_ref, *,
                     fc, tm, rows_per_chunk, single_tile):
    e = pl.program_id(0)
    c = rows_per_chunk
    d_ff = wg_ref.shape[1]
    if single_tile:
        @pl.when(e == 0)
        def _():
            o_ref[...] = acc_ref[...]
    else:
        o_ref[...] = acc_ref[...]

    gate_row = gate_t_ref[pl.ds(e, 1), :]
    rank_row = rank_t_ref[pl.ds(e, 1), :]
    routed_row = gate_row > 0.0
    n_routed = jnp.sum(jnp.where(routed_row, 1.0, 0.0)).astype(I32)
    n_chunks = (n_routed + (c - 1)) // c

    lane = lax.broadcasted_iota(I32, gate_ref.shape, 1)
    rank_col = jnp.sum(jnp.where(lane == e, rank_ref[...], 0.0), axis=1, keepdims=True)
    routed_col = jnp.sum(jnp.where(lane == e, gate_ref[...], 0.0), axis=1, keepdims=True) > 0.0

    def chunk(ci, carry):
        slot_rows = (ci * c + lax.broadcasted_iota(I32, (c, tm), 0)).astype(F32)
        pack = routed_row & (rank_row == slot_rows)
        xs = jnp.dot(jnp.where(pack, 1.0, 0.0).astype(BF16), xn_ref[...], preferred_element_type=F32).astype(BF16)
        y = None
        for f0 in range(0, d_ff, fc):
            hg = jnp.dot(xs, wg_ref[:, f0:f0 + fc], preferred_element_type=F32)
            hu = jnp.dot(xs, wu_ref[:, f0:f0 + fc], preferred_element_type=F32)
            h = ((hg * _sigmoid(hg)) * hu).astype(BF16)
            part = jnp.dot(h, wd_ref[f0:f0 + fc, :], preferred_element_type=F32)
            y = part if y is None else y + part
        y = y * jnp.sum(jnp.where(pack, gate_row, 0.0), axis=1, keepdims=True)
        y_hi = y.astype(BF16)
        y_lo = (y - y_hi.astype(F32)).astype(BF16)
        slot_cols = (ci * c + lax.broadcasted_iota(I32, (tm, c), 1)).astype(F32)
        unpack = jnp.where(routed_col & (rank_col == slot_cols), 1.0, 0.0).astype(BF16)
        o_ref[...] += jnp.dot(jnp.concatenate([unpack, unpack], axis=1), jnp.concatenate([y_hi, y_lo], axis=0),
                              preferred_element_type=F32)
        return carry

    lax.fori_loop(0, n_chunks, chunk, 0)


def _moe(x2d, g, router, wg, wu, wd):
    n, d = x2d.shape
    n_exp, _, d_ff = wg.shape
    assert TOP_K_EXPERTS == 2 and n_exp <= V7X_LANES
    tm = min(1024, n)
    rows_per_chunk = min(V7X_LANES, tm)
    assert n % tm == 0 and tm % V7X_LANES == 0
    n_tiles = n // tm
    assert n_tiles == 1 or n_tiles >= 4
    fc = d_ff
    for cand in (896, 1024, 768, 512):
        if d_ff % cand == 0:
            fc = cand
            break
    r_pad = jnp.concatenate([router, jnp.zeros((d, V7X_LANES - n_exp), router.dtype)], axis=1)
    tok = lambda width: pl.BlockSpec((tm, width), lambda i: (i, 0))
    per_tile = pl.BlockSpec((None, V7X_LANES, tm), lambda i: (i, 0, 0))
    xn, gate, gate_t, rank, rank_t = pl.pallas_call(
        functools.partial(_moe_route_body, n_exp=n_exp, tm=tm),
        out_shape=(jax.ShapeDtypeStruct((n, d), BF16),
                   jax.ShapeDtypeStruct((n, V7X_LANES), F32), jax.ShapeDtypeStruct((n_tiles, V7X_LANES, tm), F32),
                   jax.ShapeDtypeStruct((n, V7X_LANES), F32), jax.ShapeDtypeStruct((n_tiles, V7X_LANES, tm), F32)),
        grid=(n_tiles,),
        in_specs=[tok(d), _const_spec((1, d)), _const_spec((d, V7X_LANES))],
        out_specs=(tok(d), tok(V7X_LANES), per_tile, tok(V7X_LANES), per_tile),
        compiler_params=_params("parallel"),
        name="moe_route",
    )(x2d, g.reshape(1, d), r_pad)

    tok2 = lambda width: pl.BlockSpec((tm, width), lambda e, i: (i, 0))
    per_tile2 = pl.BlockSpec((None, V7X_LANES, tm), lambda e, i: (i, 0, 0))
    w_spec = lambda rows, cols: pl.BlockSpec((None, rows, cols), lambda e, i: (e, 0, 0), pipeline_mode=pl.Buffered(1))
    return pl.pallas_call(
        functools.partial(_moe_expert_body, fc=fc, tm=tm, rows_per_chunk=rows_per_chunk, single_tile=n_tiles == 1),
        out_shape=jax.ShapeDtypeStruct((n, d), F32),
        grid=(n_exp, n_tiles),
        in_specs=[tok2(d), tok2(d), tok2(V7X_LANES), per_tile2, tok2(V7X_LANES), per_tile2,
                  w_spec(d, d_ff), w_spec(d, d_ff), w_spec(d_ff, d)],
        out_specs=tok2(d),
        input_output_aliases={0: 0},
        compiler_params=_params("arbitrary", "arbitrary"),
        name="moe_experts",
    )(x2d, xn, gate, gate_t, rank, rank_t, wg.astype(BF16), wu.astype(BF16), wd.astype(BF16))


def _final_norm_body(x_ref, g_ref, o_ref):
    o_ref[...] = _rms(x_ref[...], g_ref[...])


def _final_norm(x2d, g):
    n, d = x2d.shape
    tm = min(1024, n)
    return pl.pallas_call(
        _final_norm_body,
        out_shape=jax.ShapeDtypeStruct((n, d), F32),
        grid=(n // tm,),
        in_specs=[pl.BlockSpec((tm, d), lambda i: (i, 0)), _const_spec((1, d))],
        out_specs=pl.BlockSpec((tm, d), lambda i: (i, 0)),
        compiler_params=_params("parallel"),
        name="final_norm",
    )(x2d, g.reshape(1, d))


def _layer(x, past, h0, lw, *, layer_idx, precise_ssm):
    b, t, d = x.shape
    d_ssm = lw["ssm_d"].shape[0]
    grouped = (b, t) if past is None else (1, b * t)
    group = lambda a: a.reshape(grouped + a.shape[2:])
    ungroup = lambda a: a.reshape((b, t) + a.shape[2:])
    qt, k, v, iqt, ik, iwt, u, ga, gb = _inproj(group(x), lw["norm_mix_g"], lw["w_in_packed"], d_ssm)
    k, v, ik, u = ungroup(k), ungroup(v), ungroup(ik), ungroup(u)

    if past is None:
        kb = min(512, t)
        k_all, v_all, ik_all, s_valid = k, v, ik, t
    else:
        per_batch = lambda a: jnp.pad(a.reshape(a.shape[1], b, t).transpose(1, 0, 2),
                                      ((0, 0), (0, 0), (0, (-t) % V7X_LANES)))
        qt, iqt, iwt = per_batch(qt), per_batch(iqt), per_batch(iwt)
        ck, cv, cik = past
        k_all = jnp.concatenate([ck.reshape(b, -1, D_ATTN), k], axis=1)
        v_all = jnp.concatenate([cv.reshape(b, -1, D_ATTN), v], axis=1)
        ik_all = jnp.concatenate([cik, ik], axis=1)
        s_valid = k_all.shape[1]
        kb = 512
        pad = (-s_valid) % kb
        padf = lambda a: jnp.pad(a, ((0, 0), (0, pad), (0, 0)))
        k_all, v_all, ik_all = padf(k_all), padf(v_all), padf(ik_all)
    att_t = _dsa(qt, iqt, iwt, k_all, v_all, ik_all, n_keep=min(TOPK_MAX, s_valid // 4), causal=past is None,
                 n_valid=s_valid, kb=kb)
    if past is not None:
        att_t = att_t[:, :, :t].transpose(1, 0, 2).reshape(1, D_ATTN, b * t)

    ssm_tm, h_re, h_im = _ssm(u.transpose(1, 0, 2), h0[0], h0[1], lw["disc"], lw["ssm_b_re"], lw["ssm_b_im"], lw["ssm_c_re"],
                              lw["ssm_c_im"], lw["ssm_d"], lw["glu_w"], lw["glu_b"], precise=precise_ssm)
    x = _merge(att_t, group(ssm_tm.transpose(1, 0, 2)), ga, gb, group(x), lw["w_branch_attn"], lw["w_branch_ssm"],
               lw["w_out"])
    x2d = x.reshape(b * t, d)
    if layer_idx % 2 == 0:
        x2d = _ffn(x2d, lw["norm_ffn_g"], lw["ffn_w_gate"], lw["ffn_w_up"], lw["ffn_w_down"])
    else:
        x2d = _moe(x2d, lw["norm_ffn_g"], lw["moe_router"], lw["moe_w_gate"], lw["moe_w_up"], lw["moe_w_down"])
    g_shape = (b, -1, SSM_STATE)
    return (x2d.reshape(b, t, d), k.reshape(b, t, N_HEADS, HEAD_DIM), v.reshape(b, t, N_HEADS, HEAD_DIM), ik,
            h_re.reshape(g_shape), h_im.reshape(g_shape))


def kernel(x_prompt, x_sample, cache_k, cache_v, cache_idx_k, state_ssm_re, state_ssm_im, norm_mix_g, w_in,
           ssm_a_re, ssm_a_im, ssm_log_dt, ssm_b_re, ssm_b_im, ssm_c_re, ssm_c_im, ssm_d, glu_w, glu_b,
           w_branch_attn, w_branch_ssm, w_out, norm_ffn_g, ffn_w_gate, ffn_w_up, ffn_w_down, moe_router,
           moe_w_gate, moe_w_up, moe_w_down, final_norm_g):
    depth = w_in.shape[0]
    d_model = x_prompt.shape[-1]
    d_ssm = ssm_d.shape[1]
    xp, xs = x_prompt, x_sample
    outs_p, outs_s = [], []
    for l in range(depth):
        lw = dict(
            norm_mix_g=norm_mix_g[l], w_in_packed=_pack_w_in(w_in[l], d_ssm, d_model),
            disc=_ssm_discretize(ssm_a_re[l], ssm_a_im[l], ssm_log_dt[l]),
            ssm_b_re=ssm_b_re[l], ssm_b_im=ssm_b_im[l], ssm_c_re=ssm_c_re[l], ssm_c_im=ssm_c_im[l],
            ssm_d=ssm_d[l], glu_w=glu_w[l], glu_b=glu_b[l],
            w_branch_attn=w_branch_attn[l], w_branch_ssm=w_branch_ssm[l], w_out=w_out[l], norm_ffn_g=norm_ffn_g[l])
        i = l // 2
        if l % 2 == 0:
            lw.update(ffn_w_gate=ffn_w_gate[i], ffn_w_up=ffn_w_up[i], ffn_w_down=ffn_w_down[i])
        else:
            lw.update(moe_router=moe_router[i], moe_w_gate=moe_w_gate[i], moe_w_up=moe_w_up[i],
                      moe_w_down=moe_w_down[i])
        n_state = ssm_a_re.shape[1] * ssm_a_re.shape[2]
        zeros = jnp.zeros((xp.shape[0], n_state), F32)
        xp, *rest_p = _layer(xp, None, (zeros, zeros), lw, layer_idx=l, precise_ssm=False)
        outs_p.append(rest_p)
        h0 = (state_ssm_re[l].reshape(xs.shape[0], n_state), state_ssm_im[l].reshape(xs.shape[0], n_state))
        xs, *rest_s = _layer(xs, (cache_k[l], cache_v[l], cache_idx_k[l]), h0, lw, layer_idx=l, precise_ssm=True)
        outs_s.append(rest_s)

    y_prompt = _final_norm(xp.reshape(-1, d_model), final_norm_g).reshape(xp.shape)
    y_sample = _final_norm(xs.reshape(-1, d_model), final_norm_g).reshape(xs.shape)
    stack = lambda outs, j: jnp.stack([o[j] for o in outs], 0)
    return (y_prompt, y_sample,
            stack(outs_p, 0), stack(outs_p, 1), stack(outs_p, 2), stack(outs_p, 3), stack(outs_p, 4),
            stack(outs_s, 0), stack(outs_s, 1), stack(outs_s, 2), stack(outs_s, 3), stack(outs_s, 4))
```

```python
import functools
import math

import jax
import jax.numpy as jnp
from jax import lax
from jax.experimental import pallas as pl
from jax.experimental.pallas import tpu as pltpu

F32 = jnp.float32
BF16 = jnp.bfloat16
I32 = jnp.int32

CHUNK = 64
N_HEADS = 8
HEAD_DIM = 64
N_IDX_HEADS = 8
IDX_DIM = 64
TOPK_MAX = 256
SSM_GROUP = 16
SSM_STATE = 64
TOP_K_EXPERTS = 2
RMS_EPS = 1e-6
NEG_INF = -1e30
INT32_MIN = -(2 ** 31)

D_ATTN = N_HEADS * HEAD_DIM
D_IDX = N_IDX_HEADS * IDX_DIM
IDX_W_SCALE = float(D_IDX) ** -0.5
ATTN_SCALE = float(HEAD_DIM) ** -0.5

V7X_LANES = 128
V7X_BF16_SUBLANES = 16
_V_ROWS = HEAD_DIM + V7X_BF16_SUBLANES
V7X_VMEM_LIMIT_BYTES = 56 * 1024 * 1024


def _params(*semantics):
    return pltpu.CompilerParams(dimension_semantics=semantics, vmem_limit_bytes=V7X_VMEM_LIMIT_BYTES)


def _sigmoid(x):
    return 1.0 / (1.0 + jnp.exp(-x))


def _rms(x, g):
    return (x * lax.rsqrt(jnp.mean(x * x, axis=-1, keepdims=True) + RMS_EPS)) * g


def _const_spec(shape):
    return pl.BlockSpec(shape, lambda *_: (0,) * len(shape))


_SEG_Q, _SEG_K, _SEG_V, _SEG_IQ = 0, D_ATTN, 2 * D_ATTN, 3 * D_ATTN
_SEG_SMALL = 3 * D_ATTN + D_IDX


def _pack_w_in(w_in, d_ssm, d_model):
    offs = [0]
    for s in (D_ATTN, D_ATTN, D_ATTN, D_IDX, IDX_DIM, N_IDX_HEADS, d_ssm, d_model, d_model):
        offs.append(offs[-1] + s)
    pad = V7X_LANES - IDX_DIM - N_IDX_HEADS
    small = jnp.concatenate([w_in[:, offs[4]:offs[6]], jnp.zeros((w_in.shape[0], pad), w_in.dtype)], axis=1)
    return jnp.concatenate([w_in[:, :offs[4]], small, w_in[:, offs[6]:]], axis=1).astype(BF16)


def _inproj_body(x_ref, g_ref, w_ref, q_ref, k_ref, v_ref, iq_ref, ik_ref, iw_ref, u_ref, ga_ref, gb_ref, *key_refs,
                 d_ssm, d_model):
    hb = _rms(x_ref[...], g_ref[...]).astype(BF16)

    def proj(lo, n):
        return jnp.dot(hb, w_ref[:, lo:lo + n], preferred_element_type=F32)

    q_ref[...] = proj(_SEG_Q, D_ATTN).T.astype(BF16)
    k = proj(_SEG_K, D_ATTN)
    v = proj(_SEG_V, D_ATTN)
    k_ref[...] = k
    v_ref[...] = v
    iq_ref[...] = proj(_SEG_IQ, D_IDX).T.astype(BF16)
    small = proj(_SEG_SMALL, V7X_LANES)
    ik_ref[...] = small[:, :IDX_DIM]
    iw_ref[...] = small.T[IDX_DIM:IDX_DIM + N_IDX_HEADS, :] * IDX_W_SCALE
    if key_refs:
        k_hm_ref, vt_ref, ik_bf16_ref = key_refs
        vt = v.T
        for h in range(N_HEADS):
            cols = slice(h * HEAD_DIM, (h + 1) * HEAD_DIM)
            k_hm_ref[h] = k[:, cols].astype(BF16)
            vt_ref[h, :HEAD_DIM, :] = vt[cols, :].astype(BF16)
            vt_ref[h, HEAD_DIM:, :] = jnp.ones((vt_ref.shape[1] - HEAD_DIM, vt.shape[1]), BF16)
        ik_bf16_ref[...] = small[:, :IDX_DIM].astype(BF16)
    seg_u = _SEG_SMALL + V7X_LANES
    u_ref[...] = proj(seg_u, d_ssm)
    ga_ref[...] = proj(seg_u + d_ssm, d_model)
    gb_ref[...] = proj(seg_u + d_ssm + d_model, d_model)


def _inproj(x, g, w_packed, d_ssm, key_block=None):
    b, t, d = x.shape
    tm = min(256, t)
    assert t % tm == 0 and tm % V7X_LANES == 0
    n_w = w_packed.shape[1]

    def tok(n, dtype):
        return jax.ShapeDtypeStruct((b, t, n), dtype), pl.BlockSpec((None, tm, n), lambda bi, i: (bi, i, 0))

    def tok_t(n, dtype):
        return jax.ShapeDtypeStruct((b, n, t), dtype), pl.BlockSpec((None, n, tm), lambda bi, i: (bi, 0, i))

    shapes, specs = zip(
        tok_t(D_ATTN, BF16), tok(D_ATTN, F32), tok(D_ATTN, F32), tok_t(D_IDX, BF16), tok(IDX_DIM, F32),
        tok_t(N_IDX_HEADS, F32), tok(d_ssm, F32), tok(d, F32), tok(d, F32))
    if key_block is not None:
        assert key_block % tm == 0 and t % key_block == 0
        per_blk = key_block // tm
        shapes += (jax.ShapeDtypeStruct((b, N_HEADS, t, HEAD_DIM), BF16),
                   jax.ShapeDtypeStruct((b, t // key_block, N_HEADS, _V_ROWS, key_block), BF16),
                   jax.ShapeDtypeStruct((b, t, IDX_DIM), BF16))
        specs += (pl.BlockSpec((None, N_HEADS, tm, HEAD_DIM), lambda bi, i: (bi, 0, i, 0)),
                  pl.BlockSpec((None, None, N_HEADS, _V_ROWS, tm), lambda bi, i: (bi, i // per_blk, 0, 0, i % per_blk)),
                  pl.BlockSpec((None, tm, IDX_DIM), lambda bi, i: (bi, i, 0)))
    return pl.pallas_call(
        functools.partial(_inproj_body, d_ssm=d_ssm, d_model=d),
        out_shape=shapes,
        grid=(b, t // tm),
        in_specs=[pl.BlockSpec((None, tm, d), lambda bi, i: (bi, i, 0)), _const_spec((1, d)), _const_spec((d, n_w))],
        out_specs=specs,
        compiler_params=_params("parallel", "parallel"),
        name="inproj",
    )(x, g.reshape(1, d), w_packed)


def _dsa_body(qt_ref, iqt_ref, iwt_ref, k_ref, vt_ref, ik_ref, o_ref, key_ref, qs_ref, acc_ref, m_ref, s_ref, s2_ref, *,
              tq, kb, n_keep, causal, n_valid, n_kb_total, pos_bits):
    col0 = pl.program_id(1) * tq
    if causal:
        n_kb = (col0 + tq + kb - 1) // kb
        q_pos = col0 + lax.broadcasted_iota(I32, (1, tq), 1)
        shift = CHUNK.bit_length() - 1
        q_limit = ((q_pos >> shift) + 1) << shift
    else:
        n_kb = n_kb_total
        q_limit = jnp.full((1, tq), n_valid, I32)

    ks = min(V7X_LANES, kb)
    cr = min(64, kb)

    def key_pos(j, r0, rows):
        return j * kb + r0 + lax.broadcasted_iota(I32, (rows, tq), 0)

    iwt = iwt_ref[...]

    def idx_block(j, carry):
        for r0 in range(0, kb, ks):
            ik_tile = ik_ref[pl.ds(pl.multiple_of(j * kb + r0, ks), ks), :]
            s = None
            for h in range(N_IDX_HEADS):
                d = jnp.dot(ik_tile, iqt_ref[h * IDX_DIM:(h + 1) * IDX_DIM, :], preferred_element_type=F32)
                term = iwt[h:h + 1, :] * jnp.maximum(d, 0.0)
                s = term if s is None else s + term
            s = jnp.where(s == 0.0, 0.0, s)
            s = jnp.where(key_pos(j, r0, ks) < q_limit, s, NEG_INF)
            bits = lax.bitcast_convert_type(s, I32)
            key_ref[j, r0:r0 + ks, :] = jnp.where(bits < 0, bits ^ jnp.int32(0x7FFFFFFF), bits)
        return carry

    lax.fori_loop(0, n_kb, idx_block, 0)

    def count(pred):
        def body(j, accs):
            accs = list(accs)
            for n, r0 in enumerate(range(0, kb, cr)):
                hit = pred(key_ref[j, r0:r0 + cr, :], functools.partial(key_pos, j, r0, cr))
                accs[n % 2] = accs[n % 2] + jnp.where(hit, 1.0, 0.0)
            return tuple(accs)
        zero = jnp.zeros((cr, tq), F32)
        acc_a, acc_b = lax.fori_loop(0, n_kb, body, (zero, zero))
        return jnp.sum(acc_a + acc_b, axis=0, keepdims=True)

    def bit_step(i, thr):
        cand = thr + lax.shift_left(jnp.int32(1), 31 - i)
        cnt = count(lambda key, pos: key >= cand)
        return jnp.where(cnt >= n_keep, cand, thr)

    thr = lax.fori_loop(0, 32, bit_step, jnp.full((1, tq), INT32_MIN, I32))

    need = n_keep - count(lambda key, pos: key > thr)
    n_eq = count(lambda key, pos: key == thr)
    pos_all = jnp.int32(2 ** pos_bits - 1)

    def tie_search():
        def step(i, end):
            cand = end + lax.shift_left(jnp.int32(1), pos_bits - 1 - i)
            cnt = count(lambda key, pos: (key == thr) & (pos() < cand))
            return jnp.where(cnt <= need, cand, end)
        return lax.fori_loop(0, pos_bits, step, jnp.zeros((1, tq), I32))

    tie_end = lax.cond(jnp.max(n_eq - need) > 0.0, tie_search, lambda: jnp.full((1, tq), pos_all, I32))

    qs_ref[...] = qt_ref[...] * jnp.asarray(ATTN_SCALE, BF16)
    m_ref[...] = jnp.full(m_ref.shape, NEG_INF, F32)
    acc_ref[...] = jnp.zeros(acc_ref.shape, F32)

    def bias_block(j, carry):
        for r0 in range(0, kb, cr):
            key = key_ref[j, r0:r0 + cr, :]
            pos = key_pos(j, r0, cr)
            keep = (pos < q_limit) & ((key > thr) | ((key == thr) & (pos < tie_end)))
            key_ref[j, r0:r0 + cr, :] = lax.bitcast_convert_type(jnp.where(keep, 0.0, NEG_INF), I32)
        return carry

    lax.fori_loop(0, n_kb, bias_block, 0)

    v_rows = acc_ref.shape[1]

    def issue_scores(j, r0, buf):
        k_rows = pl.ds(pl.multiple_of(j * kb + r0, ks), ks)
        for h in range(N_HEADS):
            buf[h] = jnp.dot(k_ref[h, k_rows, :], qs_ref[h * HEAD_DIM:(h + 1) * HEAD_DIM, :],
                             preferred_element_type=F32)

    def att_block(j, carry):
        tiles = list(range(0, kb, ks))
        bufs = [s_ref, s2_ref]
        issue_scores(j, tiles[0], bufs[0])
        for i, r0 in enumerate(tiles):
            if i + 1 < len(tiles):
                issue_scores(j, tiles[i + 1], bufs[(i + 1) % 2])
            bias = lax.bitcast_convert_type(key_ref[j, r0:r0 + ks, :], F32)
            for h in range(N_HEADS):
                s = bufs[i % 2][h] + bias
                m_old = m_ref[h]
                m_new = jnp.maximum(m_old, jnp.max(s, axis=0, keepdims=True))
                alpha = jnp.exp(m_old - m_new)
                p = jnp.exp((s.reshape(ks // 8, 8, tq) - m_new[None]).reshape(ks, tq).astype(BF16))
                pv = jnp.dot(vt_ref[j, h, :, r0:r0 + ks], p, preferred_element_type=F32)
                acc = acc_ref[h].reshape(v_rows // 8, 8, tq) * alpha[None]
                acc_ref[h] = acc.reshape(v_rows, tq) + pv
                m_ref[h] = m_new
        return carry

    lax.fori_loop(0, n_kb, att_block, 0)
    for h in range(N_HEADS):
        rows = slice(h * HEAD_DIM, (h + 1) * HEAD_DIM)
        o_ref[rows, :] = (acc_ref[h, :HEAD_DIM, :] / acc_ref[h, HEAD_DIM:HEAD_DIM + 1, :]).astype(o_ref.dtype)


def _key_layouts(k, v, ik, kb):
    b, s_pad, _ = k.shape
    n_kb_total = s_pad // kb
    k_hm = k.astype(BF16).reshape(b, s_pad, N_HEADS, HEAD_DIM).transpose(0, 2, 1, 3)
    vt_blk = v.astype(BF16).reshape(b, n_kb_total, kb, N_HEADS, HEAD_DIM).transpose(0, 1, 3, 4, 2)
    ones = jnp.ones((b, n_kb_total, N_HEADS, _V_ROWS - HEAD_DIM, kb), BF16)
    return k_hm, jnp.concatenate([vt_blk, ones], axis=3), ik.astype(BF16)


def _dsa(qt, iqt, iwt, k_hm, vt_blk, ik, *, n_keep, causal, n_valid):
    b, _, t_pad = qt.shape
    n_kb_total, v_rows, kb = vt_blk.shape[1], vt_blk.shape[3], vt_blk.shape[4]
    s_pad = n_kb_total * kb
    tq = min(256, t_pad)
    assert t_pad % tq == 0 and tq % V7X_LANES == 0 and (not causal or tq % CHUNK == 0)
    body = functools.partial(_dsa_body, tq=tq, kb=kb, n_keep=float(n_keep), causal=causal, n_valid=n_valid,
                             n_kb_total=n_kb_total, pos_bits=s_pad.bit_length())
    qry = lambda n: pl.BlockSpec((None, n, tq), lambda bi, i: (bi, 0, i))
    return pl.pallas_call(
        body,
        out_shape=jax.ShapeDtypeStruct((b, D_ATTN, t_pad), BF16),
        grid=(b, t_pad // tq),
        in_specs=[qry(D_ATTN), qry(D_IDX), qry(N_IDX_HEADS),
                  pl.BlockSpec((None, N_HEADS, s_pad, HEAD_DIM), lambda bi, i: (bi, 0, 0, 0)),
                  pl.BlockSpec((None, n_kb_total, N_HEADS, v_rows, kb), lambda bi, i: (bi, 0, 0, 0, 0)),
                  pl.BlockSpec((None, s_pad, IDX_DIM), lambda bi, i: (bi, 0, 0))],
        out_specs=qry(D_ATTN),
        scratch_shapes=[pltpu.VMEM((n_kb_total, kb, tq), I32),
                        pltpu.VMEM((D_ATTN, tq), BF16),
                        pltpu.VMEM((N_HEADS, v_rows, tq), F32),
                        pltpu.VMEM((N_HEADS, 8, tq), F32),
                        pltpu.VMEM((N_HEADS, min(V7X_LANES, kb), tq), F32),
                        pltpu.VMEM((N_HEADS, min(V7X_LANES, kb), tq), F32)],
        compiler_params=_params("parallel", "arbitrary"),
        name="dsa",
    )(qt, iqt, iwt, k_hm, vt_blk, ik)


def _ssm_discretize_body(a_re_ref, a_im_ref, log_dt_ref, ab_re_ref, ab_im_ref, bc_re_ref, bc_im_ref):
    a_re, a_im = a_re_ref[...], a_im_ref[...]
    dt = jnp.exp(log_dt_ref[...])
    mag = jnp.exp(dt * a_re)
    ab_re = mag * jnp.cos(dt * a_im)
    ab_im = mag * jnp.sin(dt * a_im)
    den = a_re * a_re + a_im * a_im
    nr = ab_re - 1.0
    ni = ab_im
    ab_re_ref[...] = ab_re
    ab_im_ref[...] = ab_im
    bc_re_ref[...] = (nr * a_re + ni * a_im) / den
    bc_im_ref[...] = (ni * a_re - nr * a_im) / den


def _ssm_discretize(a_re, a_im, log_dt):
    g, p = a_re.shape
    out = jax.ShapeDtypeStruct((g, p), F32)
    return pl.pallas_call(_ssm_discretize_body, out_shape=(out,) * 4, name="ssm_discretize")(
        a_re, a_im, log_dt.reshape(g, 1))


def _block_diag_halves(w, dtype):
    g, r, c = w.shape
    gh = g // 2
    eye = jnp.eye(gh, dtype=w.dtype)
    halves = [jnp.einsum("grc,gh->grhc", w[i * gh:(i + 1) * gh], eye).reshape(gh * r, gh * c) for i in range(2)]
    return jnp.stack(halves, 0).astype(dtype)


def _ssm_body(u_ref, h0r_ref, h0i_ref, abr_ref, abi_ref, bcr_ref, bci_ref, br_ref, bi_ref, cr_ref, ci_ref,
              d_ref, gw_ref, gb_ref, y_ref, hr_ref, hi_ref, xr_ref, xi_ref, *, tb, nb, precise):
    d_ssm = u_ref.shape[-1]
    n_state = hr_ref.shape[-1]
    kh, sh = d_ssm // 2, n_state // 2

    def mm(a, w):
        if precise:
            return jnp.dot(a, w, preferred_element_type=F32, precision=lax.Precision.HIGHEST)
        return jnp.dot(a.astype(BF16), w, preferred_element_type=F32)

    @pl.when(pl.program_id(0) == 0)
    def _():
        hr_ref[...] = h0r_ref[...]
        hi_ref[...] = h0i_ref[...]

    u = u_ref[...].reshape(tb * nb, d_ssm)

    for half in range(2):
        cols = slice(half * sh, (half + 1) * sh)
        uh = u[:, half * kh:(half + 1) * kh]
        bu_r = mm(uh, br_ref[half])
        bu_i = mm(uh, bi_ref[half])
        bc_r, bc_i = bcr_ref[:, cols], bci_ref[:, cols]
        xr_ref[:, cols] = bc_r * bu_r - bc_i * bu_i
        xi_ref[:, cols] = bc_r * bu_i + bc_i * bu_r

    slab = min(512, n_state)
    for s0 in range(0, n_state, slab):
        cols = slice(s0, s0 + slab)
        a_r = jnp.broadcast_to(abr_ref[:, cols], (nb, slab))
        a_i = jnp.broadcast_to(abi_ref[:, cols], (nb, slab))

        def step(t, h):
            h_r, h_i = h
            rows = pl.ds(pl.multiple_of(t * nb, nb), nb)
            n_r = a_r * h_r - a_i * h_i + xr_ref[rows, cols]
            n_i = a_r * h_i + a_i * h_r + xi_ref[rows, cols]
            xr_ref[rows, cols] = n_r
            xi_ref[rows, cols] = n_i
            return n_r, n_i

        h_r, h_i = lax.fori_loop(0, tb, step, (hr_ref[:, cols], hi_ref[:, cols]), unroll=min(8, tb))
        hr_ref[:, cols] = h_r
        hi_ref[:, cols] = h_i

    ys = []
    for half in range(2):
        cols = slice(half * sh, (half + 1) * sh)
        ys.append(mm(xr_ref[:, cols], cr_ref[half]) - mm(xi_ref[:, cols], ci_ref[half]))
    y = jnp.concatenate(ys, axis=1) + d_ref[...] * u
    y = y * (0.5 * (1.0 + jnp.tanh(math.sqrt(2.0 / math.pi) * (y + 0.044715 * (y * y * y)))))
    z = mm(y, gw_ref[...]) + gb_ref[...]
    y_ref[...] = (y * _sigmoid(z)).reshape(tb, nb, d_ssm)


def _ssm(u_tm, h0_re, h0_im, disc, b_re, b_im, c_re, c_im, d, glu_w, glu_b, *, precise):
    t, nb, d_ssm = u_tm.shape
    n_state = h0_re.shape[1]
    assert nb % 8 == 0
    tb = min(32, t)
    wdt = F32 if precise else BF16
    br = _block_diag_halves(jnp.swapaxes(b_re, 1, 2), wdt)
    bi = _block_diag_halves(jnp.swapaxes(b_im, 1, 2), wdt)
    cr = _block_diag_halves(jnp.swapaxes(c_re, 1, 2), wdt)
    ci = _block_diag_halves(jnp.swapaxes(c_im, 1, 2), wdt)
    row = lambda a: a.reshape(1, -1).astype(F32)
    state_spec = _const_spec((nb, n_state))
    vec_state = _const_spec((1, n_state))
    vec_ch = _const_spec((1, d_ssm))
    return pl.pallas_call(
        functools.partial(_ssm_body, tb=tb, nb=nb, precise=precise),
        out_shape=(jax.ShapeDtypeStruct((t, nb, d_ssm), F32),
                   jax.ShapeDtypeStruct((nb, n_state), F32), jax.ShapeDtypeStruct((nb, n_state), F32)),
        grid=(t // tb,),
        in_specs=[pl.BlockSpec((tb, nb, d_ssm), lambda i: (i, 0, 0)), state_spec, state_spec,
                  vec_state, vec_state, vec_state, vec_state,
                  _const_spec(br.shape), _const_spec(bi.shape), _const_spec(cr.shape), _const_spec(ci.shape),
                  vec_ch, _const_spec((d_ssm, d_ssm)), vec_ch],
        out_specs=(pl.BlockSpec((tb, nb, d_ssm), lambda i: (i, 0, 0)), state_spec, state_spec),
        scratch_shapes=[pltpu.VMEM((tb * nb, n_state), F32), pltpu.VMEM((tb * nb, n_state), F32)],
        compiler_params=_params("arbitrary"),
        name="ssm",
    )(u_tm, h0_re, h0_im, row(disc[0]), row(disc[1]), row(disc[2]), row(disc[3]), br, bi, cr, ci,
      row(d), glu_w.astype(wdt), row(glu_b))


def _merge_body(att_t_ref, ssm_ref, ga_ref, gb_ref, x_ref, wpa_ref, wpb_ref, wo_ref, o_ref):
    att = att_t_ref[...].astype(F32).T.astype(BF16)
    a = jnp.dot(att, wpa_ref[...], preferred_element_type=F32)
    s = jnp.dot(ssm_ref[...].astype(BF16), wpb_ref[...], preferred_element_type=F32)
    mix = _sigmoid(ga_ref[...]) * a + _sigmoid(gb_ref[...]) * s
    o_ref[...] = x_ref[...] + jnp.dot(mix.astype(BF16), wo_ref[...], preferred_element_type=F32)


def _merge(att_t, ssm, ga, gb, x, w_pa, w_pb, w_o):
    b, t, d = x.shape
    tm = min(512, t)
    assert t % tm == 0 and tm % V7X_LANES == 0
    tok = lambda n: pl.BlockSpec((None, tm, n), lambda bi, i: (bi, i, 0))
    d_ssm = ssm.shape[-1]
    return pl.pallas_call(
        _merge_body,
        out_shape=jax.ShapeDtypeStruct((b, t, d), F32),
        grid=(b, t // tm),
        in_specs=[pl.BlockSpec((None, D_ATTN, tm), lambda bi, i: (bi, 0, i)), tok(d_ssm), tok(d), tok(d), tok(d),
                  _const_spec(w_pa.shape), _const_spec(w_pb.shape), _const_spec(w_o.shape)],
        out_specs=tok(d),
        compiler_params=_params("parallel", "parallel"),
        name="merge",
    )(att_t, ssm, ga, gb, x, w_pa.astype(BF16), w_pb.astype(BF16), w_o.astype(BF16))


def _ffn_body(x_ref, g_ref, wg_ref, wu_ref, wd_ref, o_ref, xn_ref, *, n_f):
    f = pl.program_id(1)

    @pl.when(f == 0)
    def _():
        xn_ref[...] = _rms(x_ref[...], g_ref[...]).astype(BF16)
        o_ref[...] = x_ref[...]

    xn = xn_ref[...]
    hg = jnp.dot(xn, wg_ref[...], preferred_element_type=F32)
    hu = jnp.dot(xn, wu_ref[...], preferred_element_type=F32)
    h = (hg * _sigmoid(hg)) * hu
    o_ref[...] += jnp.dot(h.astype(BF16), wd_ref[...], preferred_element_type=F32)


def _ffn(x2d, g, wg, wu, wd):
    n, d = x2d.shape
    d_ff = wg.shape[1]
    tm = min(512, n)
    fc = d_ff
    for cand in (1408, 1024, 896, 768, 512):
        if d_ff % cand == 0:
            fc = cand
            break
    n_f = d_ff // fc
    return pl.pallas_call(
        functools.partial(_ffn_body, n_f=n_f),
        out_shape=jax.ShapeDtypeStruct((n, d), F32),
        grid=(n // tm, n_f),
        in_specs=[pl.BlockSpec((tm, d), lambda i, f: (i, 0)), _const_spec((1, d)),
                  pl.BlockSpec((d, fc), lambda i, f: (0, f)), pl.BlockSpec((d, fc), lambda i, f: (0, f)),
                  pl.BlockSpec((fc, d), lambda i, f: (f, 0))],
        out_specs=pl.BlockSpec((tm, d), lambda i, f: (i, 0)),
        scratch_shapes=[pltpu.VMEM((tm, d), BF16)],
        compiler_params=_params("parallel", "arbitrary"),
        name="ffn",
    )(x2d, g.reshape(1, d), wg.astype(BF16), wu.astype(BF16), wd.astype(BF16))


def _moe_route_body(x_ref, g_ref, r_ref, xn_ref, gate_ref, gate_t_ref, rank_ref, rank_t_ref, *, n_exp, tm):
    xn = _rms(x_ref[...], g_ref[...])
    xn_ref[...] = xn.astype(BF16)
    logits = jnp.dot(xn, r_ref[...], preferred_element_type=F32, precision=lax.Precision.HIGHEST)
    lane = lax.broadcasted_iota(I32, logits.shape, 1).astype(F32)
    logits = jnp.where(lane < n_exp, logits, NEG_INF)
    big = float(logits.shape[1])
    m1 = jnp.max(logits, axis=1, keepdims=True)
    i1 = jnp.min(jnp.where(logits == m1, lane, big), axis=1, keepdims=True)
    rest = jnp.where(lane == i1, NEG_INF, logits)
    m2 = jnp.max(rest, axis=1, keepdims=True)
    i2 = jnp.min(jnp.where(rest == m2, lane, big), axis=1, keepdims=True)
    e2 = jnp.exp(m2 - m1)
    w1 = 1.0 / (1.0 + e2)
    w2 = e2 / (1.0 + e2)
    gates = jnp.where(lane == i1, w1, 0.0) + jnp.where(lane == i2, w2, 0.0)
    gates_t = gates.T
    gate_ref[...] = gates
    gate_t_ref[...] = gates_t
    t_row = lax.broadcasted_iota(I32, (tm, tm), 0)
    t_col = lax.broadcasted_iota(I32, (tm, tm), 1)
    member = jnp.where(gates > 0.0, 1.0, 0.0).astype(BF16)
    member_t = jnp.where(gates_t > 0.0, 1.0, 0.0).astype(BF16)
    rank_ref[...] = jnp.dot(jnp.where(t_col < t_row, 1.0, 0.0).astype(BF16), member, preferred_element_type=F32)
    rank_t_ref[...] = jnp.dot(member_t, jnp.where(t_row < t_col, 1.0, 0.0).astype(BF16),
                              preferred_element_type=F32)


def _moe_expert_body(acc_ref, xn_ref, gate_ref, gate_t_ref, rank_ref, rank_t_ref, wg_ref, wu_ref, wd_ref, o_ref, *,
                     fc, tm, rows_per_chunk, single_tile):
    e = pl.program_id(0)
    c = rows_per_chunk
    d_ff = wg_ref.shape[1]
    if single_tile:
        @pl.when(e == 0)
        def _():
            o_ref[...] = acc_ref[...]
    else:
        o_ref[...] = acc_ref[...]

    gate_row = gate_t_ref[pl.ds(e, 1), :]
    rank_row = rank_t_ref[pl.ds(e, 1), :]
    routed_row = gate_row > 0.0
    n_routed = jnp.sum(jnp.where(routed_row, 1.0, 0.0)).astype(I32)
    n_chunks = (n_routed + (c - 1)) // c

    lane = lax.broadcasted_iota(I32, gate_ref.shape, 1)
    rank_col = jnp.sum(jnp.where(lane == e, rank_ref[...], 0.0), axis=1, keepdims=True)
    routed_col = jnp.sum(jnp.where(lane == e, gate_ref[...], 0.0), axis=1, keepdims=True) > 0.0

    def chunk(ci, carry):
        slot_rows = (ci * c + lax.broadcasted_iota(I32, (c, tm), 0)).astype(F32)
        pack = routed_row & (rank_row == slot_rows)
        xs = jnp.dot(jnp.where(pack, 1.0, 0.0).astype(BF16), xn_ref[...], preferred_element_type=F32).astype(BF16)
        y = None
        for f0 in range(0, d_ff, fc):
            hg = jnp.dot(xs, wg_ref[:, f0:f0 + fc], preferred_element_type=F32)
            hu = jnp.dot(xs, wu_ref[:, f0:f0 + fc], preferred_element_type=F32)
            h = ((hg * _sigmoid(hg)) * hu).astype(BF16)
            part = jnp.dot(h, wd_ref[f0:f0 + fc, :], preferred_element_type=F32)
            y = part if y is None else y + part
        y = y * jnp.sum(jnp.where(pack, gate_row, 0.0), axis=1, keepdims=True)
        y_hi = y.astype(BF16)
        y_lo = (y - y_hi.astype(F32)).astype(BF16)
        slot_cols = (ci * c + lax.broadcasted_iota(I32, (tm, c), 1)).astype(F32)
        unpack = jnp.where(routed_col & (rank_col == slot_cols), 1.0, 0.0).astype(BF16)
        o_ref[...] += jnp.dot(jnp.concatenate([unpack, unpack], axis=1), jnp.concatenate([y_hi, y_lo], axis=0),
                              preferred_element_type=F32)
        return carry

    lax.fori_loop(0, n_chunks, chunk, 0)


def _moe(x2d, g, router, wg, wu, wd):
    n, d = x2d.shape
    n_exp, _, d_ff = wg.shape
    assert TOP_K_EXPERTS == 2 and n_exp <= V7X_LANES
    tm = min(1024, n)
    rows_per_chunk = min(V7X_LANES, tm)
    assert n % tm == 0 and tm % V7X_LANES == 0
    n_tiles = n // tm
    assert n_tiles == 1 or n_tiles >= 4
    fc = d_ff
    for cand in (896, 1024, 768, 512):
        if d_ff % cand == 0:
            fc = cand
            break
    r_pad = jnp.concatenate([router, jnp.zeros((d, V7X_LANES - n_exp), router.dtype)], axis=1)
    tok = lambda width: pl.BlockSpec((tm, width), lambda i: (i, 0))
    per_tile = pl.BlockSpec((None, V7X_LANES, tm), lambda i: (i, 0, 0))
    xn, gate, gate_t, rank, rank_t = pl.pallas_call(
        functools.partial(_moe_route_body, n_exp=n_exp, tm=tm),
        out_shape=(jax.ShapeDtypeStruct((n, d), BF16),
                   jax.ShapeDtypeStruct((n, V7X_LANES), F32), jax.ShapeDtypeStruct((n_tiles, V7X_LANES, tm), F32),
                   jax.ShapeDtypeStruct((n, V7X_LANES), F32), jax.ShapeDtypeStruct((n_tiles, V7X_LANES, tm), F32)),
        grid=(n_tiles,),
        in_specs=[tok(d), _const_spec((1, d)), _const_spec((d, V7X_LANES))],
        out_specs=(tok(d), tok(V7X_LANES), per_tile, tok(V7X_LANES), per_tile),
        compiler_params=_params("parallel"),
        name="moe_route",
    )(x2d, g.reshape(1, d), r_pad)

    tok2 = lambda width: pl.BlockSpec((tm, width), lambda e, i: (i, 0))
    per_tile2 = pl.BlockSpec((None, V7X_LANES, tm), lambda e, i: (i, 0, 0))
    w_spec = lambda rows, cols: pl.BlockSpec((None, rows, cols), lambda e, i: (e, 0, 0), pipeline_mode=pl.Buffered(1))
    return pl.pallas_call(
        functools.partial(_moe_expert_body, fc=fc, tm=tm, rows_per_chunk=rows_per_chunk, single_tile=n_tiles == 1),
        out_shape=jax.ShapeDtypeStruct((n, d), F32),
        grid=(n_exp, n_tiles),
        in_specs=[tok2(d), tok2(d), tok2(V7X_LANES), per_tile2, tok2(V7X_LANES), per_tile2,
                  w_spec(d, d_ff), w_spec(d, d_ff), w_spec(d_ff, d)],
        out_specs=tok2(d),
        input_output_aliases={0: 0},
        compiler_params=_params("arbitrary", "arbitrary"),
        name="moe_experts",
    )(x2d, xn, gate, gate_t, rank, rank_t, wg.astype(BF16), wu.astype(BF16), wd.astype(BF16))


def _final_norm_body(x_ref, g_ref, o_ref):
    o_ref[...] = _rms(x_ref[...], g_ref[...])


def _final_norm(x2d, g):
    n, d = x2d.shape
    tm = min(1024, n)
    return pl.pallas_call(
        _final_norm_body,
        out_shape=jax.ShapeDtypeStruct((n, d), F32),
        grid=(n // tm,),
        in_specs=[pl.BlockSpec((tm, d), lambda i: (i, 0)), _const_spec((1, d))],
        out_specs=pl.BlockSpec((tm, d), lambda i: (i, 0)),
        compiler_params=_params("parallel"),
        name="final_norm",
    )(x2d, g.reshape(1, d))


def _layer(x, past, h0, lw, *, layer_idx, precise_ssm):
    b, t, d = x.shape
    d_ssm = lw["ssm_d"].shape[0]
    grouped = (b, t) if past is None else (1, b * t)
    group = lambda a: a.reshape(grouped + a.shape[2:])
    ungroup = lambda a: a.reshape((b, t) + a.shape[2:])
    kb = min(512, t) if past is None else None
    qt, k, v, iqt, ik, iwt, u, ga, gb, *key_layouts = _inproj(group(x), lw["norm_mix_g"], lw["w_in_packed"], d_ssm,
                                                             key_block=kb)
    k, v, ik, u = ungroup(k), ungroup(v), ungroup(ik), ungroup(u)

    if past is None:
        s_valid = t
    else:
        per_batch = lambda a: jnp.pad(a.reshape(a.shape[1], b, t).transpose(1, 0, 2),
                                      ((0, 0), (0, 0), (0, (-t) % V7X_LANES)))
        qt, iqt, iwt = per_batch(qt), per_batch(iqt), per_batch(iwt)
        ck, cv, cik = past
        k_all = jnp.concatenate([ck.reshape(b, -1, D_ATTN), k], axis=1)
        v_all = jnp.concatenate([cv.reshape(b, -1, D_ATTN), v], axis=1)
        ik_all = jnp.concatenate([cik, ik], axis=1)
        s_valid = k_all.shape[1]
        kb = 512
        pad = (-s_valid) % kb
        padf = lambda a: jnp.pad(a, ((0, 0), (0, pad), (0, 0)))
        key_layouts = _key_layouts(padf(k_all), padf(v_all), padf(ik_all), kb)
    att_t = _dsa(qt, iqt, iwt, *key_layouts, n_keep=min(TOPK_MAX, s_valid // 4), causal=past is None,
                 n_valid=s_valid)
    if past is not None:
        att_t = att_t[:, :, :t].transpose(1, 0, 2).reshape(1, D_ATTN, b * t)

    ssm_tm, h_re, h_im = _ssm(u.transpose(1, 0, 2), h0[0], h0[1], lw["disc"], lw["ssm_b_re"], lw["ssm_b_im"], lw["ssm_c_re"],
                              lw["ssm_c_im"], lw["ssm_d"], lw["glu_w"], lw["glu_b"], precise=precise_ssm)
    x = _merge(att_t, group(ssm_tm.transpose(1, 0, 2)), ga, gb, group(x), lw["w_branch_attn"], lw["w_branch_ssm"],
               lw["w_out"])
    x2d = x.reshape(b * t, d)
    if layer_idx % 2 == 0:
        x2d = _ffn(x2d, lw["norm_ffn_g"], lw["ffn_w_gate"], lw["ffn_w_up"], lw["ffn_w_down"])
    else:
        x2d = _moe(x2d, lw["norm_ffn_g"], lw["moe_router"], lw["moe_w_gate"], lw["moe_w_up"], lw["moe_w_down"])
    g_shape = (b, -1, SSM_STATE)
    return (x2d.reshape(b, t, d), k.reshape(b, t, N_HEADS, HEAD_DIM), v.reshape(b, t, N_HEADS, HEAD_DIM), ik,
            h_re.reshape(g_shape), h_im.reshape(g_shape))


def kernel(x_prompt, x_sample, cache_k, cache_v, cache_idx_k, state_ssm_re, state_ssm_im, norm_mix_g, w_in,
           ssm_a_re, ssm_a_im, ssm_log_dt, ssm_b_re, ssm_b_im, ssm_c_re, ssm_c_im, ssm_d, glu_w, glu_b,
           w_branch_attn, w_branch_ssm, w_out, norm_ffn_g, ffn_w_gate, ffn_w_up, ffn_w_down, moe_router,
           moe_w_gate, moe_w_up, moe_w_down, final_norm_g):
    depth = w_in.shape[0]
    d_model = x_prompt.shape[-1]
    d_ssm = ssm_d.shape[1]
    xp, xs = x_prompt, x_sample
    outs_p, outs_s = [], []
    for l in range(depth):
        lw = dict(
            norm_mix_g=norm_mix_g[l], w_in_packed=_pack_w_in(w_in[l], d_ssm, d_model),
            disc=_ssm_discretize(ssm_a_re[l], ssm_a_im[l], ssm_log_dt[l]),
            ssm_b_re=ssm_b_re[l], ssm_b_im=ssm_b_im[l], ssm_c_re=ssm_c_re[l], ssm_c_im=ssm_c_im[l],
            ssm_d=ssm_d[l], glu_w=glu_w[l], glu_b=glu_b[l],
            w_branch_attn=w_branch_attn[l], w_branch_ssm=w_branch_ssm[l], w_out=w_out[l], norm_ffn_g=norm_ffn_g[l])
        i = l // 2
        if l % 2 == 0:
            lw.update(ffn_w_gate=ffn_w_gate[i], ffn_w_up=ffn_w_up[i], ffn_w_down=ffn_w_down[i])
        else:
            lw.update(moe_router=moe_router[i], moe_w_gate=moe_w_gate[i], moe_w_up=moe_w_up[i],
                      moe_w_down=moe_w_down[i])
        n_state = ssm_a_re.shape[1] * ssm_a_re.shape[2]
        zeros = jnp.zeros((xp.shape[0], n_state), F32)
        xp, *rest_p = _layer(xp, None, (zeros, zeros), lw, layer_idx=l, precise_ssm=False)
        outs_p.append(rest_p)
        h0 = (state_ssm_re[l].reshape(xs.shape[0], n_state), state_ssm_im[l].reshape(xs.shape[0], n_state))
        xs, *rest_s = _layer(xs, (cache_k[l], cache_v[l], cache_idx_k[l]), h0, lw, layer_idx=l, precise_ssm=True)
        outs_s.append(rest_s)

    y_prompt = _final_norm(xp.reshape(-1, d_model), final_norm_g).reshape(xp.shape)
    y_sample = _final_norm(xs.reshape(-1, d_model), final_norm_g).reshape(xs.shape)
    stack = lambda outs, j: jnp.stack([o[j] for o in outs], 0)
    return (y_prompt, y_sample,
            stack(outs_p, 0), stack(outs_p, 1), stack(outs_p, 2), stack(outs_p, 3), stack(outs_p, 4),
            stack(outs_s, 0), stack(outs_s, 1), stack(outs_s, 2), stack(outs_s, 3), stack(outs_s, 4))
```

```python
import functools
import math

import jax
import jax.numpy as jnp
from jax import lax
from jax.experimental import pallas as pl
from jax.experimental.pallas import tpu as pltpu

F32 = jnp.float32
BF16 = jnp.bfloat16
I32 = jnp.int32

CHUNK = 64
N_HEADS = 8
HEAD_DIM = 64
N_IDX_HEADS = 8
IDX_DIM = 64
TOPK_MAX = 256
SSM_GROUP = 16
SSM_STATE = 64
TOP_K_EXPERTS = 2
RMS_EPS = 1e-6
NEG_INF = -1e30
INT32_MIN = -(2 ** 31)

D_ATTN = N_HEADS * HEAD_DIM
D_IDX = N_IDX_HEADS * IDX_DIM
IDX_W_SCALE = float(D_IDX) ** -0.5
ATTN_SCALE = float(HEAD_DIM) ** -0.5

V7X_LANES = 128
V7X_BF16_SUBLANES = 16
_V_ROWS = HEAD_DIM + V7X_BF16_SUBLANES
V7X_VMEM_LIMIT_BYTES = 56 * 1024 * 1024


def _params(*semantics):
    return pltpu.CompilerParams(dimension_semantics=semantics, vmem_limit_bytes=V7X_VMEM_LIMIT_BYTES)


def _sigmoid(x):
    return 1.0 / (1.0 + jnp.exp(-x))


def _rms(x, g):
    return (x * lax.rsqrt(jnp.mean(x * x, axis=-1, keepdims=True) + RMS_EPS)) * g


def _const_spec(shape):
    return pl.BlockSpec(shape, lambda *_: (0,) * len(shape))


_SEG_Q, _SEG_K, _SEG_V, _SEG_IQ = 0, D_ATTN, 2 * D_ATTN, 3 * D_ATTN
_SEG_SMALL = 3 * D_ATTN + D_IDX


def _pack_w_in(w_in, d_ssm, d_model):
    offs = [0]
    for s in (D_ATTN, D_ATTN, D_ATTN, D_IDX, IDX_DIM, N_IDX_HEADS, d_ssm, d_model, d_model):
        offs.append(offs[-1] + s)
    pad = V7X_LANES - IDX_DIM - N_IDX_HEADS
    small = jnp.concatenate([w_in[:, offs[4]:offs[6]], jnp.zeros((w_in.shape[0], pad), w_in.dtype)], axis=1)
    return jnp.concatenate([w_in[:, :offs[4]], small, w_in[:, offs[6]:]], axis=1).astype(BF16)


def _inproj_body(x_ref, g_ref, w_ref, q_ref, k_ref, v_ref, iq_ref, ik_ref, iw_ref, u_ref, ga_ref, gb_ref, *key_refs,
                 d_ssm, d_model):
    hb = _rms(x_ref[...], g_ref[...]).astype(BF16)

    def proj(lo, n):
        return jnp.dot(hb, w_ref[:, lo:lo + n], preferred_element_type=F32)

    q_ref[...] = proj(_SEG_Q, D_ATTN).T.astype(BF16)
    k = proj(_SEG_K, D_ATTN)
    v = proj(_SEG_V, D_ATTN)
    k_ref[...] = k
    v_ref[...] = v
    iq_ref[...] = proj(_SEG_IQ, D_IDX).T.astype(BF16)
    small = proj(_SEG_SMALL, V7X_LANES)
    ik_ref[...] = small[:, :IDX_DIM]
    iw_ref[...] = small.T[IDX_DIM:IDX_DIM + N_IDX_HEADS, :] * IDX_W_SCALE
    if key_refs:
        k_hm_ref, vt_ref, ik_bf16_ref = key_refs
        vt = v.T
        for h in range(N_HEADS):
            cols = slice(h * HEAD_DIM, (h + 1) * HEAD_DIM)
            k_hm_ref[h] = k[:, cols].astype(BF16)
            vt_ref[h, :HEAD_DIM, :] = vt[cols, :].astype(BF16)
            vt_ref[h, HEAD_DIM:, :] = jnp.ones((vt_ref.shape[1] - HEAD_DIM, vt.shape[1]), BF16)
        ik_bf16_ref[...] = small[:, :IDX_DIM].astype(BF16)
    seg_u = _SEG_SMALL + V7X_LANES
    u_ref[...] = proj(seg_u, d_ssm)
    ga_ref[...] = proj(seg_u + d_ssm, d_model)
    gb_ref[...] = proj(seg_u + d_ssm + d_model, d_model)


def _inproj(x, g, w_packed, d_ssm, key_block=None):
    b, t, d = x.shape
    tm = min(256, t)
    assert t % tm == 0 and tm % V7X_LANES == 0
    n_w = w_packed.shape[1]

    def tok(n, dtype):
        return jax.ShapeDtypeStruct((b, t, n), dtype), pl.BlockSpec((None, tm, n), lambda bi, i: (bi, i, 0))

    def tok_t(n, dtype):
        return jax.ShapeDtypeStruct((b, n, t), dtype), pl.BlockSpec((None, n, tm), lambda bi, i: (bi, 0, i))

    shapes, specs = zip(
        tok_t(D_ATTN, BF16), tok(D_ATTN, F32), tok(D_ATTN, F32), tok_t(D_IDX, BF16), tok(IDX_DIM, F32),
        tok_t(N_IDX_HEADS, F32), tok(d_ssm, F32), tok(d, F32), tok(d, F32))
    if key_block is not None:
        assert key_block % tm == 0 and t % key_block == 0
        per_blk = key_block // tm
        shapes += (jax.ShapeDtypeStruct((b, N_HEADS, t, HEAD_DIM), BF16),
                   jax.ShapeDtypeStruct((b, t // key_block, N_HEADS, _V_ROWS, key_block), BF16),
                   jax.ShapeDtypeStruct((b, t, IDX_DIM), BF16))
        specs += (pl.BlockSpec((None, N_HEADS, tm, HEAD_DIM), lambda bi, i: (bi, 0, i, 0)),
                  pl.BlockSpec((None, None, N_HEADS, _V_ROWS, tm), lambda bi, i: (bi, i // per_blk, 0, 0, i % per_blk)),
                  pl.BlockSpec((None, tm, IDX_DIM), lambda bi, i: (bi, i, 0)))
    return pl.pallas_call(
        functools.partial(_inproj_body, d_ssm=d_ssm, d_model=d),
        out_shape=shapes,
        grid=(b, t // tm),
        in_specs=[pl.BlockSpec((None, tm, d), lambda bi, i: (bi, i, 0)), _const_spec((1, d)), _const_spec((d, n_w))],
        out_specs=specs,
        compiler_params=_params("parallel", "parallel"),
        name="inproj",
    )(x, g.reshape(1, d), w_packed)


def _dsa_body(qt_ref, iqt_ref, iwt_ref, k_ref, vt_ref, ik_ref, o_ref, key_ref, qs_ref, acc_ref, m_ref, s_ref, s2_ref, *,
              tq, kb, n_keep, causal, n_valid, n_kb_total, pos_bits):
    col0 = pl.program_id(1) * tq
    if causal:
        n_kb = (col0 + tq + kb - 1) // kb
        q_pos = col0 + lax.broadcasted_iota(I32, (1, tq), 1)
        shift = CHUNK.bit_length() - 1
        q_limit = ((q_pos >> shift) + 1) << shift
    else:
        n_kb = n_kb_total
        q_limit = jnp.full((1, tq), n_valid, I32)

    ks = min(V7X_LANES, kb)
    cr = min(64, kb)

    def key_pos(j, r0, rows):
        return j * kb + r0 + lax.broadcasted_iota(I32, (rows, tq), 0)

    iwt = iwt_ref[...]

    def idx_block(j, carry):
        for r0 in range(0, kb, ks):
            ik_tile = ik_ref[pl.ds(pl.multiple_of(j * kb + r0, ks), ks), :]
            s = None
            for h in range(N_IDX_HEADS):
                d = jnp.dot(ik_tile, iqt_ref[h * IDX_DIM:(h + 1) * IDX_DIM, :], preferred_element_type=F32)
                term = iwt[h:h + 1, :] * jnp.maximum(d, 0.0)
                s = term if s is None else s + term
            s = jnp.where(s == 0.0, 0.0, s)
            s = jnp.where(key_pos(j, r0, ks) < q_limit, s, NEG_INF)
            bits = lax.bitcast_convert_type(s, I32)
            key_ref[j, r0:r0 + ks, :] = jnp.where(bits < 0, bits ^ jnp.int32(0x7FFFFFFF), bits)
        return carry

    lax.fori_loop(0, n_kb, idx_block, 0)

    def count(pred):
        def body(j, accs):
            accs = list(accs)
            for n, r0 in enumerate(range(0, kb, cr)):
                hit = pred(key_ref[j, r0:r0 + cr, :], functools.partial(key_pos, j, r0, cr))
                accs[n % 2] = accs[n % 2] + jnp.where(hit, 1.0, 0.0)
            return tuple(accs)
        zero = jnp.zeros((cr, tq), F32)
        acc_a, acc_b = lax.fori_loop(0, n_kb, body, (zero, zero))
        return jnp.sum(acc_a + acc_b, axis=0, keepdims=True)

    def bit_step(i, thr):
        cand = thr + lax.shift_left(jnp.int32(1), 31 - i)
        cnt = count(lambda key, pos: key >= cand)
        return jnp.where(cnt >= n_keep, cand, thr)

    thr = lax.fori_loop(0, 32, bit_step, jnp.full((1, tq), INT32_MIN, I32))

    need = n_keep - count(lambda key, pos: key > thr)
    n_eq = count(lambda key, pos: key == thr)
    pos_all = jnp.int32(2 ** pos_bits - 1)

    def tie_search():
        def step(i, end):
            cand = end + lax.shift_left(jnp.int32(1), pos_bits - 1 - i)
            cnt = count(lambda key, pos: (key == thr) & (pos() < cand))
            return jnp.where(cnt <= need, cand, end)
        return lax.fori_loop(0, pos_bits, step, jnp.zeros((1, tq), I32))

    tie_end = lax.cond(jnp.max(n_eq - need) > 0.0, tie_search, lambda: jnp.full((1, tq), pos_all, I32))

    qs_ref[...] = qt_ref[...] * jnp.asarray(ATTN_SCALE, BF16)
    m_ref[...] = jnp.full(m_ref.shape, NEG_INF, F32)
    acc_ref[...] = jnp.zeros(acc_ref.shape, F32)

    def bias_block(j, carry):
        for r0 in range(0, kb, cr):
            key = key_ref[j, r0:r0 + cr, :]
            pos = key_pos(j, r0, cr)
            keep = (pos < q_limit) & ((key > thr) | ((key == thr) & (pos < tie_end)))
            key_ref[j, r0:r0 + cr, :] = lax.bitcast_convert_type(jnp.where(keep, 0.0, NEG_INF), I32)
        return carry

    lax.fori_loop(0, n_kb, bias_block, 0)

    v_rows = acc_ref.shape[1]

    def issue_scores(j, r0, buf):
        k_rows = pl.ds(pl.multiple_of(j * kb + r0, ks), ks)
        for h in range(N_HEADS):
            buf[h] = jnp.dot(k_ref[h, k_rows, :], qs_ref[h * HEAD_DIM:(h + 1) * HEAD_DIM, :],
                             preferred_element_type=F32)

    def att_block(j, carry):
        tiles = list(range(0, kb, ks))
        bufs = [s_ref, s2_ref]
        issue_scores(j, tiles[0], bufs[0])
        for i, r0 in enumerate(tiles):
            if i + 1 < len(tiles):
                issue_scores(j, tiles[i + 1], bufs[(i + 1) % 2])
            bias = lax.bitcast_convert_type(key_ref[j, r0:r0 + ks, :], F32)
            for h in range(N_HEADS):
                s = bufs[i % 2][h] + bias
                m_old = m_ref[h]
                m_new = jnp.maximum(m_old, jnp.max(s, axis=0, keepdims=True))
                alpha = jnp.exp(m_old - m_new)
                p = jnp.exp((s.reshape(ks // 8, 8, tq) - m_new[None]).reshape(ks, tq).astype(BF16))
                pv = jnp.dot(vt_ref[j, h, :, r0:r0 + ks], p, preferred_element_type=F32)
                acc = acc_ref[h].reshape(v_rows // 8, 8, tq) * alpha[None]
                acc_ref[h] = acc.reshape(v_rows, tq) + pv
                m_ref[h] = m_new
        return carry

    lax.fori_loop(0, n_kb, att_block, 0)
    for h in range(N_HEADS):
        rows = slice(h * HEAD_DIM, (h + 1) * HEAD_DIM)
        o_ref[rows, :] = (acc_ref[h, :HEAD_DIM, :] / acc_ref[h, HEAD_DIM:HEAD_DIM + 1, :]).astype(o_ref.dtype)


def _key_layouts(k, v, ik, kb):
    b, s_pad, _ = k.shape
    n_kb_total = s_pad // kb
    k_hm = k.astype(BF16).reshape(b, s_pad, N_HEADS, HEAD_DIM).transpose(0, 2, 1, 3)
    vt_blk = v.astype(BF16).reshape(b, n_kb_total, kb, N_HEADS, HEAD_DIM).transpose(0, 1, 3, 4, 2)
    ones = jnp.ones((b, n_kb_total, N_HEADS, _V_ROWS - HEAD_DIM, kb), BF16)
    return k_hm, jnp.concatenate([vt_blk, ones], axis=3), ik.astype(BF16)


def _dsa(qt, iqt, iwt, k_hm, vt_blk, ik, *, n_keep, causal, n_valid):
    b, _, t_pad = qt.shape
    n_kb_total, v_rows, kb = vt_blk.shape[1], vt_blk.shape[3], vt_blk.shape[4]
    s_pad = n_kb_total * kb
    tq = min(256, t_pad)
    assert t_pad % tq == 0 and tq % V7X_LANES == 0 and (not causal or tq % CHUNK == 0)
    body = functools.partial(_dsa_body, tq=tq, kb=kb, n_keep=float(n_keep), causal=causal, n_valid=n_valid,
                             n_kb_total=n_kb_total, pos_bits=s_pad.bit_length())
    qry = lambda n: pl.BlockSpec((None, n, tq), lambda bi, i: (bi, 0, i))
    return pl.pallas_call(
        body,
        out_shape=jax.ShapeDtypeStruct((b, D_ATTN, t_pad), BF16),
        grid=(b, t_pad // tq),
        in_specs=[qry(D_ATTN), qry(D_IDX), qry(N_IDX_HEADS),
                  pl.BlockSpec((None, N_HEADS, s_pad, HEAD_DIM), lambda bi, i: (bi, 0, 0, 0)),
                  pl.BlockSpec((None, n_kb_total, N_HEADS, v_rows, kb), lambda bi, i: (bi, 0, 0, 0, 0)),
                  pl.BlockSpec((None, s_pad, IDX_DIM), lambda bi, i: (bi, 0, 0))],
        out_specs=qry(D_ATTN),
        scratch_shapes=[pltpu.VMEM((n_kb_total, kb, tq), I32),
                        pltpu.VMEM((D_ATTN, tq), BF16),
                        pltpu.VMEM((N_HEADS, v_rows, tq), F32),
                        pltpu.VMEM((N_HEADS, 8, tq), F32),
                        pltpu.VMEM((N_HEADS, min(V7X_LANES, kb), tq), F32),
                        pltpu.VMEM((N_HEADS, min(V7X_LANES, kb), tq), F32)],
        compiler_params=_params("parallel", "arbitrary"),
        name="dsa",
    )(qt, iqt, iwt, k_hm, vt_blk, ik)


def _ssm_discretize_body(a_re_ref, a_im_ref, log_dt_ref, ab_re_ref, ab_im_ref, bc_re_ref, bc_im_ref):
    a_re, a_im = a_re_ref[...], a_im_ref[...]
    dt = jnp.exp(log_dt_ref[...])
    mag = jnp.exp(dt * a_re)
    ab_re = mag * jnp.cos(dt * a_im)
    ab_im = mag * jnp.sin(dt * a_im)
    den = a_re * a_re + a_im * a_im
    nr = ab_re - 1.0
    ni = ab_im
    ab_re_ref[...] = ab_re
    ab_im_ref[...] = ab_im
    bc_re_ref[...] = (nr * a_re + ni * a_im) / den
    bc_im_ref[...] = (ni * a_re - nr * a_im) / den


def _ssm_discretize(a_re, a_im, log_dt):
    g, p = a_re.shape
    out = jax.ShapeDtypeStruct((g, p), F32)
    return pl.pallas_call(_ssm_discretize_body, out_shape=(out,) * 4, name="ssm_discretize")(
        a_re, a_im, log_dt.reshape(g, 1))


def _block_diag_halves(w, dtype):
    g, r, c = w.shape
    gh = g // 2
    eye = jnp.eye(gh, dtype=w.dtype)
    halves = [jnp.einsum("grc,gh->grhc", w[i * gh:(i + 1) * gh], eye).reshape(gh * r, gh * c) for i in range(2)]
    return jnp.stack(halves, 0).astype(dtype)


def _ssm_body(u_ref, h0r_ref, h0i_ref, abr_ref, abi_ref, bcr_ref, bci_ref, br_ref, bi_ref, cr_ref, ci_ref,
              d_ref, gw_ref, gb_ref, y_ref, hr_ref, hi_ref, xr_ref, xi_ref, *, tb, nb, precise):
    d_ssm = u_ref.shape[-1]
    n_state = hr_ref.shape[-1]
    kh, sh = d_ssm // 2, n_state // 2

    def mm(a, w):
        if precise:
            return jnp.dot(a, w, preferred_element_type=F32, precision=lax.Precision.HIGHEST)
        return jnp.dot(a.astype(BF16), w, preferred_element_type=F32)

    @pl.when(pl.program_id(0) == 0)
    def _():
        hr_ref[...] = h0r_ref[...]
        hi_ref[...] = h0i_ref[...]

    u = pltpu.einshape("btc->tbc", u_ref[...]).reshape(tb * nb, d_ssm)

    for half in range(2):
        cols = slice(half * sh, (half + 1) * sh)
        uh = u[:, half * kh:(half + 1) * kh]
        bu_r = mm(uh, br_ref[half])
        bu_i = mm(uh, bi_ref[half])
        bc_r, bc_i = bcr_ref[:, cols], bci_ref[:, cols]
        xr_ref[:, cols] = bc_r * bu_r - bc_i * bu_i
        xi_ref[:, cols] = bc_r * bu_i + bc_i * bu_r

    slab = min(512, n_state)
    for s0 in range(0, n_state, slab):
        cols = slice(s0, s0 + slab)
        a_r = jnp.broadcast_to(abr_ref[:, cols], (nb, slab))
        a_i = jnp.broadcast_to(abi_ref[:, cols], (nb, slab))

        def step(t, h):
            h_r, h_i = h
            rows = pl.ds(pl.multiple_of(t * nb, nb), nb)
            n_r = a_r * h_r - a_i * h_i + xr_ref[rows, cols]
            n_i = a_r * h_i + a_i * h_r + xi_ref[rows, cols]
            xr_ref[rows, cols] = n_r
            xi_ref[rows, cols] = n_i
            return n_r, n_i

        h_r, h_i = lax.fori_loop(0, tb, step, (hr_ref[:, cols], hi_ref[:, cols]), unroll=min(8, tb))
        hr_ref[:, cols] = h_r
        hi_ref[:, cols] = h_i

    ys = []
    for half in range(2):
        cols = slice(half * sh, (half + 1) * sh)
        ys.append(mm(xr_ref[:, cols], cr_ref[half]) - mm(xi_ref[:, cols], ci_ref[half]))
    y = jnp.concatenate(ys, axis=1) + d_ref[...] * u
    y = y * (0.5 * (1.0 + jnp.tanh(math.sqrt(2.0 / math.pi) * (y + 0.044715 * (y * y * y)))))
    z = mm(y, gw_ref[...]) + gb_ref[...]
    y_ref[...] = pltpu.einshape("tbc->btc", (y * _sigmoid(z)).reshape(tb, nb, d_ssm))


def _ssm(u, h0_re, h0_im, disc, b_re, b_im, c_re, c_im, d, glu_w, glu_b, *, precise):
    nb, t, d_ssm = u.shape
    n_state = h0_re.shape[1]
    assert nb % 8 == 0
    tb = min(32, t)
    wdt = F32 if precise else BF16
    br = _block_diag_halves(jnp.swapaxes(b_re, 1, 2), wdt)
    bi = _block_diag_halves(jnp.swapaxes(b_im, 1, 2), wdt)
    cr = _block_diag_halves(jnp.swapaxes(c_re, 1, 2), wdt)
    ci = _block_diag_halves(jnp.swapaxes(c_im, 1, 2), wdt)
    row = lambda a: a.reshape(1, -1).astype(F32)
    state_spec = _const_spec((nb, n_state))
    vec_state = _const_spec((1, n_state))
    vec_ch = _const_spec((1, d_ssm))
    return pl.pallas_call(
        functools.partial(_ssm_body, tb=tb, nb=nb, precise=precise),
        out_shape=(jax.ShapeDtypeStruct((nb, t, d_ssm), F32),
                   jax.ShapeDtypeStruct((nb, n_state), F32), jax.ShapeDtypeStruct((nb, n_state), F32)),
        grid=(t // tb,),
        in_specs=[pl.BlockSpec((nb, tb, d_ssm), lambda i: (0, i, 0)), state_spec, state_spec,
                  vec_state, vec_state, vec_state, vec_state,
                  _const_spec(br.shape), _const_spec(bi.shape), _const_spec(cr.shape), _const_spec(ci.shape),
                  vec_ch, _const_spec((d_ssm, d_ssm)), vec_ch],
        out_specs=(pl.BlockSpec((nb, tb, d_ssm), lambda i: (0, i, 0)), state_spec, state_spec),
        scratch_shapes=[pltpu.VMEM((tb * nb, n_state), F32), pltpu.VMEM((tb * nb, n_state), F32)],
        compiler_params=_params("arbitrary"),
        name="ssm",
    )(u, h0_re, h0_im, row(disc[0]), row(disc[1]), row(disc[2]), row(disc[3]), br, bi, cr, ci,
      row(d), glu_w.astype(wdt), row(glu_b))


def _merge_body(att_t_ref, ssm_ref, ga_ref, gb_ref, x_ref, wpa_ref, wpb_ref, wo_ref, o_ref):
    att = att_t_ref[...].astype(F32).T.astype(BF16)
    a = jnp.dot(att, wpa_ref[...], preferred_element_type=F32)
    s = jnp.dot(ssm_ref[...].astype(BF16), wpb_ref[...], preferred_element_type=F32)
    mix = _sigmoid(ga_ref[...]) * a + _sigmoid(gb_ref[...]) * s
    o_ref[...] = x_ref[...] + jnp.dot(mix.astype(BF16), wo_ref[...], preferred_element_type=F32)


def _merge(att_t, ssm, ga, gb, x, w_pa, w_pb, w_o):
    b, t, d = x.shape
    tm = min(512, t)
    assert t % tm == 0 and tm % V7X_LANES == 0
    tok = lambda n: pl.BlockSpec((None, tm, n), lambda bi, i: (bi, i, 0))
    d_ssm = ssm.shape[-1]
    return pl.pallas_call(
        _merge_body,
        out_shape=jax.ShapeDtypeStruct((b, t, d), F32),
        grid=(b, t // tm),
        in_specs=[pl.BlockSpec((None, D_ATTN, tm), lambda bi, i: (bi, 0, i)), tok(d_ssm), tok(d), tok(d), tok(d),
                  _const_spec(w_pa.shape), _const_spec(w_pb.shape), _const_spec(w_o.shape)],
        out_specs=tok(d),
        compiler_params=_params("parallel", "parallel"),
        name="merge",
    )(att_t, ssm, ga, gb, x, w_pa.astype(BF16), w_pb.astype(BF16), w_o.astype(BF16))


def _ffn_body(x_ref, g_ref, wg_ref, wu_ref, wd_ref, o_ref, xn_ref, *, n_f):
    f = pl.program_id(1)

    @pl.when(f == 0)
    def _():
        xn_ref[...] = _rms(x_ref[...], g_ref[...]).astype(BF16)
        o_ref[...] = x_ref[...]

    xn = xn_ref[...]
    hg = jnp.dot(xn, wg_ref[...], preferred_element_type=F32)
    hu = jnp.dot(xn, wu_ref[...], preferred_element_type=F32)
    h = (hg * _sigmoid(hg)) * hu
    o_ref[...] += jnp.dot(h.astype(BF16), wd_ref[...], preferred_element_type=F32)


def _ffn(x2d, g, wg, wu, wd):
    n, d = x2d.shape
    d_ff = wg.shape[1]
    tm = min(512, n)
    fc = d_ff
    for cand in (1408, 1024, 896, 768, 512):
        if d_ff % cand == 0:
            fc = cand
            break
    n_f = d_ff // fc
    return pl.pallas_call(
        functools.partial(_ffn_body, n_f=n_f),
        out_shape=jax.ShapeDtypeStruct((n, d), F32),
        grid=(n // tm, n_f),
        in_specs=[pl.BlockSpec((tm, d), lambda i, f: (i, 0)), _const_spec((1, d)),
                  pl.BlockSpec((d, fc), lambda i, f: (0, f)), pl.BlockSpec((d, fc), lambda i, f: (0, f)),
                  pl.BlockSpec((fc, d), lambda i, f: (f, 0))],
        out_specs=pl.BlockSpec((tm, d), lambda i, f: (i, 0)),
        scratch_shapes=[pltpu.VMEM((tm, d), BF16)],
        compiler_params=_params("parallel", "arbitrary"),
        name="ffn",
    )(x2d, g.reshape(1, d), wg.astype(BF16), wu.astype(BF16), wd.astype(BF16))


def _moe_route_body(x_ref, g_ref, r_ref, xn_ref, gate_ref, gate_t_ref, rank_ref, rank_t_ref, *, n_exp, tm):
    xn = _rms(x_ref[...], g_ref[...])
    xn_ref[...] = xn.astype(BF16)
    logits = jnp.dot(xn, r_ref[...], preferred_element_type=F32, precision=lax.Precision.HIGHEST)
    lane = lax.broadcasted_iota(I32, logits.shape, 1).astype(F32)
    logits = jnp.where(lane < n_exp, logits, NEG_INF)
    big = float(logits.shape[1])
    m1 = jnp.max(logits, axis=1, keepdims=True)
    i1 = jnp.min(jnp.where(logits == m1, lane, big), axis=1, keepdims=True)
    rest = jnp.where(lane == i1, NEG_INF, logits)
    m2 = jnp.max(rest, axis=1, keepdims=True)
    i2 = jnp.min(jnp.where(rest == m2, lane, big), axis=1, keepdims=True)
    e2 = jnp.exp(m2 - m1)
    w1 = 1.0 / (1.0 + e2)
    w2 = e2 / (1.0 + e2)
    gates = jnp.where(lane == i1, w1, 0.0) + jnp.where(lane == i2, w2, 0.0)
    gates_t = gates.T
    gate_ref[...] = gates
    gate_t_ref[...] = gates_t
    t_row = lax.broadcasted_iota(I32, (tm, tm), 0)
    t_col = lax.broadcasted_iota(I32, (tm, tm), 1)
    member = jnp.where(gates > 0.0, 1.0, 0.0).astype(BF16)
    member_t = jnp.where(gates_t > 0.0, 1.0, 0.0).astype(BF16)
    rank_ref[...] = jnp.dot(jnp.where(t_col < t_row, 1.0, 0.0).astype(BF16), member, preferred_element_type=F32)
    rank_t_ref[...] = jnp.dot(member_t, jnp.where(t_row < t_col, 1.0, 0.0).astype(BF16),
                              preferred_element_type=F32)


def _moe_expert_body(acc_ref, xn_ref, gate_ref, gate_t_ref, rank_ref, rank_t_ref, wg_ref, wu_ref, wd_ref, o_ref, *,
                     fc, tm, rows_per_chunk, single_tile):
    e = pl.program_id(0)
    c = rows_per_chunk
    d_ff = wg_ref.shape[1]
    if single_tile:
        @pl.when(e == 0)
        def _():
            o_ref[...] = acc_ref[...]
    else:
        o_ref[...] = acc_ref[...]

    gate_row = gate_t_ref[pl.ds(e, 1), :]
    rank_row = rank_t_ref[pl.ds(e, 1), :]
    routed_row = gate_row > 0.0
    n_routed = jnp.sum(jnp.where(routed_row, 1.0, 0.0)).astype(I32)
    n_chunks = (n_routed + (c - 1)) // c

    lane = lax.broadcasted_iota(I32, gate_ref.shape, 1)
    rank_col = jnp.sum(jnp.where(lane == e, rank_ref[...], 0.0), axis=1, keepdims=True)
    routed_col = jnp.sum(jnp.where(lane == e, gate_ref[...], 0.0), axis=1, keepdims=True) > 0.0

    def chunk(ci, carry):
        slot_rows = (ci * c + lax.broadcasted_iota(I32, (c, tm), 0)).astype(F32)
        pack = routed_row & (rank_row == slot_rows)
        xs = jnp.dot(jnp.where(pack, 1.0, 0.0).astype(BF16), xn_ref[...], preferred_element_type=F32).astype(BF16)
        y = None
        for f0 in range(0, d_ff, fc):
            hg = jnp.dot(xs, wg_ref[:, f0:f0 + fc], preferred_element_type=F32)
            hu = jnp.dot(xs, wu_ref[:, f0:f0 + fc], preferred_element_type=F32)
            h = ((hg * _sigmoid(hg)) * hu).astype(BF16)
            part = jnp.dot(h, wd_ref[f0:f0 + fc, :], preferred_element_type=F32)
            y = part if y is None else y + part
        y = y * jnp.sum(jnp.where(pack, gate_row, 0.0), axis=1, keepdims=True)
        y_hi = y.astype(BF16)
        y_lo = (y - y_hi.astype(F32)).astype(BF16)
        slot_cols = (ci * c + lax.broadcasted_iota(I32, (tm, c), 1)).astype(F32)
        unpack = jnp.where(routed_col & (rank_col == slot_cols), 1.0, 0.0).astype(BF16)
        o_ref[...] += jnp.dot(jnp.concatenate([unpack, unpack], axis=1), jnp.concatenate([y_hi, y_lo], axis=0),
                              preferred_element_type=F32)
        return carry

    lax.fori_loop(0, n_chunks, chunk, 0)


def _moe(x2d, g, router, wg, wu, wd):
    n, d = x2d.shape
    n_exp, _, d_ff = wg.shape
    assert TOP_K_EXPERTS == 2 and n_exp <= V7X_LANES
    tm = min(1024, n)
    rows_per_chunk = min(V7X_LANES, tm)
    assert n % tm == 0 and tm % V7X_LANES == 0
    n_tiles = n // tm
    assert n_tiles == 1 or n_tiles >= 4
    fc = d_ff
    for cand in (896, 1024, 768, 512):
        if d_ff % cand == 0:
            fc = cand
            break
    r_pad = jnp.concatenate([router, jnp.zeros((d, V7X_LANES - n_exp), router.dtype)], axis=1)
    tok = lambda width: pl.BlockSpec((tm, width), lambda i: (i, 0))
    per_tile = pl.BlockSpec((None, V7X_LANES, tm), lambda i: (i, 0, 0))
    xn, gate, gate_t, rank, rank_t = pl.pallas_call(
        functools.partial(_moe_route_body, n_exp=n_exp, tm=tm),
        out_shape=(jax.ShapeDtypeStruct((n, d), BF16),
                   jax.ShapeDtypeStruct((n, V7X_LANES), F32), jax.ShapeDtypeStruct((n_tiles, V7X_LANES, tm), F32),
                   jax.ShapeDtypeStruct((n, V7X_LANES), F32), jax.ShapeDtypeStruct((n_tiles, V7X_LANES, tm), F32)),
        grid=(n_tiles,),
        in_specs=[tok(d), _const_spec((1, d)), _const_spec((d, V7X_LANES))],
        out_specs=(tok(d), tok(V7X_LANES), per_tile, tok(V7X_LANES), per_tile),
        compiler_params=_params("parallel"),
        name="moe_route",
    )(x2d, g.reshape(1, d), r_pad)

    tok2 = lambda width: pl.BlockSpec((tm, width), lambda e, i: (i, 0))
    per_tile2 = pl.BlockSpec((None, V7X_LANES, tm), lambda e, i: (i, 0, 0))
    w_spec = lambda rows, cols: pl.BlockSpec((None, rows, cols), lambda e, i: (e, 0, 0), pipeline_mode=pl.Buffered(1))
    return pl.pallas_call(
        functools.partial(_moe_expert_body, fc=fc, tm=tm, rows_per_chunk=rows_per_chunk, single_tile=n_tiles == 1),
        out_shape=jax.ShapeDtypeStruct((n, d), F32),
        grid=(n_exp, n_tiles),
        in_specs=[tok2(d), tok2(d), tok2(V7X_LANES), per_tile2, tok2(V7X_LANES), per_tile2,
                  w_spec(d, d_ff), w_spec(d, d_ff), w_spec(d_ff, d)],
        out_specs=tok2(d),
        input_output_aliases={0: 0},
        compiler_params=_params("arbitrary", "arbitrary"),
        name="moe_experts",
    )(x2d, xn, gate, gate_t, rank, rank_t, wg.astype(BF16), wu.astype(BF16), wd.astype(BF16))


def _final_norm_body(x_ref, g_ref, o_ref):
    o_ref[...] = _rms(x_ref[...], g_ref[...])


def _final_norm(x2d, g):
    n, d = x2d.shape
    tm = min(1024, n)
    return pl.pallas_call(
        _final_norm_body,
        out_shape=jax.ShapeDtypeStruct((n, d), F32),
        grid=(n // tm,),
        in_specs=[pl.BlockSpec((tm, d), lambda i: (i, 0)), _const_spec((1, d))],
        out_specs=pl.BlockSpec((tm, d), lambda i: (i, 0)),
        compiler_params=_params("parallel"),
        name="final_norm",
    )(x2d, g.reshape(1, d))


def _layer(x, past, h0, lw, *, layer_idx, precise_ssm):
    b, t, d = x.shape
    d_ssm = lw["ssm_d"].shape[0]
    grouped = (b, t) if past is None else (1, b * t)
    group = lambda a: a.reshape(grouped + a.shape[2:])
    ungroup = lambda a: a.reshape((b, t) + a.shape[2:])
    kb = min(512, t) if past is None else None
    qt, k, v, iqt, ik, iwt, u, ga, gb, *key_layouts = _inproj(group(x), lw["norm_mix_g"], lw["w_in_packed"], d_ssm,
                                                             key_block=kb)
    k, v, ik, u = ungroup(k), ungroup(v), ungroup(ik), ungroup(u)

    if past is None:
        s_valid = t
    else:
        per_batch = lambda a: jnp.pad(a.reshape(a.shape[1], b, t).transpose(1, 0, 2),
                                      ((0, 0), (0, 0), (0, (-t) % V7X_LANES)))
        qt, iqt, iwt = per_batch(qt), per_batch(iqt), per_batch(iwt)
        ck, cv, cik = past
        k_all = jnp.concatenate([ck.reshape(b, -1, D_ATTN), k], axis=1)
        v_all = jnp.concatenate([cv.reshape(b, -1, D_ATTN), v], axis=1)
        ik_all = jnp.concatenate([cik, ik], axis=1)
        s_valid = k_all.shape[1]
        kb = 512
        pad = (-s_valid) % kb
        padf = lambda a: jnp.pad(a, ((0, 0), (0, pad), (0, 0)))
        key_layouts = _key_layouts(padf(k_all), padf(v_all), padf(ik_all), kb)
    att_t = _dsa(qt, iqt, iwt, *key_layouts, n_keep=min(TOPK_MAX, s_valid // 4), causal=past is None,
                 n_valid=s_valid)
    if past is not None:
        att_t = att_t[:, :, :t].transpose(1, 0, 2).reshape(1, D_ATTN, b * t)

    ssm, h_re, h_im = _ssm(u, h0[0], h0[1], lw["disc"], lw["ssm_b_re"], lw["ssm_b_im"], lw["ssm_c_re"],
                           lw["ssm_c_im"], lw["ssm_d"], lw["glu_w"], lw["glu_b"], precise=precise_ssm)
    x = _merge(att_t, group(ssm), ga, gb, group(x), lw["w_branch_attn"], lw["w_branch_ssm"], lw["w_out"])
    x2d = x.reshape(b * t, d)
    if layer_idx % 2 == 0:
        x2d = _ffn(x2d, lw["norm_ffn_g"], lw["ffn_w_gate"], lw["ffn_w_up"], lw["ffn_w_down"])
    else:
        x2d = _moe(x2d, lw["norm_ffn_g"], lw["moe_router"], lw["moe_w_gate"], lw["moe_w_up"], lw["moe_w_down"])
    g_shape = (b, -1, SSM_STATE)
    return (x2d.reshape(b, t, d), k.reshape(b, t, N_HEADS, HEAD_DIM), v.reshape(b, t, N_HEADS, HEAD_DIM), ik,
            h_re.reshape(g_shape), h_im.reshape(g_shape))


def kernel(x_prompt, x_sample, cache_k, cache_v, cache_idx_k, state_ssm_re, state_ssm_im, norm_mix_g, w_in,
           ssm_a_re, ssm_a_im, ssm_log_dt, ssm_b_re, ssm_b_im, ssm_c_re, ssm_c_im, ssm_d, glu_w, glu_b,
           w_branch_attn, w_branch_ssm, w_out, norm_ffn_g, ffn_w_gate, ffn_w_up, ffn_w_down, moe_router,
           moe_w_gate, moe_w_up, moe_w_down, final_norm_g):
    depth = w_in.shape[0]
    d_model = x_prompt.shape[-1]
    d_ssm = ssm_d.shape[1]
    xp, xs = x_prompt, x_sample
    outs_p, outs_s = [], []
    for l in range(depth):
        lw = dict(
            norm_mix_g=norm_mix_g[l], w_in_packed=_pack_w_in(w_in[l], d_ssm, d_model),
            disc=_ssm_discretize(ssm_a_re[l], ssm_a_im[l], ssm_log_dt[l]),
            ssm_b_re=ssm_b_re[l], ssm_b_im=ssm_b_im[l], ssm_c_re=ssm_c_re[l], ssm_c_im=ssm_c_im[l],
            ssm_d=ssm_d[l], glu_w=glu_w[l], glu_b=glu_b[l],
            w_branch_attn=w_branch_attn[l], w_branch_ssm=w_branch_ssm[l], w_out=w_out[l], norm_ffn_g=norm_ffn_g[l])
        i = l // 2
        if l % 2 == 0:
            lw.update(ffn_w_gate=ffn_w_gate[i], ffn_w_up=ffn_w_up[i], ffn_w_down=ffn_w_down[i])
        else:
            lw.update(moe_router=moe_router[i], moe_w_gate=moe_w_gate[i], moe_w_up=moe_w_up[i],
                      moe_w_down=moe_w_down[i])
        n_state = ssm_a_re.shape[1] * ssm_a_re.shape[2]
        zeros = jnp.zeros((xp.shape[0], n_state), F32)
        xp, *rest_p = _layer(xp, None, (zeros, zeros), lw, layer_idx=l, precise_ssm=False)
        outs_p.append(rest_p)
        h0 = (state_ssm_re[l].reshape(xs.shape[0], n_state), state_ssm_im[l].reshape(xs.shape[0], n_state))
        xs, *rest_s = _layer(xs, (cache_k[l], cache_v[l], cache_idx_k[l]), h0, lw, layer_idx=l, precise_ssm=True)
        outs_s.append(rest_s)

    y_prompt = _final_norm(xp.reshape(-1, d_model), final_norm_g).reshape(xp.shape)
    y_sample = _final_norm(xs.reshape(-1, d_model), final_norm_g).reshape(xs.shape)
    stack = lambda outs, j: jnp.stack([o[j] for o in outs], 0)
    return (y_prompt, y_sample,
            stack(outs_p, 0), stack(outs_p, 1), stack(outs_p, 2), stack(outs_p, 3), stack(outs_p, 4),
            stack(outs_s, 0), stack(outs_s, 1), stack(outs_s, 2), stack(outs_s, 3), stack(outs_s, 4))
```

```python
import functools
import math

import jax
import jax.numpy as jnp
from jax import lax
from jax.experimental import pallas as pl
from jax.experimental.pallas import tpu as pltpu

F32 = jnp.float32
BF16 = jnp.bfloat16
I32 = jnp.int32

CHUNK = 64
N_HEADS = 8
HEAD_DIM = 64
N_IDX_HEADS = 8
IDX_DIM = 64
TOPK_MAX = 256
SSM_GROUP = 16
SSM_STATE = 64
TOP_K_EXPERTS = 2
RMS_EPS = 1e-6
NEG_INF = -1e30
INT32_MIN = -(2 ** 31)

D_ATTN = N_HEADS * HEAD_DIM
D_IDX = N_IDX_HEADS * IDX_DIM
IDX_W_SCALE = float(D_IDX) ** -0.5
ATTN_SCALE = float(HEAD_DIM) ** -0.5

V7X_LANES = 128
V7X_BF16_SUBLANES = 16
_V_ROWS = HEAD_DIM + V7X_BF16_SUBLANES
V7X_VMEM_LIMIT_BYTES = 56 * 1024 * 1024


def _params(*semantics):
    return pltpu.CompilerParams(dimension_semantics=semantics, vmem_limit_bytes=V7X_VMEM_LIMIT_BYTES)


def _sigmoid(x):
    return 1.0 / (1.0 + jnp.exp(-x))


def _rms(x, g):
    return (x * lax.rsqrt(jnp.mean(x * x, axis=-1, keepdims=True) + RMS_EPS)) * g


def _const_spec(shape):
    return pl.BlockSpec(shape, lambda *_: (0,) * len(shape))


_SEG_Q, _SEG_K, _SEG_V, _SEG_IQ = 0, D_ATTN, 2 * D_ATTN, 3 * D_ATTN
_SEG_SMALL = 3 * D_ATTN + D_IDX


def _pack_w_in(w_in, d_ssm, d_model):
    offs = [0]
    for s in (D_ATTN, D_ATTN, D_ATTN, D_IDX, IDX_DIM, N_IDX_HEADS, d_ssm, d_model, d_model):
        offs.append(offs[-1] + s)
    pad = V7X_LANES - IDX_DIM - N_IDX_HEADS
    small = jnp.concatenate([w_in[:, offs[4]:offs[6]], jnp.zeros((w_in.shape[0], pad), w_in.dtype)], axis=1)
    return jnp.concatenate([w_in[:, :offs[4]], small, w_in[:, offs[6]:]], axis=1).astype(BF16)


def _inproj_body(x_ref, g_ref, w_ref, q_ref, k_ref, v_ref, iq_ref, ik_ref, iw_ref, u_ref, ga_ref, gb_ref, *key_refs,
                 d_ssm, d_model):
    hb = _rms(x_ref[...], g_ref[...]).astype(BF16)

    def proj(lo, n):
        return jnp.dot(hb, w_ref[:, lo:lo + n], preferred_element_type=F32)

    q_ref[...] = proj(_SEG_Q, D_ATTN).T.astype(BF16)
    k = proj(_SEG_K, D_ATTN)
    v = proj(_SEG_V, D_ATTN)
    k_ref[...] = k
    v_ref[...] = v
    iq_ref[...] = proj(_SEG_IQ, D_IDX).T.astype(BF16)
    small = proj(_SEG_SMALL, V7X_LANES)
    ik_ref[...] = small[:, :IDX_DIM]
    iw_ref[...] = small.T[IDX_DIM:IDX_DIM + N_IDX_HEADS, :] * IDX_W_SCALE
    if key_refs:
        k_hm_ref, vt_ref, ik_bf16_ref = key_refs
        vt = v.T
        for h in range(N_HEADS):
            cols = slice(h * HEAD_DIM, (h + 1) * HEAD_DIM)
            k_hm_ref[h] = k[:, cols].astype(BF16)
            vt_ref[h, :HEAD_DIM, :] = vt[cols, :].astype(BF16)
            vt_ref[h, HEAD_DIM:, :] = jnp.ones((vt_ref.shape[1] - HEAD_DIM, vt.shape[1]), BF16)
        ik_bf16_ref[...] = small[:, :IDX_DIM].astype(BF16)
    seg_u = _SEG_SMALL + V7X_LANES
    u_ref[...] = proj(seg_u, d_ssm)
    ga_ref[...] = proj(seg_u + d_ssm, d_model)
    gb_ref[...] = proj(seg_u + d_ssm + d_model, d_model)


def _inproj(x, g, w_packed, d_ssm, key_block=None):
    b, t, d = x.shape
    tm = min(256, t)
    assert t % tm == 0 and tm % V7X_LANES == 0
    n_w = w_packed.shape[1]

    def tok(n, dtype):
        return jax.ShapeDtypeStruct((b, t, n), dtype), pl.BlockSpec((None, tm, n), lambda bi, i: (bi, i, 0))

    def tok_t(n, dtype):
        return jax.ShapeDtypeStruct((b, n, t), dtype), pl.BlockSpec((None, n, tm), lambda bi, i: (bi, 0, i))

    shapes, specs = zip(
        tok_t(D_ATTN, BF16), tok(D_ATTN, F32), tok(D_ATTN, F32), tok_t(D_IDX, BF16), tok(IDX_DIM, F32),
        tok_t(N_IDX_HEADS, F32), tok(d_ssm, F32), tok(d, F32), tok(d, F32))
    if key_block is not None:
        assert key_block % tm == 0 and t % key_block == 0
        per_blk = key_block // tm
        shapes += (jax.ShapeDtypeStruct((b, N_HEADS, t, HEAD_DIM), BF16),
                   jax.ShapeDtypeStruct((b, t // key_block, N_HEADS, _V_ROWS, key_block), BF16),
                   jax.ShapeDtypeStruct((b, t, IDX_DIM), BF16))
        specs += (pl.BlockSpec((None, N_HEADS, tm, HEAD_DIM), lambda bi, i: (bi, 0, i, 0)),
                  pl.BlockSpec((None, None, N_HEADS, _V_ROWS, tm), lambda bi, i: (bi, i // per_blk, 0, 0, i % per_blk)),
                  pl.BlockSpec((None, tm, IDX_DIM), lambda bi, i: (bi, i, 0)))
    return pl.pallas_call(
        functools.partial(_inproj_body, d_ssm=d_ssm, d_model=d),
        out_shape=shapes,
        grid=(b, t // tm),
        in_specs=[pl.BlockSpec((None, tm, d), lambda bi, i: (bi, i, 0)), _const_spec((1, d)), _const_spec((d, n_w))],
        out_specs=specs,
        compiler_params=_params("parallel", "parallel"),
        name="inproj",
    )(x, g.reshape(1, d), w_packed)


def _dsa_body(qt_ref, iqt_ref, iwt_ref, k_ref, vt_ref, ik_ref, o_ref, key_ref, qs_ref, acc_ref, m_ref, s_ref, s2_ref, *,
              tq, kb, n_keep, causal, n_valid, n_kb_total, pos_bits):
    col0 = pl.program_id(1) * tq
    if causal:
        n_kb = (col0 + tq + kb - 1) // kb
        q_pos = col0 + lax.broadcasted_iota(I32, (1, tq), 1)
        shift = CHUNK.bit_length() - 1
        q_limit = ((q_pos >> shift) + 1) << shift
    else:
        n_kb = n_kb_total
        q_limit = jnp.full((1, tq), n_valid, I32)

    ks = min(V7X_LANES, kb)
    cr = min(64, kb)

    def key_pos(j, r0, rows):
        return j * kb + r0 + lax.broadcasted_iota(I32, (rows, tq), 0)

    iwt = iwt_ref[...]

    def idx_block(j, carry):
        for r0 in range(0, kb, ks):
            ik_tile = ik_ref[pl.ds(pl.multiple_of(j * kb + r0, ks), ks), :]
            s = None
            for h in range(N_IDX_HEADS):
                d = jnp.dot(ik_tile, iqt_ref[h * IDX_DIM:(h + 1) * IDX_DIM, :], preferred_element_type=F32)
                term = iwt[h:h + 1, :] * jnp.maximum(d, 0.0)
                s = term if s is None else s + term
            s = jnp.where(s == 0.0, 0.0, s)
            s = jnp.where(key_pos(j, r0, ks) < q_limit, s, NEG_INF)
            bits = lax.bitcast_convert_type(s, I32)
            key_ref[j, r0:r0 + ks, :] = jnp.where(bits < 0, bits ^ jnp.int32(0x7FFFFFFF), bits)
        return carry

    lax.fori_loop(0, n_kb, idx_block, 0)

    def count(pred):
        def body(j, accs):
            accs = list(accs)
            for n, r0 in enumerate(range(0, kb, cr)):
                hit = pred(key_ref[j, r0:r0 + cr, :], functools.partial(key_pos, j, r0, cr))
                accs[n % 2] = accs[n % 2] + jnp.where(hit, 1.0, 0.0)
            return tuple(accs)
        zero = jnp.zeros((cr, tq), F32)
        acc_a, acc_b = lax.fori_loop(0, n_kb, body, (zero, zero))
        return jnp.sum(acc_a + acc_b, axis=0, keepdims=True)

    def bit_step(i, thr):
        cand = thr + lax.shift_left(jnp.int32(1), 31 - i)
        cnt = count(lambda key, pos: key >= cand)
        return jnp.where(cnt >= n_keep, cand, thr)

    thr = lax.fori_loop(0, 32, bit_step, jnp.full((1, tq), INT32_MIN, I32))

    need = n_keep - count(lambda key, pos: key > thr)
    n_eq = count(lambda key, pos: key == thr)
    pos_all = jnp.int32(2 ** pos_bits - 1)

    def tie_search():
        def step(i, end):
            cand = end + lax.shift_left(jnp.int32(1), pos_bits - 1 - i)
            cnt = count(lambda key, pos: (key == thr) & (pos() < cand))
            return jnp.where(cnt <= need, cand, end)
        return lax.fori_loop(0, pos_bits, step, jnp.zeros((1, tq), I32))

    tie_end = lax.cond(jnp.max(n_eq - need) > 0.0, tie_search, lambda: jnp.full((1, tq), pos_all, I32))

    qs_ref[...] = qt_ref[...] * jnp.asarray(ATTN_SCALE, BF16)
    m_ref[...] = jnp.full(m_ref.shape, NEG_INF, F32)
    acc_ref[...] = jnp.zeros(acc_ref.shape, F32)

    def mask_bias(j, r0):
        key = key_ref[j, r0:r0 + ks, :]
        pos = key_pos(j, r0, ks)
        keep = (pos < q_limit) & ((key > thr) | ((key == thr) & (pos < tie_end)))
        return jnp.where(keep, 0.0, NEG_INF)

    v_rows = acc_ref.shape[1]

    def issue_scores(j, r0, buf):
        k_rows = pl.ds(pl.multiple_of(j * kb + r0, ks), ks)
        for h in range(N_HEADS):
            buf[h] = jnp.dot(k_ref[h, k_rows, :], qs_ref[h * HEAD_DIM:(h + 1) * HEAD_DIM, :],
                             preferred_element_type=F32)

    def att_block(j, carry):
        tiles = list(range(0, kb, ks))
        bufs = [s_ref, s2_ref]
        issue_scores(j, tiles[0], bufs[0])
        for i, r0 in enumerate(tiles):
            if i + 1 < len(tiles):
                issue_scores(j, tiles[i + 1], bufs[(i + 1) % 2])
            bias = mask_bias(j, r0)
            for h in range(N_HEADS):
                s = bufs[i % 2][h] + bias
                m_old = m_ref[h]
                m_new = jnp.maximum(m_old, jnp.max(s, axis=0, keepdims=True))
                alpha = jnp.exp(m_old - m_new)
                p = jnp.exp((s.reshape(ks // 8, 8, tq) - m_new[None]).reshape(ks, tq).astype(BF16))
                pv = jnp.dot(vt_ref[j, h, :, r0:r0 + ks], p, preferred_element_type=F32)
                acc = acc_ref[h].reshape(v_rows // 8, 8, tq) * alpha[None]
                acc_ref[h] = acc.reshape(v_rows, tq) + pv
                m_ref[h] = m_new
        return carry

    lax.fori_loop(0, n_kb, att_block, 0)
    for h in range(N_HEADS):
        rows = slice(h * HEAD_DIM, (h + 1) * HEAD_DIM)
        o_ref[rows, :] = (acc_ref[h, :HEAD_DIM, :] / acc_ref[h, HEAD_DIM:HEAD_DIM + 1, :]).astype(o_ref.dtype)


def _key_layouts(k, v, ik, kb):
    b, s_pad, _ = k.shape
    n_kb_total = s_pad // kb
    k_hm = k.astype(BF16).reshape(b, s_pad, N_HEADS, HEAD_DIM).transpose(0, 2, 1, 3)
    vt_blk = v.astype(BF16).reshape(b, n_kb_total, kb, N_HEADS, HEAD_DIM).transpose(0, 1, 3, 4, 2)
    ones = jnp.ones((b, n_kb_total, N_HEADS, _V_ROWS - HEAD_DIM, kb), BF16)
    return k_hm, jnp.concatenate([vt_blk, ones], axis=3), ik.astype(BF16)


def _dsa(qt, iqt, iwt, k_hm, vt_blk, ik, *, n_keep, causal, n_valid):
    b, _, t_pad = qt.shape
    n_kb_total, v_rows, kb = vt_blk.shape[1], vt_blk.shape[3], vt_blk.shape[4]
    s_pad = n_kb_total * kb
    tq = min(256, t_pad)
    assert t_pad % tq == 0 and tq % V7X_LANES == 0 and (not causal or tq % CHUNK == 0)
    body = functools.partial(_dsa_body, tq=tq, kb=kb, n_keep=float(n_keep), causal=causal, n_valid=n_valid,
                             n_kb_total=n_kb_total, pos_bits=s_pad.bit_length())
    qry = lambda n: pl.BlockSpec((None, n, tq), lambda bi, i: (bi, 0, i))
    return pl.pallas_call(
        body,
        out_shape=jax.ShapeDtypeStruct((b, D_ATTN, t_pad), BF16),
        grid=(b, t_pad // tq),
        in_specs=[qry(D_ATTN), qry(D_IDX), qry(N_IDX_HEADS),
                  pl.BlockSpec((None, N_HEADS, s_pad, HEAD_DIM), lambda bi, i: (bi, 0, 0, 0)),
                  pl.BlockSpec((None, n_kb_total, N_HEADS, v_rows, kb), lambda bi, i: (bi, 0, 0, 0, 0)),
                  pl.BlockSpec((None, s_pad, IDX_DIM), lambda bi, i: (bi, 0, 0))],
        out_specs=qry(D_ATTN),
        scratch_shapes=[pltpu.VMEM((n_kb_total, kb, tq), I32),
                        pltpu.VMEM((D_ATTN, tq), BF16),
                        pltpu.VMEM((N_HEADS, v_rows, tq), F32),
                        pltpu.VMEM((N_HEADS, 8, tq), F32),
                        pltpu.VMEM((N_HEADS, min(V7X_LANES, kb), tq), F32),
                        pltpu.VMEM((N_HEADS, min(V7X_LANES, kb), tq), F32)],
        compiler_params=_params("parallel", "arbitrary"),
        name="dsa",
    )(qt, iqt, iwt, k_hm, vt_blk, ik)


def _ssm_discretize_body(a_re_ref, a_im_ref, log_dt_ref, ab_re_ref, ab_im_ref, bc_re_ref, bc_im_ref):
    a_re, a_im = a_re_ref[...], a_im_ref[...]
    dt = jnp.exp(log_dt_ref[...])
    mag = jnp.exp(dt * a_re)
    ab_re = mag * jnp.cos(dt * a_im)
    ab_im = mag * jnp.sin(dt * a_im)
    den = a_re * a_re + a_im * a_im
    nr = ab_re - 1.0
    ni = ab_im
    ab_re_ref[...] = ab_re
    ab_im_ref[...] = ab_im
    bc_re_ref[...] = (nr * a_re + ni * a_im) / den
    bc_im_ref[...] = (ni * a_re - nr * a_im) / den


def _ssm_discretize(a_re, a_im, log_dt):
    g, p = a_re.shape
    out = jax.ShapeDtypeStruct((g, p), F32)
    return pl.pallas_call(_ssm_discretize_body, out_shape=(out,) * 4, name="ssm_discretize")(
        a_re, a_im, log_dt.reshape(g, 1))


def _block_diag_halves(w, dtype):
    g, r, c = w.shape
    gh = g // 2
    eye = jnp.eye(gh, dtype=w.dtype)
    halves = [jnp.einsum("grc,gh->grhc", w[i * gh:(i + 1) * gh], eye).reshape(gh * r, gh * c) for i in range(2)]
    return jnp.stack(halves, 0).astype(dtype)


def _ssm_body(u_ref, h0r_ref, h0i_ref, abr_ref, abi_ref, bcr_ref, bci_ref, br_ref, bi_ref, cr_ref, ci_ref,
              d_ref, gw_ref, gb_ref, y_ref, hr_ref, hi_ref, xr_ref, xi_ref, *, tb, nb, precise):
    d_ssm = u_ref.shape[-1]
    n_state = hr_ref.shape[-1]
    kh, sh = d_ssm // 2, n_state // 2

    def mm(a, w):
        if precise:
            return jnp.dot(a, w, preferred_element_type=F32, precision=lax.Precision.HIGHEST)
        return jnp.dot(a.astype(BF16), w, preferred_element_type=F32)

    @pl.when(pl.program_id(0) == 0)
    def _():
        hr_ref[...] = h0r_ref[...]
        hi_ref[...] = h0i_ref[...]

    u = pltpu.einshape("btc->tbc", u_ref[...]).reshape(tb * nb, d_ssm)

    for half in range(2):
        cols = slice(half * sh, (half + 1) * sh)
        uh = u[:, half * kh:(half + 1) * kh]
        bu_r = mm(uh, br_ref[half])
        bu_i = mm(uh, bi_ref[half])
        bc_r, bc_i = bcr_ref[:, cols], bci_ref[:, cols]
        xr_ref[:, cols] = bc_r * bu_r - bc_i * bu_i
        xi_ref[:, cols] = bc_r * bu_i + bc_i * bu_r

    slab = min(512, n_state)
    for s0 in range(0, n_state, slab):
        cols = slice(s0, s0 + slab)
        a_r = jnp.broadcast_to(abr_ref[:, cols], (nb, slab))
        a_i = jnp.broadcast_to(abi_ref[:, cols], (nb, slab))

        def step(t, h):
            h_r, h_i = h
            rows = pl.ds(pl.multiple_of(t * nb, nb), nb)
            n_r = a_r * h_r - a_i * h_i + xr_ref[rows, cols]
            n_i = a_r * h_i + a_i * h_r + xi_ref[rows, cols]
            xr_ref[rows, cols] = n_r
            xi_ref[rows, cols] = n_i
            return n_r, n_i

        h_r, h_i = lax.fori_loop(0, tb, step, (hr_ref[:, cols], hi_ref[:, cols]), unroll=min(8, tb))
        hr_ref[:, cols] = h_r
        hi_ref[:, cols] = h_i

    ys = []
    for half in range(2):
        cols = slice(half * sh, (half + 1) * sh)
        ys.append(mm(xr_ref[:, cols], cr_ref[half]) - mm(xi_ref[:, cols], ci_ref[half]))
    y = jnp.concatenate(ys, axis=1) + d_ref[...] * u
    y = y * (0.5 * (1.0 + jnp.tanh(math.sqrt(2.0 / math.pi) * (y + 0.044715 * (y * y * y)))))
    z = mm(y, gw_ref[...]) + gb_ref[...]
    y_ref[...] = pltpu.einshape("tbc->btc", (y * _sigmoid(z)).reshape(tb, nb, d_ssm))


def _ssm(u, h0_re, h0_im, disc, b_re, b_im, c_re, c_im, d, glu_w, glu_b, *, precise):
    nb, t, d_ssm = u.shape
    n_state = h0_re.shape[1]
    assert nb % 8 == 0
    tb = min(32, t)
    wdt = F32 if precise else BF16
    br = _block_diag_halves(jnp.swapaxes(b_re, 1, 2), wdt)
    bi = _block_diag_halves(jnp.swapaxes(b_im, 1, 2), wdt)
    cr = _block_diag_halves(jnp.swapaxes(c_re, 1, 2), wdt)
    ci = _block_diag_halves(jnp.swapaxes(c_im, 1, 2), wdt)
    row = lambda a: a.reshape(1, -1).astype(F32)
    state_spec = _const_spec((nb, n_state))
    vec_state = _const_spec((1, n_state))
    vec_ch = _const_spec((1, d_ssm))
    return pl.pallas_call(
        functools.partial(_ssm_body, tb=tb, nb=nb, precise=precise),
        out_shape=(jax.ShapeDtypeStruct((nb, t, d_ssm), F32),
                   jax.ShapeDtypeStruct((nb, n_state), F32), jax.ShapeDtypeStruct((nb, n_state), F32)),
        grid=(t // tb,),
        in_specs=[pl.BlockSpec((nb, tb, d_ssm), lambda i: (0, i, 0)), state_spec, state_spec,
                  vec_state, vec_state, vec_state, vec_state,
                  _const_spec(br.shape), _const_spec(bi.shape), _const_spec(cr.shape), _const_spec(ci.shape),
                  vec_ch, _const_spec((d_ssm, d_ssm)), vec_ch],
        out_specs=(pl.BlockSpec((nb, tb, d_ssm), lambda i: (0, i, 0)), state_spec, state_spec),
        scratch_shapes=[pltpu.VMEM((tb * nb, n_state), F32), pltpu.VMEM((tb * nb, n_state), F32)],
        compiler_params=_params("arbitrary"),
        name="ssm",
    )(u, h0_re, h0_im, row(disc[0]), row(disc[1]), row(disc[2]), row(disc[3]), br, bi, cr, ci,
      row(d), glu_w.astype(wdt), row(glu_b))


def _merge_body(att_t_ref, ssm_ref, ga_ref, gb_ref, x_ref, wpa_ref, wpb_ref, wo_ref, o_ref):
    att = att_t_ref[...].astype(F32).T.astype(BF16)
    a = jnp.dot(att, wpa_ref[...], preferred_element_type=F32)
    s = jnp.dot(ssm_ref[...].astype(BF16), wpb_ref[...], preferred_element_type=F32)
    mix = _sigmoid(ga_ref[...]) * a + _sigmoid(gb_ref[...]) * s
    o_ref[...] = x_ref[...] + jnp.dot(mix.astype(BF16), wo_ref[...], preferred_element_type=F32)


def _merge(att_t, ssm, ga, gb, x, w_pa, w_pb, w_o):
    b, t, d = x.shape
    tm = min(512, t)
    assert t % tm == 0 and tm % V7X_LANES == 0
    tok = lambda n: pl.BlockSpec((None, tm, n), lambda bi, i: (bi, i, 0))
    d_ssm = ssm.shape[-1]
    return pl.pallas_call(
        _merge_body,
        out_shape=jax.ShapeDtypeStruct((b, t, d), F32),
        grid=(b, t // tm),
        in_specs=[pl.BlockSpec((None, D_ATTN, tm), lambda bi, i: (bi, 0, i)), tok(d_ssm), tok(d), tok(d), tok(d),
                  _const_spec(w_pa.shape), _const_spec(w_pb.shape), _const_spec(w_o.shape)],
        out_specs=tok(d),
        compiler_params=_params("parallel", "parallel"),
        name="merge",
    )(att_t, ssm, ga, gb, x, w_pa.astype(BF16), w_pb.astype(BF16), w_o.astype(BF16))


def _ffn_body(x_ref, g_ref, wg_ref, wu_ref, wd_ref, o_ref, xn_ref, *, n_f):
    f = pl.program_id(1)

    @pl.when(f == 0)
    def _():
        xn_ref[...] = _rms(x_ref[...], g_ref[...]).astype(BF16)
        o_ref[...] = x_ref[...]

    xn = xn_ref[...]
    hg = jnp.dot(xn, wg_ref[...], preferred_element_type=F32)
    hu = jnp.dot(xn, wu_ref[...], preferred_element_type=F32)
    h = (hg * _sigmoid(hg)) * hu
    o_ref[...] += jnp.dot(h.astype(BF16), wd_ref[...], preferred_element_type=F32)


def _ffn(x2d, g, wg, wu, wd):
    n, d = x2d.shape
    d_ff = wg.shape[1]
    tm = min(512, n)
    fc = d_ff
    for cand in (1408, 1024, 896, 768, 512):
        if d_ff % cand == 0:
            fc = cand
            break
    n_f = d_ff // fc
    return pl.pallas_call(
        functools.partial(_ffn_body, n_f=n_f),
        out_shape=jax.ShapeDtypeStruct((n, d), F32),
        grid=(n // tm, n_f),
        in_specs=[pl.BlockSpec((tm, d), lambda i, f: (i, 0)), _const_spec((1, d)),
                  pl.BlockSpec((d, fc), lambda i, f: (0, f)), pl.BlockSpec((d, fc), lambda i, f: (0, f)),
                  pl.BlockSpec((fc, d), lambda i, f: (f, 0))],
        out_specs=pl.BlockSpec((tm, d), lambda i, f: (i, 0)),
        scratch_shapes=[pltpu.VMEM((tm, d), BF16)],
        compiler_params=_params("parallel", "arbitrary"),
        name="ffn",
    )(x2d, g.reshape(1, d), wg.astype(BF16), wu.astype(BF16), wd.astype(BF16))


def _moe_route_body(x_ref, g_ref, r_ref, xn_ref, gate_ref, gate_t_ref, rank_ref, rank_t_ref, *, n_exp, tm):
    xn = _rms(x_ref[...], g_ref[...])
    xn_ref[...] = xn.astype(BF16)
    logits = jnp.dot(xn, r_ref[...], preferred_element_type=F32, precision=lax.Precision.HIGHEST)
    lane = lax.broadcasted_iota(I32, logits.shape, 1).astype(F32)
    logits = jnp.where(lane < n_exp, logits, NEG_INF)
    big = float(logits.shape[1])
    m1 = jnp.max(logits, axis=1, keepdims=True)
    i1 = jnp.min(jnp.where(logits == m1, lane, big), axis=1, keepdims=True)
    rest = jnp.where(lane == i1, NEG_INF, logits)
    m2 = jnp.max(rest, axis=1, keepdims=True)
    i2 = jnp.min(jnp.where(rest == m2, lane, big), axis=1, keepdims=True)
    e2 = jnp.exp(m2 - m1)
    w1 = 1.0 / (1.0 + e2)
    w2 = e2 / (1.0 + e2)
    gates = jnp.where(lane == i1, w1, 0.0) + jnp.where(lane == i2, w2, 0.0)
    gates_t = gates.T
    gate_ref[...] = gates
    gate_t_ref[...] = gates_t
    t_row = lax.broadcasted_iota(I32, (tm, tm), 0)
    t_col = lax.broadcasted_iota(I32, (tm, tm), 1)
    member = jnp.where(gates > 0.0, 1.0, 0.0).astype(BF16)
    member_t = jnp.where(gates_t > 0.0, 1.0, 0.0).astype(BF16)
    rank_ref[...] = jnp.dot(jnp.where(t_col < t_row, 1.0, 0.0).astype(BF16), member, preferred_element_type=F32)
    rank_t_ref[...] = jnp.dot(member_t, jnp.where(t_row < t_col, 1.0, 0.0).astype(BF16),
                              preferred_element_type=F32)


def _moe_expert_body(acc_ref, xn_ref, gate_ref, gate_t_ref, rank_ref, rank_t_ref, wg_ref, wu_ref, wd_ref, o_ref, *,
                     fc, tm, rows_per_chunk, single_tile):
    e = pl.program_id(0)
    c = rows_per_chunk
    d_ff = wg_ref.shape[1]
    if single_tile:
        @pl.when(e == 0)
        def _():
            o_ref[...] = acc_ref[...]
    else:
        o_ref[...] = acc_ref[...]

    gate_row = gate_t_ref[pl.ds(e, 1), :]
    rank_row = rank_t_ref[pl.ds(e, 1), :]
    routed_row = gate_row > 0.0
    n_routed = jnp.sum(jnp.where(routed_row, 1.0, 0.0)).astype(I32)
    n_chunks = (n_routed + (c - 1)) // c

    lane = lax.broadcasted_iota(I32, gate_ref.shape, 1)
    rank_col = jnp.sum(jnp.where(lane == e, rank_ref[...], 0.0), axis=1, keepdims=True)
    routed_col = jnp.sum(jnp.where(lane == e, gate_ref[...], 0.0), axis=1, keepdims=True) > 0.0

    def chunk(ci, carry):
        slot_rows = (ci * c + lax.broadcasted_iota(I32, (c, tm), 0)).astype(F32)
        pack = routed_row & (rank_row == slot_rows)
        xs = jnp.dot(jnp.where(pack, 1.0, 0.0).astype(BF16), xn_ref[...], preferred_element_type=F32).astype(BF16)
        y = None
        for f0 in range(0, d_ff, fc):
            hg = jnp.dot(xs, wg_ref[:, f0:f0 + fc], preferred_element_type=F32)
            hu = jnp.dot(xs, wu_ref[:, f0:f0 + fc], preferred_element_type=F32)
            h = ((hg * _sigmoid(hg)) * hu).astype(BF16)
            part = jnp.dot(h, wd_ref[f0:f0 + fc, :], preferred_element_type=F32)
            y = part if y is None else y + part
        y = y * jnp.sum(jnp.where(pack, gate_row, 0.0), axis=1, keepdims=True)
        y_hi = y.astype(BF16)
        y_lo = (y - y_hi.astype(F32)).astype(BF16)
        slot_cols = (ci * c + lax.broadcasted_iota(I32, (tm, c), 1)).astype(F32)
        unpack = jnp.where(routed_col & (rank_col == slot_cols), 1.0, 0.0).astype(BF16)
        o_ref[...] += jnp.dot(jnp.concatenate([unpack, unpack], axis=1), jnp.concatenate([y_hi, y_lo], axis=0),
                              preferred_element_type=F32)
        return carry

    lax.fori_loop(0, n_chunks, chunk, 0)


def _moe(x2d, g, router, wg, wu, wd):
    n, d = x2d.shape
    n_exp, _, d_ff = wg.shape
    assert TOP_K_EXPERTS == 2 and n_exp <= V7X_LANES
    tm = min(1024, n)
    rows_per_chunk = min(V7X_LANES, tm)
    assert n % tm == 0 and tm % V7X_LANES == 0
    n_tiles = n // tm
    assert n_tiles == 1 or n_tiles >= 4
    fc = d_ff
    for cand in (896, 1024, 768, 512):
        if d_ff % cand == 0:
            fc = cand
            break
    r_pad = jnp.concatenate([router, jnp.zeros((d, V7X_LANES - n_exp), router.dtype)], axis=1)
    tok = lambda width: pl.BlockSpec((tm, width), lambda i: (i, 0))
    per_tile = pl.BlockSpec((None, V7X_LANES, tm), lambda i: (i, 0, 0))
    xn, gate, gate_t, rank, rank_t = pl.pallas_call(
        functools.partial(_moe_route_body, n_exp=n_exp, tm=tm),
        out_shape=(jax.ShapeDtypeStruct((n, d), BF16),
                   jax.ShapeDtypeStruct((n, V7X_LANES), F32), jax.ShapeDtypeStruct((n_tiles, V7X_LANES, tm), F32),
                   jax.ShapeDtypeStruct((n, V7X_LANES), F32), jax.ShapeDtypeStruct((n_tiles, V7X_LANES, tm), F32)),
        grid=(n_tiles,),
        in_specs=[tok(d), _const_spec((1, d)), _const_spec((d, V7X_LANES))],
        out_specs=(tok(d), tok(V7X_LANES), per_tile, tok(V7X_LANES), per_tile),
        compiler_params=_params("parallel"),
        name="moe_route",
    )(x2d, g.reshape(1, d), r_pad)

    tok2 = lambda width: pl.BlockSpec((tm, width), lambda e, i: (i, 0))
    per_tile2 = pl.BlockSpec((None, V7X_LANES, tm), lambda e, i: (i, 0, 0))
    w_spec = lambda rows, cols: pl.BlockSpec((None, rows, cols), lambda e, i: (e, 0, 0), pipeline_mode=pl.Buffered(1))
    return pl.pallas_call(
        functools.partial(_moe_expert_body, fc=fc, tm=tm, rows_per_chunk=rows_per_chunk, single_tile=n_tiles == 1),
        out_shape=jax.ShapeDtypeStruct((n, d), F32),
        grid=(n_exp, n_tiles),
        in_specs=[tok2(d), tok2(d), tok2(V7X_LANES), per_tile2, tok2(V7X_LANES), per_tile2,
                  w_spec(d, d_ff), w_spec(d, d_ff), w_spec(d_ff, d)],
        out_specs=tok2(d),
        input_output_aliases={0: 0},
        compiler_params=_params("arbitrary", "arbitrary"),
        name="moe_experts",
    )(x2d, xn, gate, gate_t, rank, rank_t, wg.astype(BF16), wu.astype(BF16), wd.astype(BF16))


def _final_norm_body(x_ref, g_ref, o_ref):
    o_ref[...] = _rms(x_ref[...], g_ref[...])


def _final_norm(x2d, g):
    n, d = x2d.shape
    tm = min(1024, n)
    return pl.pallas_call(
        _final_norm_body,
        out_shape=jax.ShapeDtypeStruct((n, d), F32),
        grid=(n // tm,),
        in_specs=[pl.BlockSpec((tm, d), lambda i: (i, 0)), _const_spec((1, d))],
        out_specs=pl.BlockSpec((tm, d), lambda i: (i, 0)),
        compiler_params=_params("parallel"),
        name="final_norm",
    )(x2d, g.reshape(1, d))


def _layer(x, past, h0, lw, *, layer_idx, precise_ssm):
    b, t, d = x.shape
    d_ssm = lw["ssm_d"].shape[0]
    grouped = (b, t) if past is None else (1, b * t)
    group = lambda a: a.reshape(grouped + a.shape[2:])
    ungroup = lambda a: a.reshape((b, t) + a.shape[2:])
    kb = min(512, t) if past is None else None
    qt, k, v, iqt, ik, iwt, u, ga, gb, *key_layouts = _inproj(group(x), lw["norm_mix_g"], lw["w_in_packed"], d_ssm,
                                                             key_block=kb)
    k, v, ik, u = ungroup(k), ungroup(v), ungroup(ik), ungroup(u)

    if past is None:
        s_valid = t
    else:
        per_batch = lambda a: jnp.pad(a.reshape(a.shape[1], b, t).transpose(1, 0, 2),
                                      ((0, 0), (0, 0), (0, (-t) % V7X_LANES)))
        qt, iqt, iwt = per_batch(qt), per_batch(iqt), per_batch(iwt)
        ck, cv, cik = past
        k_all = jnp.concatenate([ck.reshape(b, -1, D_ATTN), k], axis=1)
        v_all = jnp.concatenate([cv.reshape(b, -1, D_ATTN), v], axis=1)
        ik_all = jnp.concatenate([cik, ik], axis=1)
        s_valid = k_all.shape[1]
        kb = 512
        pad = (-s_valid) % kb
        padf = lambda a: jnp.pad(a, ((0, 0), (0, pad), (0, 0)))
        key_layouts = _key_layouts(padf(k_all), padf(v_all), padf(ik_all), kb)
    att_t = _dsa(qt, iqt, iwt, *key_layouts, n_keep=min(TOPK_MAX, s_valid // 4), causal=past is None,
                 n_valid=s_valid)
    if past is not None:
        att_t = att_t[:, :, :t].transpose(1, 0, 2).reshape(1, D_ATTN, b * t)

    ssm, h_re, h_im = _ssm(u, h0[0], h0[1], lw["disc"], lw["ssm_b_re"], lw["ssm_b_im"], lw["ssm_c_re"],
                           lw["ssm_c_im"], lw["ssm_d"], lw["glu_w"], lw["glu_b"], precise=precise_ssm)
    x = _merge(att_t, group(ssm), ga, gb, group(x), lw["w_branch_attn"], lw["w_branch_ssm"], lw["w_out"])
    x2d = x.reshape(b * t, d)
    if layer_idx % 2 == 0:
        x2d = _ffn(x2d, lw["norm_ffn_g"], lw["ffn_w_gate"], lw["ffn_w_up"], lw["ffn_w_down"])
    else:
        x2d = _moe(x2d, lw["norm_ffn_g"], lw["moe_router"], lw["moe_w_gate"], lw["moe_w_up"], lw["moe_w_down"])
    g_shape = (b, -1, SSM_STATE)
    return (x2d.reshape(b, t, d), k.reshape(b, t, N_HEADS, HEAD_DIM), v.reshape(b, t, N_HEADS, HEAD_DIM), ik,
            h_re.reshape(g_shape), h_im.reshape(g_shape))


def kernel(x_prompt, x_sample, cache_k, cache_v, cache_idx_k, state_ssm_re, state_ssm_im, norm_mix_g, w_in,
           ssm_a_re, ssm_a_im, ssm_log_dt, ssm_b_re, ssm_b_im, ssm_c_re, ssm_c_im, ssm_d, glu_w, glu_b,
           w_branch_attn, w_branch_ssm, w_out, norm_ffn_g, ffn_w_gate, ffn_w_up, ffn_w_down, moe_router,
           moe_w_gate, moe_w_up, moe_w_down, final_norm_g):
    depth = w_in.shape[0]
    d_model = x_prompt.shape[-1]
    d_ssm = ssm_d.shape[1]
    xp, xs = x_prompt, x_sample
    outs_p, outs_s = [], []
    for l in range(depth):
        lw = dict(
            norm_mix_g=norm_mix_g[l], w_in_packed=_pack_w_in(w_in[l], d_ssm, d_model),
            disc=_ssm_discretize(ssm_a_re[l], ssm_a_im[l], ssm_log_dt[l]),
            ssm_b_re=ssm_b_re[l], ssm_b_im=ssm_b_im[l], ssm_c_re=ssm_c_re[l], ssm_c_im=ssm_c_im[l],
            ssm_d=ssm_d[l], glu_w=glu_w[l], glu_b=glu_b[l],
            w_branch_attn=w_branch_attn[l], w_branch_ssm=w_branch_ssm[l], w_out=w_out[l], norm_ffn_g=norm_ffn_g[l])
        i = l // 2
        if l % 2 == 0:
            lw.update(ffn_w_gate=ffn_w_gate[i], ffn_w_up=ffn_w_up[i], ffn_w_down=ffn_w_down[i])
        else:
            lw.update(moe_router=moe_router[i], moe_w_gate=moe_w_gate[i], moe_w_up=moe_w_up[i],
                      moe_w_down=moe_w_down[i])
        n_state = ssm_a_re.shape[1] * ssm_a_re.shape[2]
        zeros = jnp.zeros((xp.shape[0], n_state), F32)
        xp, *rest_p = _layer(xp, None, (zeros, zeros), lw, layer_idx=l, precise_ssm=False)
        outs_p.append(rest_p)
        h0 = (state_ssm_re[l].reshape(xs.shape[0], n_state), state_ssm_im[l].reshape(xs.shape[0], n_state))
        xs, *rest_s = _layer(xs, (cache_k[l], cache_v[l], cache_idx_k[l]), h0, lw, layer_idx=l, precise_ssm=True)
        outs_s.append(rest_s)

    y_prompt = _final_norm(xp.reshape(-1, d_model), final_norm_g).reshape(xp.shape)
    y_sample = _final_norm(xs.reshape(-1, d_model), final_norm_g).reshape(xs.shape)
    stack = lambda outs, j: jnp.stack([o[j] for o in outs], 0)
    return (y_prompt, y_sample,
            stack(outs_p, 0), stack(outs_p, 1), stack(outs_p, 2), stack(outs_p, 3), stack(outs_p, 4),
            stack(outs_s, 0), stack(outs_s, 1), stack(outs_s, 2), stack(outs_s, 3), stack(outs_s, 4))
```

```python
import functools
import math

import jax
import jax.numpy as jnp
from jax import lax
from jax.experimental import pallas as pl
from jax.experimental.pallas import tpu as pltpu

F32 = jnp.float32
BF16 = jnp.bfloat16
I32 = jnp.int32

CHUNK = 64
N_HEADS = 8
HEAD_DIM = 64
N_IDX_HEADS = 8
IDX_DIM = 64
TOPK_MAX = 256
SSM_GROUP = 16
SSM_STATE = 64
TOP_K_EXPERTS = 2
RMS_EPS = 1e-6
NEG_INF = -1e30
INT32_MIN = -(2 ** 31)

D_ATTN = N_HEADS * HEAD_DIM
D_IDX = N_IDX_HEADS * IDX_DIM
IDX_W_SCALE = float(D_IDX) ** -0.5
ATTN_SCALE = float(HEAD_DIM) ** -0.5

V7X_LANES = 128
V7X_BF16_SUBLANES = 16
_V_ROWS = HEAD_DIM + V7X_BF16_SUBLANES
V7X_VMEM_LIMIT_BYTES = 56 * 1024 * 1024


def _params(*semantics):
    return pltpu.CompilerParams(dimension_semantics=semantics, vmem_limit_bytes=V7X_VMEM_LIMIT_BYTES)


def _sigmoid(x):
    return 1.0 / (1.0 + jnp.exp(-x))


def _rms(x, g):
    return (x * lax.rsqrt(jnp.mean(x * x, axis=-1, keepdims=True) + RMS_EPS)) * g


def _const_spec(shape):
    return pl.BlockSpec(shape, lambda *_: (0,) * len(shape))


_SEG_Q, _SEG_K, _SEG_V, _SEG_IQ = 0, D_ATTN, 2 * D_ATTN, 3 * D_ATTN
_SEG_SMALL = 3 * D_ATTN + D_IDX


def _pack_w_in(w_in, d_ssm, d_model):
    offs = [0]
    for s in (D_ATTN, D_ATTN, D_ATTN, D_IDX, IDX_DIM, N_IDX_HEADS, d_ssm, d_model, d_model):
        offs.append(offs[-1] + s)
    pad = V7X_LANES - IDX_DIM - N_IDX_HEADS
    small = jnp.concatenate([w_in[:, offs[4]:offs[6]], jnp.zeros((w_in.shape[0], pad), w_in.dtype)], axis=1)
    return jnp.concatenate([w_in[:, :offs[4]], small, w_in[:, offs[6]:]], axis=1).astype(BF16)


def _inproj_body(x_ref, g_ref, w_ref, q_ref, k_ref, v_ref, iq_ref, ik_ref, iw_ref, u_ref, ga_ref, gb_ref, *key_refs,
                 d_ssm, d_model):
    hb = _rms(x_ref[...], g_ref[...]).astype(BF16)

    def proj(lo, n):
        return jnp.dot(hb, w_ref[:, lo:lo + n], preferred_element_type=F32)

    q_ref[...] = proj(_SEG_Q, D_ATTN).T.astype(BF16)
    k = proj(_SEG_K, D_ATTN)
    v = proj(_SEG_V, D_ATTN)
    k_ref[...] = k
    v_ref[...] = v
    iq_ref[...] = proj(_SEG_IQ, D_IDX).T.astype(BF16)
    small = proj(_SEG_SMALL, V7X_LANES)
    ik_ref[...] = small[:, :IDX_DIM]
    iw_ref[...] = small.T[IDX_DIM:IDX_DIM + N_IDX_HEADS, :] * IDX_W_SCALE
    if key_refs:
        k_hm_ref, vt_ref, ik_bf16_ref = key_refs
        vt = v.T
        for h in range(N_HEADS):
            cols = slice(h * HEAD_DIM, (h + 1) * HEAD_DIM)
            k_hm_ref[h] = k[:, cols].astype(BF16)
            vt_ref[h, :HEAD_DIM, :] = vt[cols, :].astype(BF16)
            vt_ref[h, HEAD_DIM:, :] = jnp.ones((vt_ref.shape[1] - HEAD_DIM, vt.shape[1]), BF16)
        ik_bf16_ref[...] = small[:, :IDX_DIM].astype(BF16)
    seg_u = _SEG_SMALL + V7X_LANES
    u_ref[...] = proj(seg_u, d_ssm)
    ga_ref[...] = proj(seg_u + d_ssm, d_model)
    gb_ref[...] = proj(seg_u + d_ssm + d_model, d_model)


def _inproj(x, g, w_packed, d_ssm, key_block=None):
    b, t, d = x.shape
    tm = min(256, t)
    assert t % tm == 0 and tm % V7X_LANES == 0
    n_w = w_packed.shape[1]

    def tok(n, dtype):
        return jax.ShapeDtypeStruct((b, t, n), dtype), pl.BlockSpec((None, tm, n), lambda bi, i: (bi, i, 0))

    def tok_t(n, dtype):
        return jax.ShapeDtypeStruct((b, n, t), dtype), pl.BlockSpec((None, n, tm), lambda bi, i: (bi, 0, i))

    shapes, specs = zip(
        tok_t(D_ATTN, BF16), tok(D_ATTN, F32), tok(D_ATTN, F32), tok_t(D_IDX, BF16), tok(IDX_DIM, F32),
        tok_t(N_IDX_HEADS, F32), tok(d_ssm, F32), tok(d, F32), tok(d, F32))
    if key_block is not None:
        assert key_block % tm == 0 and t % key_block == 0
        per_blk = key_block // tm
        shapes += (jax.ShapeDtypeStruct((b, N_HEADS, t, HEAD_DIM), BF16),
                   jax.ShapeDtypeStruct((b, t // key_block, N_HEADS, _V_ROWS, key_block), BF16),
                   jax.ShapeDtypeStruct((b, t, IDX_DIM), BF16))
        specs += (pl.BlockSpec((None, N_HEADS, tm, HEAD_DIM), lambda bi, i: (bi, 0, i, 0)),
                  pl.BlockSpec((None, None, N_HEADS, _V_ROWS, tm), lambda bi, i: (bi, i // per_blk, 0, 0, i % per_blk)),
                  pl.BlockSpec((None, tm, IDX_DIM), lambda bi, i: (bi, i, 0)))
    return pl.pallas_call(
        functools.partial(_inproj_body, d_ssm=d_ssm, d_model=d),
        out_shape=shapes,
        grid=(b, t // tm),
        in_specs=[pl.BlockSpec((None, tm, d), lambda bi, i: (bi, i, 0)), _const_spec((1, d)), _const_spec((d, n_w))],
        out_specs=specs,
        compiler_params=_params("parallel", "parallel"),
        name="inproj",
    )(x, g.reshape(1, d), w_packed)


def _dsa_body(qt_ref, iqt_ref, iwt_ref, k_ref, vt_ref, ik_ref, o_ref, key_ref, qs_ref, acc_ref, m_ref, s_ref, s2_ref, *,
              tq, kb, n_keep, causal, n_valid, n_kb_total, pos_bits):
    col0 = pl.program_id(1) * tq
    if causal:
        n_kb = (col0 + tq + kb - 1) // kb
        q_pos = col0 + lax.broadcasted_iota(I32, (1, tq), 1)
        shift = CHUNK.bit_length() - 1
        q_limit = ((q_pos >> shift) + 1) << shift
    else:
        n_kb = n_kb_total
        q_limit = jnp.full((1, tq), n_valid, I32)

    ks = min(V7X_LANES, kb)
    cr = min(64, kb)

    def key_pos(j, r0, rows):
        return j * kb + r0 + lax.broadcasted_iota(I32, (rows, tq), 0)

    iwt = iwt_ref[...]

    def idx_block(j, carry):
        for r0 in range(0, kb, ks):
            ik_tile = ik_ref[pl.ds(pl.multiple_of(j * kb + r0, ks), ks), :]
            s = None
            for h in range(N_IDX_HEADS):
                d = jnp.dot(ik_tile, iqt_ref[h * IDX_DIM:(h + 1) * IDX_DIM, :], preferred_element_type=F32)
                term = iwt[h:h + 1, :] * jnp.maximum(d, 0.0)
                s = term if s is None else s + term
            s = jnp.where(s == 0.0, 0.0, s)
            s = jnp.where(key_pos(j, r0, ks) < q_limit, s, NEG_INF)
            bits = lax.bitcast_convert_type(s, I32)
            key_ref[j, r0:r0 + ks, :] = jnp.where(bits < 0, bits ^ jnp.int32(0x7FFFFFFF), bits)
        return carry

    lax.fori_loop(0, n_kb, idx_block, 0)

    def count(pred):
        def body(j, accs):
            accs = list(accs)
            for n, r0 in enumerate(range(0, kb, cr)):
                hit = pred(key_ref[j, r0:r0 + cr, :], functools.partial(key_pos, j, r0, cr))
                accs[n % 2] = accs[n % 2] + jnp.where(hit, 1.0, 0.0)
            return tuple(accs)
        zero = jnp.zeros((cr, tq), F32)
        acc_a, acc_b = lax.fori_loop(0, n_kb, body, (zero, zero))
        return jnp.sum(acc_a + acc_b, axis=0, keepdims=True)

    def bit_step(i, thr):
        cand = thr + lax.shift_left(jnp.int32(1), 31 - i)
        cnt = count(lambda key, pos: key >= cand)
        return jnp.where(cnt >= n_keep, cand, thr)

    thr = lax.fori_loop(0, 32, bit_step, jnp.full((1, tq), INT32_MIN, I32))

    need = n_keep - count(lambda key, pos: key > thr)
    n_eq = count(lambda key, pos: key == thr)
    pos_all = jnp.int32(2 ** pos_bits - 1)

    def tie_search():
        def step(i, end):
            cand = end + lax.shift_left(jnp.int32(1), pos_bits - 1 - i)
            cnt = count(lambda key, pos: (key == thr) & (pos() < cand))
            return jnp.where(cnt <= need, cand, end)
        return lax.fori_loop(0, pos_bits, step, jnp.zeros((1, tq), I32))

    tie_end = lax.cond(jnp.max(n_eq - need) > 0.0, tie_search, lambda: jnp.full((1, tq), pos_all, I32))

    qs_ref[...] = qt_ref[...] * jnp.asarray(ATTN_SCALE, BF16)
    m_ref[...] = jnp.full(m_ref.shape, NEG_INF, F32)
    acc_ref[...] = jnp.zeros(acc_ref.shape, F32)

    def mask_bias(j, r0):
        key = key_ref[j, r0:r0 + ks, :]
        pos = key_pos(j, r0, ks)
        keep = (pos < q_limit) & ((key > thr) | ((key == thr) & (pos < tie_end)))
        return jnp.where(keep, 0.0, NEG_INF)

    v_rows = acc_ref.shape[1]

    def issue_scores(j, r0, buf):
        k_rows = pl.ds(pl.multiple_of(j * kb + r0, ks), ks)
        for h in range(N_HEADS):
            buf[h] = jnp.dot(k_ref[h, k_rows, :], qs_ref[h * HEAD_DIM:(h + 1) * HEAD_DIM, :],
                             preferred_element_type=F32)

    def att_block(j, carry):
        tiles = list(range(0, kb, ks))
        bufs = [s_ref, s2_ref]
        issue_scores(j, tiles[0], bufs[0])
        for i, r0 in enumerate(tiles):
            if i + 1 < len(tiles):
                issue_scores(j, tiles[i + 1], bufs[(i + 1) % 2])
            bias = mask_bias(j, r0)
            for h in range(N_HEADS):
                s = bufs[i % 2][h] + bias
                m_old = m_ref[h]
                m_new = jnp.maximum(m_old, jnp.max(s, axis=0, keepdims=True))
                alpha = jnp.exp(m_old - m_new)
                p = jnp.exp((s.reshape(ks // 8, 8, tq) - m_new[None]).reshape(ks, tq).astype(BF16))
                pv = jnp.dot(vt_ref[j, h, :, r0:r0 + ks], p, preferred_element_type=F32)
                acc = acc_ref[h].reshape(v_rows // 8, 8, tq) * alpha[None]
                acc_ref[h] = acc.reshape(v_rows, tq) + pv
                m_ref[h] = m_new
        return carry

    lax.fori_loop(0, n_kb, att_block, 0)
    for h in range(N_HEADS):
        rows = slice(h * HEAD_DIM, (h + 1) * HEAD_DIM)
        o_ref[rows, :] = (acc_ref[h, :HEAD_DIM, :] / acc_ref[h, HEAD_DIM:HEAD_DIM + 1, :]).astype(o_ref.dtype)


def _key_layouts(k, v, ik, kb):
    b, s_pad, _ = k.shape
    n_kb_total = s_pad // kb
    k_hm = k.astype(BF16).reshape(b, s_pad, N_HEADS, HEAD_DIM).transpose(0, 2, 1, 3)
    vt_blk = v.astype(BF16).reshape(b, n_kb_total, kb, N_HEADS, HEAD_DIM).transpose(0, 1, 3, 4, 2)
    ones = jnp.ones((b, n_kb_total, N_HEADS, _V_ROWS - HEAD_DIM, kb), BF16)
    return k_hm, jnp.concatenate([vt_blk, ones], axis=3), ik.astype(BF16)


def _dsa(qt, iqt, iwt, k_hm, vt_blk, ik, *, n_keep, causal, n_valid):
    b, _, t_pad = qt.shape
    n_kb_total, v_rows, kb = vt_blk.shape[1], vt_blk.shape[3], vt_blk.shape[4]
    s_pad = n_kb_total * kb
    tq = min(256, t_pad)
    assert t_pad % tq == 0 and tq % V7X_LANES == 0 and (not causal or tq % CHUNK == 0)
    body = functools.partial(_dsa_body, tq=tq, kb=kb, n_keep=float(n_keep), causal=causal, n_valid=n_valid,
                             n_kb_total=n_kb_total, pos_bits=s_pad.bit_length())
    qry = lambda n: pl.BlockSpec((None, n, tq), lambda bi, i: (bi, 0, i))
    return pl.pallas_call(
        body,
        out_shape=jax.ShapeDtypeStruct((b, D_ATTN, t_pad), BF16),
        grid=(b, t_pad // tq),
        in_specs=[qry(D_ATTN), qry(D_IDX), qry(N_IDX_HEADS),
                  pl.BlockSpec((None, N_HEADS, s_pad, HEAD_DIM), lambda bi, i: (bi, 0, 0, 0)),
                  pl.BlockSpec((None, n_kb_total, N_HEADS, v_rows, kb), lambda bi, i: (bi, 0, 0, 0, 0)),
                  pl.BlockSpec((None, s_pad, IDX_DIM), lambda bi, i: (bi, 0, 0))],
        out_specs=qry(D_ATTN),
        scratch_shapes=[pltpu.VMEM((n_kb_total, kb, tq), I32),
                        pltpu.VMEM((D_ATTN, tq), BF16),
                        pltpu.VMEM((N_HEADS, v_rows, tq), F32),
                        pltpu.VMEM((N_HEADS, 8, tq), F32),
                        pltpu.VMEM((N_HEADS, min(V7X_LANES, kb), tq), F32),
                        pltpu.VMEM((N_HEADS, min(V7X_LANES, kb), tq), F32)],
        compiler_params=_params("parallel", "arbitrary"),
        name="dsa",
    )(qt, iqt, iwt, k_hm, vt_blk, ik)


def _ssm_discretize_body(a_re_ref, a_im_ref, log_dt_ref, ab_re_ref, ab_im_ref, bc_re_ref, bc_im_ref):
    a_re, a_im = a_re_ref[...], a_im_ref[...]
    dt = jnp.exp(log_dt_ref[...])
    mag = jnp.exp(dt * a_re)
    ab_re = mag * jnp.cos(dt * a_im)
    ab_im = mag * jnp.sin(dt * a_im)
    den = a_re * a_re + a_im * a_im
    nr = ab_re - 1.0
    ni = ab_im
    ab_re_ref[...] = ab_re
    ab_im_ref[...] = ab_im
    bc_re_ref[...] = (nr * a_re + ni * a_im) / den
    bc_im_ref[...] = (ni * a_re - nr * a_im) / den


def _ssm_discretize(a_re, a_im, log_dt):
    g, p = a_re.shape
    out = jax.ShapeDtypeStruct((g, p), F32)
    return pl.pallas_call(_ssm_discretize_body, out_shape=(out,) * 4, name="ssm_discretize")(
        a_re, a_im, log_dt.reshape(g, 1))


def _block_diag_halves(w, dtype):
    g, r, c = w.shape
    gh = g // 2
    eye = jnp.eye(gh, dtype=w.dtype)
    halves = [jnp.einsum("grc,gh->grhc", w[i * gh:(i + 1) * gh], eye).reshape(gh * r, gh * c) for i in range(2)]
    return jnp.stack(halves, 0).astype(dtype)


def _ssm_body(u_ref, h0r_ref, h0i_ref, abr_ref, abi_ref, bcr_ref, bci_ref, br_ref, bi_ref, cr_ref, ci_ref,
              d_ref, gw_ref, gb_ref, y_ref, hr_ref, hi_ref, xr_ref, xi_ref, *, tb, nb, precise):
    d_ssm = u_ref.shape[-1]
    n_state = hr_ref.shape[-1]
    kh, sh = d_ssm // 2, n_state // 2

    def mm(a, w):
        if precise:
            return jnp.dot(a, w, preferred_element_type=F32, precision=lax.Precision.HIGHEST)
        return jnp.dot(a.astype(BF16), w, preferred_element_type=F32)

    @pl.when(pl.program_id(0) == 0)
    def _():
        hr_ref[...] = h0r_ref[...]
        hi_ref[...] = h0i_ref[...]

    u = pltpu.einshape("btc->tbc", u_ref[...]).reshape(tb * nb, d_ssm)

    for half in range(2):
        cols = slice(half * sh, (half + 1) * sh)
        uh = u[:, half * kh:(half + 1) * kh]
        bu_r = mm(uh, br_ref[half])
        bu_i = mm(uh, bi_ref[half])
        bc_r, bc_i = bcr_ref[:, cols], bci_ref[:, cols]
        xr_ref[:, cols] = bc_r * bu_r - bc_i * bu_i
        xi_ref[:, cols] = bc_r * bu_i + bc_i * bu_r

    slab = min(512, n_state)
    for s0 in range(0, n_state, slab):
        cols = slice(s0, s0 + slab)
        a_r = jnp.broadcast_to(abr_ref[:, cols], (nb, slab))
        a_i = jnp.broadcast_to(abi_ref[:, cols], (nb, slab))

        def step(t, h):
            h_r, h_i = h
            rows = pl.ds(pl.multiple_of(t * nb, nb), nb)
            n_r = a_r * h_r - a_i * h_i + xr_ref[rows, cols]
            n_i = a_r * h_i + a_i * h_r + xi_ref[rows, cols]
            xr_ref[rows, cols] = n_r
            xi_ref[rows, cols] = n_i
            return n_r, n_i

        h_r, h_i = lax.fori_loop(0, tb, step, (hr_ref[:, cols], hi_ref[:, cols]), unroll=min(8, tb))
        hr_ref[:, cols] = h_r
        hi_ref[:, cols] = h_i

    ys = []
    for half in range(2):
        cols = slice(half * sh, (half + 1) * sh)
        ys.append(mm(xr_ref[:, cols], cr_ref[half]) - mm(xi_ref[:, cols], ci_ref[half]))
    y = jnp.concatenate(ys, axis=1) + d_ref[...] * u
    y = y * (0.5 * (1.0 + jnp.tanh(math.sqrt(2.0 / math.pi) * (y + 0.044715 * (y * y * y)))))
    z = mm(y, gw_ref[...]) + gb_ref[...]
    y_ref[...] = pltpu.einshape("tbc->btc", (y * _sigmoid(z)).reshape(tb, nb, d_ssm))


def _ssm(u, h0_re, h0_im, disc, b_re, b_im, c_re, c_im, d, glu_w, glu_b, *, precise):
    nb, t, d_ssm = u.shape
    n_state = h0_re.shape[1]
    assert nb % 8 == 0
    tb = min(32, t)
    wdt = F32 if precise else BF16
    br = _block_diag_halves(jnp.swapaxes(b_re, 1, 2), wdt)
    bi = _block_diag_halves(jnp.swapaxes(b_im, 1, 2), wdt)
    cr = _block_diag_halves(jnp.swapaxes(c_re, 1, 2), wdt)
    ci = _block_diag_halves(jnp.swapaxes(c_im, 1, 2), wdt)
    row = lambda a: a.reshape(1, -1).astype(F32)
    state_spec = _const_spec((nb, n_state))
    vec_state = _const_spec((1, n_state))
    vec_ch = _const_spec((1, d_ssm))
    return pl.pallas_call(
        functools.partial(_ssm_body, tb=tb, nb=nb, precise=precise),
        out_shape=(jax.ShapeDtypeStruct((nb, t, d_ssm), F32),
                   jax.ShapeDtypeStruct((nb, n_state), F32), jax.ShapeDtypeStruct((nb, n_state), F32)),
        grid=(t // tb,),
        in_specs=[pl.BlockSpec((nb, tb, d_ssm), lambda i: (0, i, 0)), state_spec, state_spec,
                  vec_state, vec_state, vec_state, vec_state,
                  _const_spec(br.shape), _const_spec(bi.shape), _const_spec(cr.shape), _const_spec(ci.shape),
                  vec_ch, _const_spec((d_ssm, d_ssm)), vec_ch],
        out_specs=(pl.BlockSpec((nb, tb, d_ssm), lambda i: (0, i, 0)), state_spec, state_spec),
        scratch_shapes=[pltpu.VMEM((tb * nb, n_state), F32), pltpu.VMEM((tb * nb, n_state), F32)],
        compiler_params=_params("arbitrary"),
        name="ssm",
    )(u, h0_re, h0_im, row(disc[0]), row(disc[1]), row(disc[2]), row(disc[3]), br, bi, cr, ci,
      row(d), glu_w.astype(wdt), row(glu_b))


def _merge_body(att_t_ref, ssm_ref, ga_ref, gb_ref, x_ref, wpa_ref, wpb_ref, wo_ref, o_ref):
    att = att_t_ref[...].astype(F32).T.astype(BF16)
    a = jnp.dot(att, wpa_ref[...], preferred_element_type=F32)
    s = jnp.dot(ssm_ref[...].astype(BF16), wpb_ref[...], preferred_element_type=F32)
    mix = _sigmoid(ga_ref[...]) * a + _sigmoid(gb_ref[...]) * s
    o_ref[...] = x_ref[...] + jnp.dot(mix.astype(BF16), wo_ref[...], preferred_element_type=F32)


def _merge(att_t, ssm, ga, gb, x, w_pa, w_pb, w_o):
    b, t, d = x.shape
    tm = min(512, t)
    assert t % tm == 0 and tm % V7X_LANES == 0
    tok = lambda n: pl.BlockSpec((None, tm, n), lambda bi, i: (bi, i, 0))
    d_ssm = ssm.shape[-1]
    return pl.pallas_call(
        _merge_body,
        out_shape=jax.ShapeDtypeStruct((b, t, d), F32),
        grid=(b, t // tm),
        in_specs=[pl.BlockSpec((None, D_ATTN, tm), lambda bi, i: (bi, 0, i)), tok(d_ssm), tok(d), tok(d), tok(d),
                  _const_spec(w_pa.shape), _const_spec(w_pb.shape), _const_spec(w_o.shape)],
        out_specs=tok(d),
        compiler_params=_params("parallel", "parallel"),
        name="merge",
    )(att_t, ssm, ga, gb, x, w_pa.astype(BF16), w_pb.astype(BF16), w_o.astype(BF16))


def _ffn_body(x_ref, g_ref, wg_ref, wu_ref, wd_ref, fg_ref, o_ref, xn_ref, *, n_f, final_norm):
    f = pl.program_id(1)

    @pl.when(f == 0)
    def _():
        xn_ref[...] = _rms(x_ref[...], g_ref[...]).astype(BF16)
        o_ref[...] = x_ref[...]

    xn = xn_ref[...]
    hg = jnp.dot(xn, wg_ref[...], preferred_element_type=F32)
    hu = jnp.dot(xn, wu_ref[...], preferred_element_type=F32)
    h = (hg * _sigmoid(hg)) * hu
    o_ref[...] += jnp.dot(h.astype(BF16), wd_ref[...], preferred_element_type=F32)
    if final_norm:
        @pl.when(f == n_f - 1)
        def _():
            o_ref[...] = _rms(o_ref[...], fg_ref[...])


def _ffn(x2d, g, wg, wu, wd, final_g, *, final_norm):
    n, d = x2d.shape
    d_ff = wg.shape[1]
    tm = min(512, n)
    fc = d_ff
    for cand in (1408, 1024, 896, 768, 512):
        if d_ff % cand == 0:
            fc = cand
            break
    n_f = d_ff // fc
    return pl.pallas_call(
        functools.partial(_ffn_body, n_f=n_f, final_norm=final_norm),
        out_shape=jax.ShapeDtypeStruct((n, d), F32),
        grid=(n // tm, n_f),
        in_specs=[pl.BlockSpec((tm, d), lambda i, f: (i, 0)), _const_spec((1, d)),
                  pl.BlockSpec((d, fc), lambda i, f: (0, f)), pl.BlockSpec((d, fc), lambda i, f: (0, f)),
                  pl.BlockSpec((fc, d), lambda i, f: (f, 0)), _const_spec((1, d))],
        out_specs=pl.BlockSpec((tm, d), lambda i, f: (i, 0)),
        scratch_shapes=[pltpu.VMEM((tm, d), BF16)],
        compiler_params=_params("parallel", "arbitrary"),
        name="ffn",
    )(x2d, g.reshape(1, d), wg.astype(BF16), wu.astype(BF16), wd.astype(BF16), final_g.reshape(1, d))


def _moe_route_body(x_ref, g_ref, r_ref, xn_ref, gate_ref, gate_t_ref, rank_ref, rank_t_ref, *, n_exp, tm):
    xn = _rms(x_ref[...], g_ref[...])
    xn_ref[...] = xn.astype(BF16)
    logits = jnp.dot(xn, r_ref[...], preferred_element_type=F32, precision=lax.Precision.HIGHEST)
    lane = lax.broadcasted_iota(I32, logits.shape, 1).astype(F32)
    logits = jnp.where(lane < n_exp, logits, NEG_INF)
    big = float(logits.shape[1])
    m1 = jnp.max(logits, axis=1, keepdims=True)
    i1 = jnp.min(jnp.where(logits == m1, lane, big), axis=1, keepdims=True)
    rest = jnp.where(lane == i1, NEG_INF, logits)
    m2 = jnp.max(rest, axis=1, keepdims=True)
    i2 = jnp.min(jnp.where(rest == m2, lane, big), axis=1, keepdims=True)
    e2 = jnp.exp(m2 - m1)
    w1 = 1.0 / (1.0 + e2)
    w2 = e2 / (1.0 + e2)
    gates = jnp.where(lane == i1, w1, 0.0) + jnp.where(lane == i2, w2, 0.0)
    gates_t = gates.T
    gate_ref[...] = gates
    gate_t_ref[...] = gates_t
    t_row = lax.broadcasted_iota(I32, (tm, tm), 0)
    t_col = lax.broadcasted_iota(I32, (tm, tm), 1)
    member = jnp.where(gates > 0.0, 1.0, 0.0).astype(BF16)
    member_t = jnp.where(gates_t > 0.0, 1.0, 0.0).astype(BF16)
    rank_ref[...] = jnp.dot(jnp.where(t_col < t_row, 1.0, 0.0).astype(BF16), member, preferred_element_type=F32)
    rank_t_ref[...] = jnp.dot(member_t, jnp.where(t_row < t_col, 1.0, 0.0).astype(BF16),
                              preferred_element_type=F32)


def _moe_expert_body(acc_ref, xn_ref, gate_ref, gate_t_ref, rank_ref, rank_t_ref, wg_ref, wu_ref, wd_ref, fg_ref,
                     o_ref, *, fc, tm, rows_per_chunk, single_tile, final_norm):
    e = pl.program_id(0)
    c = rows_per_chunk
    d_ff = wg_ref.shape[1]
    if single_tile:
        @pl.when(e == 0)
        def _():
            o_ref[...] = acc_ref[...]
    else:
        o_ref[...] = acc_ref[...]

    gate_row = gate_t_ref[pl.ds(e, 1), :]
    rank_row = rank_t_ref[pl.ds(e, 1), :]
    routed_row = gate_row > 0.0
    n_routed = jnp.sum(jnp.where(routed_row, 1.0, 0.0)).astype(I32)
    n_chunks = (n_routed + (c - 1)) // c

    lane = lax.broadcasted_iota(I32, gate_ref.shape, 1)
    rank_col = jnp.sum(jnp.where(lane == e, rank_ref[...], 0.0), axis=1, keepdims=True)
    routed_col = jnp.sum(jnp.where(lane == e, gate_ref[...], 0.0), axis=1, keepdims=True) > 0.0

    def chunk(ci, carry):
        slot_rows = (ci * c + lax.broadcasted_iota(I32, (c, tm), 0)).astype(F32)
        pack = routed_row & (rank_row == slot_rows)
        xs = jnp.dot(jnp.where(pack, 1.0, 0.0).astype(BF16), xn_ref[...], preferred_element_type=F32).astype(BF16)
        y = None
        for f0 in range(0, d_ff, fc):
            hg = jnp.dot(xs, wg_ref[:, f0:f0 + fc], preferred_element_type=F32)
            hu = jnp.dot(xs, wu_ref[:, f0:f0 + fc], preferred_element_type=F32)
            h = ((hg * _sigmoid(hg)) * hu).astype(BF16)
            part = jnp.dot(h, wd_ref[f0:f0 + fc, :], preferred_element_type=F32)
            y = part if y is None else y + part
        y = y * jnp.sum(jnp.where(pack, gate_row, 0.0), axis=1, keepdims=True)
        y_hi = y.astype(BF16)
        y_lo = (y - y_hi.astype(F32)).astype(BF16)
        slot_cols = (ci * c + lax.broadcasted_iota(I32, (tm, c), 1)).astype(F32)
        unpack = jnp.where(routed_col & (rank_col == slot_cols), 1.0, 0.0).astype(BF16)
        o_ref[...] += jnp.dot(jnp.concatenate([unpack, unpack], axis=1), jnp.concatenate([y_hi, y_lo], axis=0),
                              preferred_element_type=F32)
        return carry

    lax.fori_loop(0, n_chunks, chunk, 0)
    if final_norm:
        @pl.when(e == pl.num_programs(0) - 1)
        def _():
            o_ref[...] = _rms(o_ref[...], fg_ref[...])


def _moe(x2d, g, router, wg, wu, wd, final_g, *, final_norm):
    n, d = x2d.shape
    n_exp, _, d_ff = wg.shape
    assert TOP_K_EXPERTS == 2 and n_exp <= V7X_LANES
    tm = min(1024, n)
    rows_per_chunk = min(V7X_LANES, tm)
    assert n % tm == 0 and tm % V7X_LANES == 0
    n_tiles = n // tm
    assert n_tiles == 1 or n_tiles >= 4
    fc = d_ff
    for cand in (896, 1024, 768, 512):
        if d_ff % cand == 0:
            fc = cand
            break
    r_pad = jnp.concatenate([router, jnp.zeros((d, V7X_LANES - n_exp), router.dtype)], axis=1)
    tok = lambda width: pl.BlockSpec((tm, width), lambda i: (i, 0))
    per_tile = pl.BlockSpec((None, V7X_LANES, tm), lambda i: (i, 0, 0))
    xn, gate, gate_t, rank, rank_t = pl.pallas_call(
        functools.partial(_moe_route_body, n_exp=n_exp, tm=tm),
        out_shape=(jax.ShapeDtypeStruct((n, d), BF16),
                   jax.ShapeDtypeStruct((n, V7X_LANES), F32), jax.ShapeDtypeStruct((n_tiles, V7X_LANES, tm), F32),
                   jax.ShapeDtypeStruct((n, V7X_LANES), F32), jax.ShapeDtypeStruct((n_tiles, V7X_LANES, tm), F32)),
        grid=(n_tiles,),
        in_specs=[tok(d), _const_spec((1, d)), _const_spec((d, V7X_LANES))],
        out_specs=(tok(d), tok(V7X_LANES), per_tile, tok(V7X_LANES), per_tile),
        compiler_params=_params("parallel"),
        name="moe_route",
    )(x2d, g.reshape(1, d), r_pad)

    tok2 = lambda width: pl.BlockSpec((tm, width), lambda e, i: (i, 0))
    per_tile2 = pl.BlockSpec((None, V7X_LANES, tm), lambda e, i: (i, 0, 0))
    w_spec = lambda rows, cols: pl.BlockSpec((None, rows, cols), lambda e, i: (e, 0, 0), pipeline_mode=pl.Buffered(1))
    return pl.pallas_call(
        functools.partial(_moe_expert_body, fc=fc, tm=tm, rows_per_chunk=rows_per_chunk, single_tile=n_tiles == 1,
                          final_norm=final_norm),
        out_shape=jax.ShapeDtypeStruct((n, d), F32),
        grid=(n_exp, n_tiles),
        in_specs=[tok2(d), tok2(d), tok2(V7X_LANES), per_tile2, tok2(V7X_LANES), per_tile2,
                  w_spec(d, d_ff), w_spec(d, d_ff), w_spec(d_ff, d), _const_spec((1, d))],
        out_specs=tok2(d),
        input_output_aliases={0: 0},
        compiler_params=_params("arbitrary", "arbitrary"),
        name="moe_experts",
    )(x2d, xn, gate, gate_t, rank, rank_t, wg.astype(BF16), wu.astype(BF16), wd.astype(BF16), final_g.reshape(1, d))


def _layer(x, past, h0, lw, *, layer_idx, precise_ssm, last):
    b, t, d = x.shape
    d_ssm = lw["ssm_d"].shape[0]
    grouped = (b, t) if past is None else (1, b * t)
    group = lambda a: a.reshape(grouped + a.shape[2:])
    ungroup = lambda a: a.reshape((b, t) + a.shape[2:])
    kb = min(512, t) if past is None else None
    qt, k, v, iqt, ik, iwt, u, ga, gb, *key_layouts = _inproj(group(x), lw["norm_mix_g"], lw["w_in_packed"], d_ssm,
                                                             key_block=kb)
    k, v, ik, u = ungroup(k), ungroup(v), ungroup(ik), ungroup(u)

    if past is None:
        s_valid = t
    else:
        per_batch = lambda a: jnp.pad(a.reshape(a.shape[1], b, t).transpose(1, 0, 2),
                                      ((0, 0), (0, 0), (0, (-t) % V7X_LANES)))
        qt, iqt, iwt = per_batch(qt), per_batch(iqt), per_batch(iwt)
        ck, cv, cik = past
        k_all = jnp.concatenate([ck.reshape(b, -1, D_ATTN), k], axis=1)
        v_all = jnp.concatenate([cv.reshape(b, -1, D_ATTN), v], axis=1)
        ik_all = jnp.concatenate([cik, ik], axis=1)
        s_valid = k_all.shape[1]
        kb = 512
        pad = (-s_valid) % kb
        padf = lambda a: jnp.pad(a, ((0, 0), (0, pad), (0, 0)))
        key_layouts = _key_layouts(padf(k_all), padf(v_all), padf(ik_all), kb)
    att_t = _dsa(qt, iqt, iwt, *key_layouts, n_keep=min(TOPK_MAX, s_valid // 4), causal=past is None,
                 n_valid=s_valid)
    if past is not None:
        att_t = att_t[:, :, :t].transpose(1, 0, 2).reshape(1, D_ATTN, b * t)

    ssm, h_re, h_im = _ssm(u, h0[0], h0[1], lw["disc"], lw["ssm_b_re"], lw["ssm_b_im"], lw["ssm_c_re"],
                           lw["ssm_c_im"], lw["ssm_d"], lw["glu_w"], lw["glu_b"], precise=precise_ssm)
    x = _merge(att_t, group(ssm), ga, gb, group(x), lw["w_branch_attn"], lw["w_branch_ssm"], lw["w_out"])
    x2d = x.reshape(b * t, d)
    if layer_idx % 2 == 0:
        x2d = _ffn(x2d, lw["norm_ffn_g"], lw["ffn_w_gate"], lw["ffn_w_up"], lw["ffn_w_down"], lw["final_norm_g"],
                   final_norm=last)
    else:
        x2d = _moe(x2d, lw["norm_ffn_g"], lw["moe_router"], lw["moe_w_gate"], lw["moe_w_up"], lw["moe_w_down"],
                   lw["final_norm_g"], final_norm=last)
    g_shape = (b, -1, SSM_STATE)
    return (x2d.reshape(b, t, d), k.reshape(b, t, N_HEADS, HEAD_DIM), v.reshape(b, t, N_HEADS, HEAD_DIM), ik,
            h_re.reshape(g_shape), h_im.reshape(g_shape))


def kernel(x_prompt, x_sample, cache_k, cache_v, cache_idx_k, state_ssm_re, state_ssm_im, norm_mix_g, w_in,
           ssm_a_re, ssm_a_im, ssm_log_dt, ssm_b_re, ssm_b_im, ssm_c_re, ssm_c_im, ssm_d, glu_w, glu_b,
           w_branch_attn, w_branch_ssm, w_out, norm_ffn_g, ffn_w_gate, ffn_w_up, ffn_w_down, moe_router,
           moe_w_gate, moe_w_up, moe_w_down, final_norm_g):
    depth = w_in.shape[0]
    d_model = x_prompt.shape[-1]
    d_ssm = ssm_d.shape[1]
    xp, xs = x_prompt, x_sample
    outs_p, outs_s = [], []
    for l in range(depth):
        lw = dict(
            norm_mix_g=norm_mix_g[l], w_in_packed=_pack_w_in(w_in[l], d_ssm, d_model),
            disc=_ssm_discretize(ssm_a_re[l], ssm_a_im[l], ssm_log_dt[l]),
            ssm_b_re=ssm_b_re[l], ssm_b_im=ssm_b_im[l], ssm_c_re=ssm_c_re[l], ssm_c_im=ssm_c_im[l],
            ssm_d=ssm_d[l], glu_w=glu_w[l], glu_b=glu_b[l],
            w_branch_attn=w_branch_attn[l], w_branch_ssm=w_branch_ssm[l], w_out=w_out[l], norm_ffn_g=norm_ffn_g[l],
            final_norm_g=final_norm_g)
        i = l // 2
        if l % 2 == 0:
            lw.update(ffn_w_gate=ffn_w_gate[i], ffn_w_up=ffn_w_up[i], ffn_w_down=ffn_w_down[i])
        else:
            lw.update(moe_router=moe_router[i], moe_w_gate=moe_w_gate[i], moe_w_up=moe_w_up[i],
                      moe_w_down=moe_w_down[i])
        n_state = ssm_a_re.shape[1] * ssm_a_re.shape[2]
        zeros = jnp.zeros((xp.shape[0], n_state), F32)
        last = l == depth - 1
        xp, *rest_p = _layer(xp, None, (zeros, zeros), lw, layer_idx=l, precise_ssm=False, last=last)
        outs_p.append(rest_p)
        h0 = (state_ssm_re[l].reshape(xs.shape[0], n_state), state_ssm_im[l].reshape(xs.shape[0], n_state))
        xs, *rest_s = _layer(xs, (cache_k[l], cache_v[l], cache_idx_k[l]), h0, lw, layer_idx=l, precise_ssm=True,
                             last=last)
        outs_s.append(rest_s)

    stack = lambda outs, j: jnp.stack([o[j] for o in outs], 0)
    return (xp, xs,
            stack(outs_p, 0), stack(outs_p, 1), stack(outs_p, 2), stack(outs_p, 3), stack(outs_p, 4),
            stack(outs_s, 0), stack(outs_s, 1), stack(outs_s, 2), stack(outs_s, 3), stack(outs_s, 4))
```

```python
import functools
import math

import jax
import jax.numpy as jnp
from jax import lax
from jax.experimental import pallas as pl
from jax.experimental.pallas import tpu as pltpu

F32 = jnp.float32
BF16 = jnp.bfloat16
I32 = jnp.int32

CHUNK = 64
N_HEADS = 8
HEAD_DIM = 64
N_IDX_HEADS = 8
IDX_DIM = 64
TOPK_MAX = 256
SSM_GROUP = 16
SSM_STATE = 64
TOP_K_EXPERTS = 2
RMS_EPS = 1e-6
NEG_INF = -1e30
INT32_MIN = -(2 ** 31)

D_ATTN = N_HEADS * HEAD_DIM
D_IDX = N_IDX_HEADS * IDX_DIM
IDX_W_SCALE = float(D_IDX) ** -0.5
ATTN_SCALE = float(HEAD_DIM) ** -0.5

V7X_LANES = 128
V7X_BF16_SUBLANES = 16
_V_ROWS = HEAD_DIM + V7X_BF16_SUBLANES
V7X_VMEM_LIMIT_BYTES = 56 * 1024 * 1024


def _params(*semantics):
    return pltpu.CompilerParams(dimension_semantics=semantics, vmem_limit_bytes=V7X_VMEM_LIMIT_BYTES)


def _sigmoid(x):
    return 1.0 / (1.0 + jnp.exp(-x))


def _rms(x, g):
    return (x * lax.rsqrt(jnp.mean(x * x, axis=-1, keepdims=True) + RMS_EPS)) * g


def _const_spec(shape):
    return pl.BlockSpec(shape, lambda *_: (0,) * len(shape))


_SEG_Q, _SEG_K, _SEG_V, _SEG_IQ = 0, D_ATTN, 2 * D_ATTN, 3 * D_ATTN
_SEG_SMALL = 3 * D_ATTN + D_IDX


def _pack_w_in(w_in, d_ssm, d_model):
    offs = [0]
    for s in (D_ATTN, D_ATTN, D_ATTN, D_IDX, IDX_DIM, N_IDX_HEADS, d_ssm, d_model, d_model):
        offs.append(offs[-1] + s)
    pad = V7X_LANES - IDX_DIM - N_IDX_HEADS
    small = jnp.concatenate([w_in[:, offs[4]:offs[6]], jnp.zeros((w_in.shape[0], pad), w_in.dtype)], axis=1)
    return jnp.concatenate([w_in[:, :offs[4]], small, w_in[:, offs[6]:]], axis=1).astype(BF16)


def _inproj_body(x_ref, g_ref, w_ref, q_ref, k_ref, v_ref, iq_ref, ik_ref, iw_ref, u_ref, ga_ref, gb_ref, *key_refs,
                 d_ssm, d_model):
    hb = _rms(x_ref[...], g_ref[...]).astype(BF16)

    def proj(lo, n):
        return jnp.dot(hb, w_ref[:, lo:lo + n], preferred_element_type=F32)

    q_ref[...] = proj(_SEG_Q, D_ATTN).T.astype(BF16)
    k = proj(_SEG_K, D_ATTN)
    v = proj(_SEG_V, D_ATTN)
    k_ref[...] = k
    v_ref[...] = v
    iq_ref[...] = proj(_SEG_IQ, D_IDX).T.astype(BF16)
    small = proj(_SEG_SMALL, V7X_LANES)
    ik_ref[...] = small[:, :IDX_DIM]
    iw_ref[...] = small.T[IDX_DIM:IDX_DIM + N_IDX_HEADS, :] * IDX_W_SCALE
    if key_refs:
        k_hm_ref, vt_ref, ik_bf16_ref = key_refs
        vt = v.T
        for h in range(N_HEADS):
            cols = slice(h * HEAD_DIM, (h + 1) * HEAD_DIM)
            k_hm_ref[h] = k[:, cols].astype(BF16)
            vt_ref[h, :HEAD_DIM, :] = vt[cols, :].astype(BF16)
            vt_ref[h, HEAD_DIM:, :] = jnp.ones((vt_ref.shape[1] - HEAD_DIM, vt.shape[1]), BF16)
        ik_bf16_ref[...] = small[:, :IDX_DIM].astype(BF16)
    seg_u = _SEG_SMALL + V7X_LANES
    u_ref[...] = proj(seg_u, d_ssm)
    ga_ref[...] = proj(seg_u + d_ssm, d_model)
    gb_ref[...] = proj(seg_u + d_ssm + d_model, d_model)


def _inproj(x, g, w_packed, d_ssm, key_block=None):
    b, t, d = x.shape
    tm = min(512, t)
    assert t % tm == 0 and tm % V7X_LANES == 0
    n_w = w_packed.shape[1]

    def tok(n, dtype):
        return jax.ShapeDtypeStruct((b, t, n), dtype), pl.BlockSpec((None, tm, n), lambda bi, i: (bi, i, 0))

    def tok_t(n, dtype):
        return jax.ShapeDtypeStruct((b, n, t), dtype), pl.BlockSpec((None, n, tm), lambda bi, i: (bi, 0, i))

    shapes, specs = zip(
        tok_t(D_ATTN, BF16), tok(D_ATTN, F32), tok(D_ATTN, F32), tok_t(D_IDX, BF16), tok(IDX_DIM, F32),
        tok_t(N_IDX_HEADS, F32), tok(d_ssm, F32), tok(d, F32), tok(d, F32))
    if key_block is not None:
        assert key_block % tm == 0 and t % key_block == 0
        per_blk = key_block // tm
        shapes += (jax.ShapeDtypeStruct((b, N_HEADS, t, HEAD_DIM), BF16),
                   jax.ShapeDtypeStruct((b, t // key_block, N_HEADS, _V_ROWS, key_block), BF16),
                   jax.ShapeDtypeStruct((b, t, IDX_DIM), BF16))
        specs += (pl.BlockSpec((None, N_HEADS, tm, HEAD_DIM), lambda bi, i: (bi, 0, i, 0)),
                  pl.BlockSpec((None, None, N_HEADS, _V_ROWS, tm), lambda bi, i: (bi, i // per_blk, 0, 0, i % per_blk)),
                  pl.BlockSpec((None, tm, IDX_DIM), lambda bi, i: (bi, i, 0)))
    return pl.pallas_call(
        functools.partial(_inproj_body, d_ssm=d_ssm, d_model=d),
        out_shape=shapes,
        grid=(b, t // tm),
        in_specs=[pl.BlockSpec((None, tm, d), lambda bi, i: (bi, i, 0)), _const_spec((1, d)), _const_spec((d, n_w))],
        out_specs=specs,
        compiler_params=_params("parallel", "parallel"),
        name="inproj",
    )(x, g.reshape(1, d), w_packed)


def _dsa_body(qt_ref, iqt_ref, iwt_ref, k_ref, vt_ref, ik_ref, o_ref, key_ref, qs_ref, acc_ref, m_ref, s_ref, s2_ref, *,
              tq, kb, n_keep, causal, n_valid, n_kb_total, pos_bits):
    col0 = pl.program_id(1) * tq
    if causal:
        n_kb = (col0 + tq + kb - 1) // kb
        q_pos = col0 + lax.broadcasted_iota(I32, (1, tq), 1)
        shift = CHUNK.bit_length() - 1
        q_limit = ((q_pos >> shift) + 1) << shift
    else:
        n_kb = n_kb_total
        q_limit = jnp.full((1, tq), n_valid, I32)

    ks = min(V7X_LANES, kb)
    cr = min(64, kb)

    def key_pos(j, r0, rows):
        return j * kb + r0 + lax.broadcasted_iota(I32, (rows, tq), 0)

    iwt = iwt_ref[...]

    def idx_block(j, carry):
        for r0 in range(0, kb, ks):
            ik_tile = ik_ref[pl.ds(pl.multiple_of(j * kb + r0, ks), ks), :]
            s = None
            for h in range(N_IDX_HEADS):
                d = jnp.dot(ik_tile, iqt_ref[h * IDX_DIM:(h + 1) * IDX_DIM, :], preferred_element_type=F32)
                term = iwt[h:h + 1, :] * jnp.maximum(d, 0.0)
                s = term if s is None else s + term
            s = jnp.where(s == 0.0, 0.0, s)
            s = jnp.where(key_pos(j, r0, ks) < q_limit, s, NEG_INF)
            bits = lax.bitcast_convert_type(s, I32)
            key_ref[j, r0:r0 + ks, :] = jnp.where(bits < 0, bits ^ jnp.int32(0x7FFFFFFF), bits)
        return carry

    lax.fori_loop(0, n_kb, idx_block, 0)

    def count(pred):
        def body(j, accs):
            accs = list(accs)
            for n, r0 in enumerate(range(0, kb, cr)):
                hit = pred(key_ref[j, r0:r0 + cr, :], functools.partial(key_pos, j, r0, cr))
                accs[n % 2] = accs[n % 2] + jnp.where(hit, 1.0, 0.0)
            return tuple(accs)
        zero = jnp.zeros((cr, tq), F32)
        acc_a, acc_b = lax.fori_loop(0, n_kb, body, (zero, zero))
        return jnp.sum(acc_a + acc_b, axis=0, keepdims=True)

    def bit_step(i, thr):
        cand = thr + lax.shift_left(jnp.int32(1), 31 - i)
        cnt = count(lambda key, pos: key >= cand)
        return jnp.where(cnt >= n_keep, cand, thr)

    thr = lax.fori_loop(0, 32, bit_step, jnp.full((1, tq), INT32_MIN, I32))

    need = n_keep - count(lambda key, pos: key > thr)
    n_eq = count(lambda key, pos: key == thr)
    pos_all = jnp.int32(2 ** pos_bits - 1)

    def tie_search():
        def step(i, end):
            cand = end + lax.shift_left(jnp.int32(1), pos_bits - 1 - i)
            cnt = count(lambda key, pos: (key == thr) & (pos() < cand))
            return jnp.where(cnt <= need, cand, end)
        return lax.fori_loop(0, pos_bits, step, jnp.zeros((1, tq), I32))

    tie_end = lax.cond(jnp.max(n_eq - need) > 0.0, tie_search, lambda: jnp.full((1, tq), pos_all, I32))

    qs_ref[...] = qt_ref[...] * jnp.asarray(ATTN_SCALE, BF16)
    m_ref[...] = jnp.full(m_ref.shape, NEG_INF, F32)
    acc_ref[...] = jnp.zeros(acc_ref.shape, F32)

    def mask_bias(j, r0):
        key = key_ref[j, r0:r0 + ks, :]
        pos = key_pos(j, r0, ks)
        keep = (pos < q_limit) & ((key > thr) | ((key == thr) & (pos < tie_end)))
        return jnp.where(keep, 0.0, NEG_INF)

    v_rows = acc_ref.shape[1]

    def issue_scores(j, r0, buf):
        k_rows = pl.ds(pl.multiple_of(j * kb + r0, ks), ks)
        for h in range(N_HEADS):
            buf[h] = jnp.dot(k_ref[h, k_rows, :], qs_ref[h * HEAD_DIM:(h + 1) * HEAD_DIM, :],
                             preferred_element_type=F32)

    def att_block(j, carry):
        tiles = list(range(0, kb, ks))
        bufs = [s_ref, s2_ref]
        issue_scores(j, tiles[0], bufs[0])
        for i, r0 in enumerate(tiles):
            if i + 1 < len(tiles):
                issue_scores(j, tiles[i + 1], bufs[(i + 1) % 2])
            bias = mask_bias(j, r0)
            for h in range(N_HEADS):
                s = bufs[i % 2][h] + bias
                m_old = m_ref[h]
                m_new = jnp.maximum(m_old, jnp.max(s, axis=0, keepdims=True))
                alpha = jnp.exp(m_old - m_new)
                p = jnp.exp((s.reshape(ks // 8, 8, tq) - m_new[None]).reshape(ks, tq).astype(BF16))
                pv = jnp.dot(vt_ref[j, h, :, r0:r0 + ks], p, preferred_element_type=F32)
                acc = acc_ref[h].reshape(v_rows // 8, 8, tq) * alpha[None]
                acc_ref[h] = acc.reshape(v_rows, tq) + pv
                m_ref[h] = m_new
        return carry

    lax.fori_loop(0, n_kb, att_block, 0)
    for h in range(N_HEADS):
        rows = slice(h * HEAD_DIM, (h + 1) * HEAD_DIM)
        o_ref[rows, :] = (acc_ref[h, :HEAD_DIM, :] / acc_ref[h, HEAD_DIM:HEAD_DIM + 1, :]).astype(o_ref.dtype)


def _key_layouts(k, v, ik, kb):
    b, s_pad, _ = k.shape
    n_kb_total = s_pad // kb
    k_hm = k.astype(BF16).reshape(b, s_pad, N_HEADS, HEAD_DIM).transpose(0, 2, 1, 3)
    vt_blk = v.astype(BF16).reshape(b, n_kb_total, kb, N_HEADS, HEAD_DIM).transpose(0, 1, 3, 4, 2)
    ones = jnp.ones((b, n_kb_total, N_HEADS, _V_ROWS - HEAD_DIM, kb), BF16)
    return k_hm, jnp.concatenate([vt_blk, ones], axis=3), ik.astype(BF16)


def _dsa(qt, iqt, iwt, k_hm, vt_blk, ik, *, n_keep, causal, n_valid):
    b, _, t_pad = qt.shape
    n_kb_total, v_rows, kb = vt_blk.shape[1], vt_blk.shape[3], vt_blk.shape[4]
    s_pad = n_kb_total * kb
    tq = min(256, t_pad)
    assert t_pad % tq == 0 and tq % V7X_LANES == 0 and (not causal or tq % CHUNK == 0)
    body = functools.partial(_dsa_body, tq=tq, kb=kb, n_keep=float(n_keep), causal=causal, n_valid=n_valid,
                             n_kb_total=n_kb_total, pos_bits=s_pad.bit_length())
    qry = lambda n: pl.BlockSpec((None, n, tq), lambda bi, i: (bi, 0, i))
    return pl.pallas_call(
        body,
        out_shape=jax.ShapeDtypeStruct((b, D_ATTN, t_pad), BF16),
        grid=(b, t_pad // tq),
        in_specs=[qry(D_ATTN), qry(D_IDX), qry(N_IDX_HEADS),
                  pl.BlockSpec((None, N_HEADS, s_pad, HEAD_DIM), lambda bi, i: (bi, 0, 0, 0)),
                  pl.BlockSpec((None, n_kb_total, N_HEADS, v_rows, kb), lambda bi, i: (bi, 0, 0, 0, 0)),
                  pl.BlockSpec((None, s_pad, IDX_DIM), lambda bi, i: (bi, 0, 0))],
        out_specs=qry(D_ATTN),
        scratch_shapes=[pltpu.VMEM((n_kb_total, kb, tq), I32),
                        pltpu.VMEM((D_ATTN, tq), BF16),
                        pltpu.VMEM((N_HEADS, v_rows, tq), F32),
                        pltpu.VMEM((N_HEADS, 8, tq), F32),
                        pltpu.VMEM((N_HEADS, min(V7X_LANES, kb), tq), F32),
                        pltpu.VMEM((N_HEADS, min(V7X_LANES, kb), tq), F32)],
        compiler_params=_params("parallel", "arbitrary"),
        name="dsa",
    )(qt, iqt, iwt, k_hm, vt_blk, ik)


def _ssm_discretize_body(a_re_ref, a_im_ref, log_dt_ref, ab_re_ref, ab_im_ref, bc_re_ref, bc_im_ref):
    a_re, a_im = a_re_ref[...], a_im_ref[...]
    dt = jnp.exp(log_dt_ref[...])
    mag = jnp.exp(dt * a_re)
    ab_re = mag * jnp.cos(dt * a_im)
    ab_im = mag * jnp.sin(dt * a_im)
    den = a_re * a_re + a_im * a_im
    nr = ab_re - 1.0
    ni = ab_im
    ab_re_ref[...] = ab_re
    ab_im_ref[...] = ab_im
    bc_re_ref[...] = (nr * a_re + ni * a_im) / den
    bc_im_ref[...] = (ni * a_re - nr * a_im) / den


def _ssm_discretize(a_re, a_im, log_dt):
    g, p = a_re.shape
    out = jax.ShapeDtypeStruct((g, p), F32)
    return pl.pallas_call(_ssm_discretize_body, out_shape=(out,) * 4, name="ssm_discretize")(
        a_re, a_im, log_dt.reshape(g, 1))


def _block_diag_halves(w, dtype):
    g, r, c = w.shape
    gh = g // 2
    eye = jnp.eye(gh, dtype=w.dtype)
    halves = [jnp.einsum("grc,gh->grhc", w[i * gh:(i + 1) * gh], eye).reshape(gh * r, gh * c) for i in range(2)]
    return jnp.stack(halves, 0).astype(dtype)


def _ssm_body(u_ref, h0r_ref, h0i_ref, abr_ref, abi_ref, bcr_ref, bci_ref, br_ref, bi_ref, cr_ref, ci_ref,
              d_ref, gw_ref, gb_ref, y_ref, hr_ref, hi_ref, xr_ref, xi_ref, *, tb, nb, precise):
    d_ssm = u_ref.shape[-1]
    n_state = hr_ref.shape[-1]
    kh, sh = d_ssm // 2, n_state // 2

    def mm(a, w):
        if precise:
            return jnp.dot(a, w, preferred_element_type=F32, precision=lax.Precision.HIGHEST)
        return jnp.dot(a.astype(BF16), w, preferred_element_type=F32)

    @pl.when(pl.program_id(0) == 0)
    def _():
        hr_ref[...] = h0r_ref[...]
        hi_ref[...] = h0i_ref[...]

    u = pltpu.einshape("btc->tbc", u_ref[...]).reshape(tb * nb, d_ssm)

    for half in range(2):
        cols = slice(half * sh, (half + 1) * sh)
        uh = u[:, half * kh:(half + 1) * kh]
        bu_r = mm(uh, br_ref[half])
        bu_i = mm(uh, bi_ref[half])
        bc_r, bc_i = bcr_ref[:, cols], bci_ref[:, cols]
        xr_ref[:, cols] = bc_r * bu_r - bc_i * bu_i
        xi_ref[:, cols] = bc_r * bu_i + bc_i * bu_r

    slab = min(512, n_state)
    for s0 in range(0, n_state, slab):
        cols = slice(s0, s0 + slab)
        a_r = jnp.broadcast_to(abr_ref[:, cols], (nb, slab))
        a_i = jnp.broadcast_to(abi_ref[:, cols], (nb, slab))

        def step(t, h):
            h_r, h_i = h
            rows = pl.ds(pl.multiple_of(t * nb, nb), nb)
            n_r = a_r * h_r - a_i * h_i + xr_ref[rows, cols]
            n_i = a_r * h_i + a_i * h_r + xi_ref[rows, cols]
            xr_ref[rows, cols] = n_r
            xi_ref[rows, cols] = n_i
            return n_r, n_i

        h_r, h_i = lax.fori_loop(0, tb, step, (hr_ref[:, cols], hi_ref[:, cols]), unroll=min(8, tb))
        hr_ref[:, cols] = h_r
        hi_ref[:, cols] = h_i

    ys = []
    for half in range(2):
        cols = slice(half * sh, (half + 1) * sh)
        ys.append(mm(xr_ref[:, cols], cr_ref[half]) - mm(xi_ref[:, cols], ci_ref[half]))
    y = jnp.concatenate(ys, axis=1) + d_ref[...] * u
    y = y * (0.5 * (1.0 + jnp.tanh(math.sqrt(2.0 / math.pi) * (y + 0.044715 * (y * y * y)))))
    z = mm(y, gw_ref[...]) + gb_ref[...]
    y_ref[...] = pltpu.einshape("tbc->btc", (y * _sigmoid(z)).reshape(tb, nb, d_ssm))


def _ssm(u, h0_re, h0_im, disc, b_re, b_im, c_re, c_im, d, glu_w, glu_b, *, precise):
    nb, t, d_ssm = u.shape
    n_state = h0_re.shape[1]
    assert nb % 8 == 0
    tb = min(32, t)
    wdt = F32 if precise else BF16
    br = _block_diag_halves(jnp.swapaxes(b_re, 1, 2), wdt)
    bi = _block_diag_halves(jnp.swapaxes(b_im, 1, 2), wdt)
    cr = _block_diag_halves(jnp.swapaxes(c_re, 1, 2), wdt)
    ci = _block_diag_halves(jnp.swapaxes(c_im, 1, 2), wdt)
    row = lambda a: a.reshape(1, -1).astype(F32)
    state_spec = _const_spec((nb, n_state))
    vec_state = _const_spec((1, n_state))
    vec_ch = _const_spec((1, d_ssm))
    return pl.pallas_call(
        functools.partial(_ssm_body, tb=tb, nb=nb, precise=precise),
        out_shape=(jax.ShapeDtypeStruct((nb, t, d_ssm), F32),
                   jax.ShapeDtypeStruct((nb, n_state), F32), jax.ShapeDtypeStruct((nb, n_state), F32)),
        grid=(t // tb,),
        in_specs=[pl.BlockSpec((nb, tb, d_ssm), lambda i: (0, i, 0)), state_spec, state_spec,
                  vec_state, vec_state, vec_state, vec_state,
                  _const_spec(br.shape), _const_spec(bi.shape), _const_spec(cr.shape), _const_spec(ci.shape),
                  vec_ch, _const_spec((d_ssm, d_ssm)), vec_ch],
        out_specs=(pl.BlockSpec((nb, tb, d_ssm), lambda i: (0, i, 0)), state_spec, state_spec),
        scratch_shapes=[pltpu.VMEM((tb * nb, n_state), F32), pltpu.VMEM((tb * nb, n_state), F32)],
        compiler_params=_params("arbitrary"),
        name="ssm",
    )(u, h0_re, h0_im, row(disc[0]), row(disc[1]), row(disc[2]), row(disc[3]), br, bi, cr, ci,
      row(d), glu_w.astype(wdt), row(glu_b))


def _merge_body(att_t_ref, ssm_ref, ga_ref, gb_ref, x_ref, wpa_ref, wpb_ref, wo_ref, o_ref):
    att = att_t_ref[...].astype(F32).T.astype(BF16)
    a = jnp.dot(att, wpa_ref[...], preferred_element_type=F32)
    s = jnp.dot(ssm_ref[...].astype(BF16), wpb_ref[...], preferred_element_type=F32)
    mix = _sigmoid(ga_ref[...]) * a + _sigmoid(gb_ref[...]) * s
    o_ref[...] = x_ref[...] + jnp.dot(mix.astype(BF16), wo_ref[...], preferred_element_type=F32)


def _merge(att_t, ssm, ga, gb, x, w_pa, w_pb, w_o):
    b, t, d = x.shape
    tm = min(512, t)
    assert t % tm == 0 and tm % V7X_LANES == 0
    tok = lambda n: pl.BlockSpec((None, tm, n), lambda bi, i: (bi, i, 0))
    d_ssm = ssm.shape[-1]
    return pl.pallas_call(
        _merge_body,
        out_shape=jax.ShapeDtypeStruct((b, t, d), F32),
        grid=(b, t // tm),
        in_specs=[pl.BlockSpec((None, D_ATTN, tm), lambda bi, i: (bi, 0, i)), tok(d_ssm), tok(d), tok(d), tok(d),
                  _const_spec(w_pa.shape), _const_spec(w_pb.shape), _const_spec(w_o.shape)],
        out_specs=tok(d),
        compiler_params=_params("parallel", "parallel"),
        name="merge",
    )(att_t, ssm, ga, gb, x, w_pa.astype(BF16), w_pb.astype(BF16), w_o.astype(BF16))


def _ffn_body(x_ref, g_ref, wg_ref, wu_ref, wd_ref, fg_ref, o_ref, xn_ref, *, n_f, final_norm):
    f = pl.program_id(1)

    @pl.when(f == 0)
    def _():
        xn_ref[...] = _rms(x_ref[...], g_ref[...]).astype(BF16)
        o_ref[...] = x_ref[...]

    xn = xn_ref[...]
    hg = jnp.dot(xn, wg_ref[...], preferred_element_type=F32)
    hu = jnp.dot(xn, wu_ref[...], preferred_element_type=F32)
    h = (hg * _sigmoid(hg)) * hu
    o_ref[...] += jnp.dot(h.astype(BF16), wd_ref[...], preferred_element_type=F32)
    if final_norm:
        @pl.when(f == n_f - 1)
        def _():
            o_ref[...] = _rms(o_ref[...], fg_ref[...])


def _ffn(x2d, g, wg, wu, wd, final_g, *, final_norm):
    n, d = x2d.shape
    d_ff = wg.shape[1]
    tm = min(512, n)
    fc = d_ff
    for cand in (1408, 1024, 896, 768, 512):
        if d_ff % cand == 0:
            fc = cand
            break
    n_f = d_ff // fc
    return pl.pallas_call(
        functools.partial(_ffn_body, n_f=n_f, final_norm=final_norm),
        out_shape=jax.ShapeDtypeStruct((n, d), F32),
        grid=(n // tm, n_f),
        in_specs=[pl.BlockSpec((tm, d), lambda i, f: (i, 0)), _const_spec((1, d)),
                  pl.BlockSpec((d, fc), lambda i, f: (0, f)), pl.BlockSpec((d, fc), lambda i, f: (0, f)),
                  pl.BlockSpec((fc, d), lambda i, f: (f, 0)), _const_spec((1, d))],
        out_specs=pl.BlockSpec((tm, d), lambda i, f: (i, 0)),
        scratch_shapes=[pltpu.VMEM((tm, d), BF16)],
        compiler_params=_params("parallel", "arbitrary"),
        name="ffn",
    )(x2d, g.reshape(1, d), wg.astype(BF16), wu.astype(BF16), wd.astype(BF16), final_g.reshape(1, d))


def _moe_route_body(x_ref, g_ref, r_ref, xn_ref, gate_ref, gate_t_ref, rank_ref, rank_t_ref, *, n_exp, tm):
    xn = _rms(x_ref[...], g_ref[...])
    xn_ref[...] = xn.astype(BF16)
    logits = jnp.dot(xn, r_ref[...], preferred_element_type=F32, precision=lax.Precision.HIGHEST)
    lane = lax.broadcasted_iota(I32, logits.shape, 1).astype(F32)
    logits = jnp.where(lane < n_exp, logits, NEG_INF)
    big = float(logits.shape[1])
    m1 = jnp.max(logits, axis=1, keepdims=True)
    i1 = jnp.min(jnp.where(logits == m1, lane, big), axis=1, keepdims=True)
    rest = jnp.where(lane == i1, NEG_INF, logits)
    m2 = jnp.max(rest, axis=1, keepdims=True)
    i2 = jnp.min(jnp.where(rest == m2, lane, big), axis=1, keepdims=True)
    e2 = jnp.exp(m2 - m1)
    w1 = 1.0 / (1.0 + e2)
    w2 = e2 / (1.0 + e2)
    gates = jnp.where(lane == i1, w1, 0.0) + jnp.where(lane == i2, w2, 0.0)
    gates_t = gates.T
    gate_ref[...] = gates
    gate_t_ref[...] = gates_t
    t_row = lax.broadcasted_iota(I32, (tm, tm), 0)
    t_col = lax.broadcasted_iota(I32, (tm, tm), 1)
    member = jnp.where(gates > 0.0, 1.0, 0.0).astype(BF16)
    member_t = jnp.where(gates_t > 0.0, 1.0, 0.0).astype(BF16)
    rank_ref[...] = jnp.dot(jnp.where(t_col < t_row, 1.0, 0.0).astype(BF16), member, preferred_element_type=F32)
    rank_t_ref[...] = jnp.dot(member_t, jnp.where(t_row < t_col, 1.0, 0.0).astype(BF16),
                              preferred_element_type=F32)


def _moe_expert_body(acc_ref, xn_ref, gate_ref, gate_t_ref, rank_ref, rank_t_ref, wg_ref, wu_ref, wd_ref, fg_ref,
                     o_ref, *, fc, tm, rows_per_chunk, single_tile, final_norm):
    e = pl.program_id(0)
    c = rows_per_chunk
    d_ff = wg_ref.shape[1]
    if single_tile:
        @pl.when(e == 0)
        def _():
            o_ref[...] = acc_ref[...]
    else:
        o_ref[...] = acc_ref[...]

    gate_row = gate_t_ref[pl.ds(e, 1), :]
    rank_row = rank_t_ref[pl.ds(e, 1), :]
    routed_row = gate_row > 0.0
    n_routed = jnp.sum(jnp.where(routed_row, 1.0, 0.0)).astype(I32)
    n_chunks = (n_routed + (c - 1)) // c

    lane = lax.broadcasted_iota(I32, gate_ref.shape, 1)
    rank_col = jnp.sum(jnp.where(lane == e, rank_ref[...], 0.0), axis=1, keepdims=True)
    routed_col = jnp.sum(jnp.where(lane == e, gate_ref[...], 0.0), axis=1, keepdims=True) > 0.0

    def chunk(ci, carry):
        slot_rows = (ci * c + lax.broadcasted_iota(I32, (c, tm), 0)).astype(F32)
        pack = routed_row & (rank_row == slot_rows)
        xs = jnp.dot(jnp.where(pack, 1.0, 0.0).astype(BF16), xn_ref[...], preferred_element_type=F32).astype(BF16)
        y = None
        for f0 in range(0, d_ff, fc):
            hg = jnp.dot(xs, wg_ref[:, f0:f0 + fc], preferred_element_type=F32)
            hu = jnp.dot(xs, wu_ref[:, f0:f0 + fc], preferred_element_type=F32)
            h = ((hg * _sigmoid(hg)) * hu).astype(BF16)
            part = jnp.dot(h, wd_ref[f0:f0 + fc, :], preferred_element_type=F32)
            y = part if y is None else y + part
        y = y * jnp.sum(jnp.where(pack, gate_row, 0.0), axis=1, keepdims=True)
        y_hi = y.astype(BF16)
        y_lo = (y - y_hi.astype(F32)).astype(BF16)
        slot_cols = (ci * c + lax.broadcasted_iota(I32, (tm, c), 1)).astype(F32)
        unpack = jnp.where(routed_col & (rank_col == slot_cols), 1.0, 0.0).astype(BF16)
        o_ref[...] += jnp.dot(jnp.concatenate([unpack, unpack], axis=1), jnp.concatenate([y_hi, y_lo], axis=0),
                              preferred_element_type=F32)
        return carry

    lax.fori_loop(0, n_chunks, chunk, 0)
    if final_norm:
        @pl.when(e == pl.num_programs(0) - 1)
        def _():
            o_ref[...] = _rms(o_ref[...], fg_ref[...])


def _moe(x2d, g, router, wg, wu, wd, final_g, *, final_norm):
    n, d = x2d.shape
    n_exp, _, d_ff = wg.shape
    assert TOP_K_EXPERTS == 2 and n_exp <= V7X_LANES
    tm = min(1024, n)
    rows_per_chunk = min(V7X_LANES, tm)
    assert n % tm == 0 and tm % V7X_LANES == 0
    n_tiles = n // tm
    assert n_tiles == 1 or n_tiles >= 4
    fc = d_ff
    for cand in (896, 1024, 768, 512):
        if d_ff % cand == 0:
            fc = cand
            break
    r_pad = jnp.concatenate([router, jnp.zeros((d, V7X_LANES - n_exp), router.dtype)], axis=1)
    tok = lambda width: pl.BlockSpec((tm, width), lambda i: (i, 0))
    per_tile = pl.BlockSpec((None, V7X_LANES, tm), lambda i: (i, 0, 0))
    xn, gate, gate_t, rank, rank_t = pl.pallas_call(
        functools.partial(_moe_route_body, n_exp=n_exp, tm=tm),
        out_shape=(jax.ShapeDtypeStruct((n, d), BF16),
                   jax.ShapeDtypeStruct((n, V7X_LANES), F32), jax.ShapeDtypeStruct((n_tiles, V7X_LANES, tm), F32),
                   jax.ShapeDtypeStruct((n, V7X_LANES), F32), jax.ShapeDtypeStruct((n_tiles, V7X_LANES, tm), F32)),
        grid=(n_tiles,),
        in_specs=[tok(d), _const_spec((1, d)), _const_spec((d, V7X_LANES))],
        out_specs=(tok(d), tok(V7X_LANES), per_tile, tok(V7X_LANES), per_tile),
        compiler_params=_params("parallel"),
        name="moe_route",
    )(x2d, g.reshape(1, d), r_pad)

    tok2 = lambda width: pl.BlockSpec((tm, width), lambda e, i: (i, 0))
    per_tile2 = pl.BlockSpec((None, V7X_LANES, tm), lambda e, i: (i, 0, 0))
    w_spec = lambda rows, cols: pl.BlockSpec((None, rows, cols), lambda e, i: (e, 0, 0), pipeline_mode=pl.Buffered(1))
    return pl.pallas_call(
        functools.partial(_moe_expert_body, fc=fc, tm=tm, rows_per_chunk=rows_per_chunk, single_tile=n_tiles == 1,
                          final_norm=final_norm),
        out_shape=jax.ShapeDtypeStruct((n, d), F32),
        grid=(n_exp, n_tiles),
        in_specs=[tok2(d), tok2(d), tok2(V7X_LANES), per_tile2, tok2(V7X_LANES), per_tile2,
                  w_spec(d, d_ff), w_spec(d, d_ff), w_spec(d_ff, d), _const_spec((1, d))],
        out_specs=tok2(d),
        input_output_aliases={0: 0},
        compiler_params=_params("arbitrary", "arbitrary"),
        name="moe_experts",
    )(x2d, xn, gate, gate_t, rank, rank_t, wg.astype(BF16), wu.astype(BF16), wd.astype(BF16), final_g.reshape(1, d))


def _layer(x, past, h0, lw, *, layer_idx, precise_ssm, last):
    b, t, d = x.shape
    d_ssm = lw["ssm_d"].shape[0]
    grouped = (b, t) if past is None else (1, b * t)
    group = lambda a: a.reshape(grouped + a.shape[2:])
    ungroup = lambda a: a.reshape((b, t) + a.shape[2:])
    kb = min(512, t) if past is None else None
    qt, k, v, iqt, ik, iwt, u, ga, gb, *key_layouts = _inproj(group(x), lw["norm_mix_g"], lw["w_in_packed"], d_ssm,
                                                             key_block=kb)
    k, v, ik, u = ungroup(k), ungroup(v), ungroup(ik), ungroup(u)

    if past is None:
        s_valid = t
    else:
        per_batch = lambda a: jnp.pad(a.reshape(a.shape[1], b, t).transpose(1, 0, 2),
                                      ((0, 0), (0, 0), (0, (-t) % V7X_LANES)))
        qt, iqt, iwt = per_batch(qt), per_batch(iqt), per_batch(iwt)
        ck, cv, cik = past
        k_all = jnp.concatenate([ck.reshape(b, -1, D_ATTN), k], axis=1)
        v_all = jnp.concatenate([cv.reshape(b, -1, D_ATTN), v], axis=1)
        ik_all = jnp.concatenate([cik, ik], axis=1)
        s_valid = k_all.shape[1]
        kb = 512
        pad = (-s_valid) % kb
        padf = lambda a: jnp.pad(a, ((0, 0), (0, pad), (0, 0)))
        key_layouts = _key_layouts(padf(k_all), padf(v_all), padf(ik_all), kb)
    att_t = _dsa(qt, iqt, iwt, *key_layouts, n_keep=min(TOPK_MAX, s_valid // 4), causal=past is None,
                 n_valid=s_valid)
    if past is not None:
        att_t = att_t[:, :, :t].transpose(1, 0, 2).reshape(1, D_ATTN, b * t)

    ssm, h_re, h_im = _ssm(u, h0[0], h0[1], lw["disc"], lw["ssm_b_re"], lw["ssm_b_im"], lw["ssm_c_re"],
                           lw["ssm_c_im"], lw["ssm_d"], lw["glu_w"], lw["glu_b"], precise=precise_ssm)
    x = _merge(att_t, group(ssm), ga, gb, group(x), lw["w_branch_attn"], lw["w_branch_ssm"], lw["w_out"])
    x2d = x.reshape(b * t, d)
    if layer_idx % 2 == 0:
        x2d = _ffn(x2d, lw["norm_ffn_g"], lw["ffn_w_gate"], lw["ffn_w_up"], lw["ffn_w_down"], lw["final_norm_g"],
                   final_norm=last)
    else:
        x2d = _moe(x2d, lw["norm_ffn_g"], lw["moe_router"], lw["moe_w_gate"], lw["moe_w_up"], lw["moe_w_down"],
                   lw["final_norm_g"], final_norm=last)
    g_shape = (b, -1, SSM_STATE)
    return (x2d.reshape(b, t, d), k.reshape(b, t, N_HEADS, HEAD_DIM), v.reshape(b, t, N_HEADS, HEAD_DIM), ik,
            h_re.reshape(g_shape), h_im.reshape(g_shape))


def kernel(x_prompt, x_sample, cache_k, cache_v, cache_idx_k, state_ssm_re, state_ssm_im, norm_mix_g, w_in,
           ssm_a_re, ssm_a_im, ssm_log_dt, ssm_b_re, ssm_b_im, ssm_c_re, ssm_c_im, ssm_d, glu_w, glu_b,
           w_branch_attn, w_branch_ssm, w_out, norm_ffn_g, ffn_w_gate, ffn_w_up, ffn_w_down, moe_router,
           moe_w_gate, moe_w_up, moe_w_down, final_norm_g):
    depth = w_in.shape[0]
    d_model = x_prompt.shape[-1]
    d_ssm = ssm_d.shape[1]
    xp, xs = x_prompt, x_sample
    outs_p, outs_s = [], []
    for l in range(depth):
        lw = dict(
            norm_mix_g=norm_mix_g[l], w_in_packed=_pack_w_in(w_in[l], d_ssm, d_model),
            disc=_ssm_discretize(ssm_a_re[l], ssm_a_im[l], ssm_log_dt[l]),
            ssm_b_re=ssm_b_re[l], ssm_b_im=ssm_b_im[l], ssm_c_re=ssm_c_re[l], ssm_c_im=ssm_c_im[l],
            ssm_d=ssm_d[l], glu_w=glu_w[l], glu_b=glu_b[l],
            w_branch_attn=w_branch_attn[l], w_branch_ssm=w_branch_ssm[l], w_out=w_out[l], norm_ffn_g=norm_ffn_g[l],
            final_norm_g=final_norm_g)
        i = l // 2
        if l % 2 == 0:
            lw.update(ffn_w_gate=ffn_w_gate[i], ffn_w_up=ffn_w_up[i], ffn_w_down=ffn_w_down[i])
        else:
            lw.update(moe_router=moe_router[i], moe_w_gate=moe_w_gate[i], moe_w_up=moe_w_up[i],
                      moe_w_down=moe_w_down[i])
        n_state = ssm_a_re.shape[1] * ssm_a_re.shape[2]
        zeros = jnp.zeros((xp.shape[0], n_state), F32)
        last = l == depth - 1
        xp, *rest_p = _layer(xp, None, (zeros, zeros), lw, layer_idx=l, precise_ssm=False, last=last)
        outs_p.append(rest_p)
        h0 = (state_ssm_re[l].reshape(xs.shape[0], n_state), state_ssm_im[l].reshape(xs.shape[0], n_state))
        xs, *rest_s = _layer(xs, (cache_k[l], cache_v[l], cache_idx_k[l]), h0, lw, layer_idx=l, precise_ssm=True,
                             last=last)
        outs_s.append(rest_s)

    stack = lambda outs, j: jnp.stack([o[j] for o in outs], 0)
    return (xp, xs,
            stack(outs_p, 0), stack(outs_p, 1), stack(outs_p, 2), stack(outs_p, 3), stack(outs_p, 4),
            stack(outs_s, 0), stack(outs_s, 1), stack(outs_s, 2), stack(outs_s, 3), stack(outs_s, 4))
```
